```python
import jax, jax.numpy as jnp
from jax import lax
import numpy as np

D_MODEL = 2048
BATCH = 2
SEQ = 4096
DEPTH = 2

GRID_W = 64
CTX_LEN = 256
EPS = 1e-6
GLA_WIDTH = D_MODEL // 2
POOL_WIDTH = D_MODEL - GLA_WIDTH
GLA_HEADS = 4
GLA_DV = GLA_WIDTH // GLA_HEADS
GLA_DK = GLA_DV // 2
GLA_KEY_WIDTH = GLA_HEADS * GLA_DK
GATE_RANK = 16
GATE_TAU = 16.0
CHUNK = 64
POOL_WINDOWS = (2, 4, 8, 16)
POOL_GROUPS = len(POOL_WINDOWS)
POOL_GW = POOL_WIDTH // POOL_GROUPS
IN_WIDTH = 2 * GLA_KEY_WIDTH + 2 * GLA_WIDTH + POOL_WIDTH + 2 * GATE_RANK
SPLITS = (GLA_KEY_WIDTH, 2 * GLA_KEY_WIDTH, 2 * GLA_KEY_WIDTH + GLA_WIDTH,
          2 * GLA_KEY_WIDTH + 2 * GLA_WIDTH, 2 * GLA_KEY_WIDTH + 2 * GLA_WIDTH + POOL_WIDTH)
D_FF = ((8 * D_MODEL // 3 + 255) // 256) * 256
N_EXPERTS = 8
TOP_K = 2
EXPERT_FF = D_FF
N_DENSE = (DEPTH + 1) // 2
N_MOE = DEPTH // 2

kernel_name = 'hybrid_gla_pool_moe_dit'


def rmsnorm(x, g):
    xf = x.astype(jnp.float32)
    xf = xf * lax.rsqrt(jnp.mean(xf * xf, axis=-1, keepdims=True) + EPS)
    return (xf * g.astype(jnp.float32)).astype(x.dtype)


def to_heads(t, d):
    b, l, _ = t.shape
    return t.reshape(b, l, -1, d).transpose(0, 2, 1, 3)


def gla_chunked(q, k, v, log_a, s0):
    bsz, nh, L, _ = q.shape
    dv = v.shape[-1]
    nc = L // CHUNK

    def blocks(t):
        return t.astype(jnp.float32).reshape(bsz, nh, nc, CHUNK, t.shape[-1]).transpose(2, 0, 1, 3, 4)

    lower = jnp.tril(jnp.ones((CHUNK, CHUNK), dtype=bool))[:, :, None]

    def step(state, blk):
        qb, kb, vb, ab = blk
        cum = jnp.cumsum(ab, axis=2)
        o_inter = jnp.einsum('bhtd,bhdv->bhtv', qb * jnp.exp(cum), state)
        diff = cum[:, :, :, None, :] - cum[:, :, None, :, :]
        decay = jnp.where(lower, jnp.exp(jnp.minimum(diff, 0.0)), 0.0)
        scores = jnp.einsum('bhtd,bhsd,bhtsd->bhts', qb, kb, decay)
        o_intra = jnp.einsum('bhts,bhsv->bhtv', scores, vb)
        last = cum[:, :, -1:, :]
        new_state = (jnp.exp(last[:, :, 0, :])[..., None] * state
                     + jnp.einsum('bhsd,bhsv->bhdv', kb * jnp.exp(last - cum), vb))
        return new_state, o_inter + o_intra

    s_fin, o = lax.scan(step, s0, (blocks(q), blocks(k), blocks(v), blocks(log_a)))
    o = o.transpose(1, 2, 0, 3, 4).reshape(bsz, nh, L, dv)
    return o, s_fin


def gla_bidir(q, k, v, lg_f, lg_b, s0_f, s0_b):
    o_f, s_f = gla_chunked(q, k, v, lg_f, s0_f)
    flip = lambda t: jnp.flip(t, axis=2)
    o_b, s_b = gla_chunked(flip(q), flip(k), flip(v), flip(lg_b), s0_b)
    return o_f + flip(o_b), s_f, s_b


def gla_readout(o, g, gain):
    bsz, _, L, _ = o.shape
    o = o * lax.rsqrt(jnp.mean(o * o, axis=-1, keepdims=True) + EPS)
    o = o.transpose(0, 2, 1, 3).reshape(bsz, L, GLA_WIDTH)
    return (o * gain.astype(jnp.float32) * jax.nn.silu(g.astype(jnp.float32))).astype(g.dtype)


def centred_pool_minus_self(u, window):
    L = u.shape[2]
    uf = u.astype(jnp.float32)
    cs = jnp.concatenate([jnp.zeros_like(uf[:, :, :1]), jnp.cumsum(uf, axis=2)], axis=2)
    j = np.arange(L)
    lo = np.clip(j - window // 2, 0, L)
    hi = np.clip(j - window // 2 + window, 0, L)
    win_sum = jnp.take(cs, hi, axis=2) - jnp.take(cs, lo, axis=2)
    cnt = jnp.asarray(hi - lo, dtype=jnp.float32)[:, None]
    return (win_sum / cnt - uf).astype(u.dtype)


def pool_mix(p, w_pool_l, scale_l, rows):
    bsz, L, _ = p.shape
    pg = p.reshape(bsz, rows, L // rows, POOL_GROUPS, POOL_GW)
    mixed = jnp.stack([centred_pool_minus_self(pg[..., gi, :], w) for gi, w in enumerate(POOL_WINDOWS)], axis=3)
    out = jnp.einsum('brwgc,gcd->brwgd', mixed, w_pool_l)
    return out.reshape(bsz, L, POOL_WIDTH) * scale_l


def project(h, w_in_l, w_gate_l, b_gate_l):
    u = h @ w_in_l
    q, k, v, g, p, r = jnp.split(u, SPLITS, axis=-1)
    q = to_heads(q, GLA_DK) * (GLA_DK ** -0.5)
    k = to_heads(k, GLA_DK)
    v = to_heads(v, GLA_DV)
    lgs = []
    for d in range(2):
        pre = r[..., d * GATE_RANK:(d + 1) * GATE_RANK] @ w_gate_l[d] + b_gate_l[d]
        lgs.append(to_heads(jax.nn.log_sigmoid(pre.astype(jnp.float32)) / GATE_TAU, GLA_DK))
    return q, k, v, g, p, lgs[0], lgs[1]


def swiglu(h, w1, w3, w2):
    return (jax.nn.silu(h @ w1) * (h @ w3)) @ w2


def moe_swiglu(h, w_r, b_r, w1, w3, w2):
    logits = (h @ w_r).astype(jnp.float32) + b_r.astype(jnp.float32)
    probs = jax.nn.softmax(logits, axis=-1)
    top_p, top_i = lax.top_k(probs, TOP_K)
    top_p = top_p / jnp.sum(top_p, axis=-1, keepdims=True)
    gates = jnp.sum(jax.nn.one_hot(top_i, N_EXPERTS, dtype=jnp.float32) * top_p[..., None], axis=-2)
    gates = gates.astype(h.dtype)
    y = jnp.zeros_like(h)
    for e in range(N_EXPERTS):
        y = y + gates[..., e:e + 1] * swiglu(h, w1[e], w3[e], w2[e])
    return y


def setup_inputs(seed: int = 0) -> dict:
    key = jax.random.key(seed)
    ks = jax.random.split(key, 24)
    f32 = jnp.float32
    D = D_MODEL

    def nrm(k, shape, scale):
        return jax.random.normal(k, shape, f32) * scale

    return {
        'x': nrm(ks[0], (BATCH, SEQ, D), 1.0),
        'c': nrm(ks[1], (BATCH, D), 1.0),
        'ctx': nrm(ks[2], (BATCH, CTX_LEN, D), 1.0),
        'c_ctx': nrm(ks[3], (D,), 1.0),
        'w_ada': nrm(ks[4], (DEPTH, D, 6 * D), 0.5 * D ** -0.5),
        'b_ada': nrm(ks[5], (DEPTH, 6 * D), 0.02),
        'norm1_g': 1.0 + nrm(ks[6], (DEPTH, D), 0.02),
        'norm2_g': 1.0 + nrm(ks[7], (DEPTH, D), 0.02),
        'w_in': nrm(ks[8], (DEPTH, D, IN_WIDTH), D ** -0.5),
        'w_gate': nrm(ks[9], (DEPTH, 2, GATE_RANK, GLA_KEY_WIDTH), GATE_RANK ** -0.5),
        'b_gate': nrm(ks[10], (DEPTH, 2, GLA_KEY_WIDTH), 0.5),
        'gla_norm_g': 1.0 + nrm(ks[11], (DEPTH, GLA_WIDTH), 0.02),
        'w_pool': nrm(ks[12], (DEPTH, POOL_GROUPS, POOL_GW, POOL_GW), POOL_GW ** -0.5),
        'pool_scale': 1.0 + nrm(ks[13], (DEPTH, POOL_WIDTH), 0.1),
        'w_out': nrm(ks[14], (DEPTH, D, D), D ** -0.5),
        'ffn_w1': nrm(ks[15], (N_DENSE, D, D_FF), D ** -0.5),
        'ffn_w3': nrm(ks[16], (N_DENSE, D, D_FF), D ** -0.5),
        'ffn_w2': nrm(ks[17], (N_DENSE, D_FF, D), D_FF ** -0.5),
        'moe_w_router': nrm(ks[18], (N_MOE, D, N_EXPERTS), D ** -0.5),
        'moe_b_router': nrm(ks[19], (N_MOE, N_EXPERTS), 0.01),
        'moe_w1': nrm(ks[20], (N_MOE, N_EXPERTS, D, EXPERT_FF), D ** -0.5),
        'moe_w3': nrm(ks[21], (N_MOE, N_EXPERTS, D, EXPERT_FF), D ** -0.5),
        'moe_w2': nrm(ks[22], (N_MOE, N_EXPERTS, EXPERT_FF, D), EXPERT_FF ** -0.5),
        'final_g': 1.0 + nrm(ks[23], (D,), 0.02),
    }


def reference(x, c, ctx, c_ctx, w_ada, b_ada, norm1_g, norm2_g, w_in, w_gate, b_gate, gla_norm_g,
              w_pool, pool_scale, w_out, ffn_w1, ffn_w3, ffn_w2, moe_w_router, moe_b_router,
              moe_w1, moe_w3, moe_w2, final_g):
    bsz, n_lat, _ = x.shape
    rows = n_lat // GRID_W
    zero_state = jnp.zeros((bsz, GLA_HEADS, GLA_DK, GLA_DV), jnp.float32)

    def ffn(h, i):
        if i % 2 == 0:
            j = i // 2
            return swiglu(h, ffn_w1[j], ffn_w3[j], ffn_w2[j])
        j = i // 2
        return moe_swiglu(h, moe_w_router[j], moe_b_router[j], moe_w1[j], moe_w3[j], moe_w2[j])

    x_lat = x
    x_ctx = ctx
    for i in range(DEPTH):
        last = i == DEPTH - 1
        mod = (jax.nn.silu(c) @ w_ada[i] + b_ada[i])[:, None, :]
        mod_c = (jax.nn.silu(c_ctx) @ w_ada[i] + b_ada[i])[None, None, :]
        sh1, sc1, g1, sh2, sc2, g2 = jnp.split(mod, 6, axis=-1)
        csh1, csc1, cg1, csh2, csc2, cg2 = jnp.split(mod_c, 6, axis=-1)

        h = rmsnorm(x_lat, norm1_g[i]) * (1.0 + sc1) + sh1
        hc = rmsnorm(x_ctx, norm1_g[i]) * (1.0 + csc1) + csh1
        q, k, v, g, p, lgf, lgb = project(h, w_in[i], w_gate[i], b_gate[i])
        qc, kc, vc, gc, pc, lgfc, lgbc = project(hc, w_in[i], w_gate[i], b_gate[i])
        o_c, s_f, s_b = gla_bidir(qc, kc, vc, lgfc, lgbc, zero_state, zero_state)
        o_l, _, _ = gla_bidir(q, k, v, lgf, lgb, s_f, s_b)
        y = jnp.concatenate([gla_readout(o_l, g, gla_norm_g[i]),
                             pool_mix(p, w_pool[i], pool_scale[i], rows)], axis=-1) @ w_out[i]
        x_lat = x_lat + g1 * y

        h2 = rmsnorm(x_lat, norm2_g[i]) * (1.0 + sc2) + sh2
        x_lat = x_lat + g2 * ffn(h2, i)

        if not last:
            yc = jnp.concatenate([gla_readout(o_c, gc, gla_norm_g[i]),
                                  pool_mix(pc, w_pool[i], pool_scale[i], 1)], axis=-1) @ w_out[i]
            x_ctx = x_ctx + cg1 * yc
            h2c = rmsnorm(x_ctx, norm2_g[i]) * (1.0 + csc2) + csh2
            x_ctx = x_ctx + cg2 * ffn(h2c, i)

    return rmsnorm(x_lat, final_g)
```

```python
import functools

import numpy as np
import jax
import jax.numpy as jnp
from jax import lax
from jax.experimental import pallas as pl
from jax.experimental.pallas import tpu as pltpu

F32 = jnp.float32
BF16 = jnp.bfloat16

D = 2048
B = 2
SEQ = 4096
CTX = 256
DEPTH = 2
GRID_W = 64
EPS = 1e-6
GLA_W = 1024
POOL_W = 1024
H = 4
DV = 256
DK = 128
KEYW = H * DK
RANK = 16
TAU = 16.0
WINDOWS = (2, 4, 8, 16)
PG = 4
PGW = 256
D_FF = 5632
NE = 8

LT = CTX + SEQ
NT = B * LT
TM = 256
TPB = LT // TM
NTILES = NT // TM
LANES = 128

CH = 64
SB = 16
NCH_CTX = CTX // CH
NCH = LT // CH

TF = 256
NJ = D_FF // TF
TM_DENSE = NT // 8
SUB_DENSE = TM_DENSE // 4
TM_MOE = 1024
SUB_MOE = 256
N_ASSIGN = B * SEQ * 2
NT_MOE = N_ASSIGN // TM_MOE + NE
R_MOE = NT_MOE * TM_MOE

VMEM_LIMIT = 56 * 1024 * 1024


def _cparams(sem, vmem=VMEM_LIMIT):
    return pltpu.CompilerParams(dimension_semantics=sem, vmem_limit_bytes=vmem)


def _split2(a):
    hi = a.astype(BF16)
    lo = (a - hi.astype(F32)).astype(BF16)
    return hi, lo


def _dot(a, b):
    return jnp.dot(a, b, preferred_element_type=F32)


def _dot3(a, b):
    ah, al = _split2(a)
    bh, bl = _split2(b)
    return _dot(ah, bh) + (_dot(al, bh) + _dot(ah, bl))


def _silu(a):
    return a / (1.0 + jnp.exp(-a))


def _rms(x):
    return x * lax.rsqrt(jnp.mean(x * x, axis=-1, keepdims=True) + EPS)


def _mod_row(i):
    return jnp.where(i % TPB == 0, 2, i // TPB)


ADA_TN = 1024


def _ada_kernel(c_ref, w_ref, b_ref, o_ref):
    s = _silu(c_ref[...])
    o_ref[0] = _dot3(s, w_ref[0]) + b_ref[0]


def _ada(cvec, w_ada, b_ada):
    n = 6 * D
    return pl.pallas_call(
        _ada_kernel,
        grid=(DEPTH, n // ADA_TN),
        in_specs=[
            pl.BlockSpec((8, D), lambda l, j: (0, 0)),
            pl.BlockSpec((1, D, ADA_TN), lambda l, j: (l, 0, j)),
            pl.BlockSpec((1, 1, ADA_TN), lambda l, j: (l, 0, j)),
        ],
        out_specs=pl.BlockSpec((1, 8, ADA_TN), lambda l, j: (l, 0, j)),
        out_shape=jax.ShapeDtypeStruct((DEPTH, 8, n), F32),
        compiler_params=_cparams(("parallel", "parallel")),
        name="ada",
    )(cvec, w_ada, b_ada.reshape(DEPTH, 1, n))


IN_MAIN = 2 * KEYW + 2 * GLA_W + POOL_W
IN_PAD = IN_MAIN + LANES


def _inproj_kernel(x_ref, mod_ref, g_ref, w_ref, qk_ref, vgp_ref, r_ref):
    mod = mod_ref[0]
    h = _rms(x_ref[...]) * g_ref[...] * (1.0 + mod[:, D:2 * D]) + mod[:, 0:D]
    u = _dot(h.astype(BF16), w_ref[...])
    qk_ref[...] = u[:, :2 * KEYW]
    vgp_ref[...] = u[:, 2 * KEYW:IN_MAIN].astype(BF16)
    r_ref[...] = u[:, IN_MAIN:]


def _inproj(x, mod3, g, w):
    return pl.pallas_call(
        _inproj_kernel,
        grid=(NTILES,),
        in_specs=[
            pl.BlockSpec((TM, D), lambda i: (i, 0)),
            pl.BlockSpec((1, 1, 6 * D), lambda i: (_mod_row(i), 0, 0)),
            pl.BlockSpec((1, D), lambda i: (0, 0)),
            pl.BlockSpec((D, IN_PAD), lambda i: (0, 0)),
        ],
        out_specs=[
            pl.BlockSpec((TM, 2 * KEYW), lambda i: (i, 0)),
            pl.BlockSpec((TM, IN_MAIN - 2 * KEYW), lambda i: (i, 0)),
            pl.BlockSpec((TM, LANES), lambda i: (i, 0)),
        ],
        out_shape=[
            jax.ShapeDtypeStruct((NT, 2 * KEYW), F32),
            jax.ShapeDtypeStruct((NT, IN_MAIN - 2 * KEYW), BF16),
            jax.ShapeDtypeStruct((NT, LANES), F32),
        ],
        compiler_params=_cparams(("parallel",)),
        name="inproj",
    )(x, mod3, g, w)


_NT_DIMS = (((1,), (1,)), ((), ()))
_TN_DIMS = (((0,), (0,)), ((), ()))


def _gla_dir(qk_ref, v_ref, r_ref, wg_ref, bg_ref, o_ref, st_ref, rev):
    d = 1 if rev else 0
    pre = _dot(r_ref[...].astype(BF16), wg_ref[d]) + bg_ref[d]
    lg = (jnp.minimum(pre, 0.0) - jnp.log(1.0 + jnp.exp(-jnp.abs(pre)))) * (1.0 / TAU)
    row = lax.broadcasted_iota(jnp.int32, (CH, CH), 0)
    col = lax.broadcasted_iota(jnp.int32, (CH, CH), 1)
    inside = (col >= row) if rev else (col <= row)
    tri = jnp.where(inside, 1.0, 0.0).astype(BF16)
    lg_hi, lg_lo = _split2(lg)
    cum = _dot(tri, lg_hi) + _dot(tri, lg_lo)
    cex = cum - lg
    rows_k = lax.broadcasted_iota(jnp.int32, (CH, DK), 0)
    t_idx = lax.broadcasted_iota(jnp.int32, (SB, CH), 0)
    s_idx = lax.broadcasted_iota(jnp.int32, (SB, CH), 1)
    for h in range(H):
        q = qk_ref[:, h * DK:(h + 1) * DK] * (DK ** -0.5)
        k = qk_ref[:, KEYW + h * DK:KEYW + (h + 1) * DK]
        v = v_ref[:, h * DV:(h + 1) * DV]
        c = cum[:, h * DK:(h + 1) * DK]
        ce = cex[:, h * DK:(h + 1) * DK]
        st = st_ref[h]
        o = lax.dot_general((q * jnp.exp(c)).astype(BF16), st.astype(BF16), _NT_DIMS,
                            preferred_element_type=F32)
        blocks = []
        for blk in range(CH // SB):
            r0 = blk * SB
            if rev:
                a = ce[r0 + SB - 1:r0 + SB, :]
                live = rows_k >= r0
                keep = (s_idx >= t_idx + r0)
            else:
                a = ce[r0:r0 + 1, :]
                live = rows_k < r0 + SB
                keep = (s_idx <= t_idx + r0)
            qb = (q[r0:r0 + SB] * jnp.exp(c[r0:r0 + SB] - a)).astype(BF16)
            kb = (k * jnp.exp(jnp.where(live, a - c, 0.0))).astype(BF16)
            s = lax.dot_general(qb, kb, _NT_DIMS, preferred_element_type=F32)
            blocks.append(jnp.where(keep, s, 0.0))
        scores = jnp.concatenate(blocks, axis=0).astype(BF16)
        o_ref[:, h * DV:(h + 1) * DV] = o + _dot(scores, v)
        tot = c[0:1, :] if rev else c[CH - 1:CH, :]
        kd = (k * jnp.exp(tot - c)).astype(BF16)
        st_ref[h] = st * jnp.exp(tot) + lax.dot_general(v, kd, _TN_DIMS,
                                                        preferred_element_type=F32)


def _gla_kernel(qkf_ref, qkb_ref, vf_ref, vb_ref, rf_ref, rb_ref, wg_ref, bg_ref,
                of_ref, ob_ref, stf_ref, stb_ref):
    @pl.when(pl.program_id(1) == 0)
    def _():
        stf_ref[...] = jnp.zeros_like(stf_ref)
        stb_ref[...] = jnp.zeros_like(stb_ref)

    _gla_dir(qkf_ref, vf_ref, rf_ref, wg_ref, bg_ref, of_ref, stf_ref, rev=False)
    _gla_dir(qkb_ref, vb_ref, rb_ref, wg_ref, bg_ref, ob_ref, stb_ref, rev=True)


def _bwd_chunk(j):
    return jnp.where(j < NCH_CTX, NCH_CTX - 1 - j, NCH + NCH_CTX - 1 - j)


def _gla(qk, vgp, r, wg, bg):
    qk3 = qk.reshape(B, LT, 2 * KEYW)
    vgp3 = vgp.reshape(B, LT, IN_MAIN - 2 * KEYW)
    r3 = r.reshape(B, LT, LANES)
    fwd = lambda b, j: (b, j, 0)
    bwd = lambda b, j: (b, _bwd_chunk(j), 0)
    o_f, o_b = pl.pallas_call(
        _gla_kernel,
        grid=(B, NCH),
        in_specs=[
            pl.BlockSpec((None, CH, 2 * KEYW), fwd),
            pl.BlockSpec((None, CH, 2 * KEYW), bwd),
            pl.BlockSpec((None, CH, GLA_W), fwd),
            pl.BlockSpec((None, CH, GLA_W), bwd),
            pl.BlockSpec((None, CH, LANES), fwd),
            pl.BlockSpec((None, CH, LANES), bwd),
            pl.BlockSpec((2, LANES, KEYW), lambda b, j: (0, 0, 0)),
            pl.BlockSpec((2, 1, KEYW), lambda b, j: (0, 0, 0)),
        ],
        out_specs=[
            pl.BlockSpec((None, CH, GLA_W), fwd),
            pl.BlockSpec((None, CH, GLA_W), bwd),
        ],
        out_shape=[jax.ShapeDtypeStruct((B, LT, GLA_W), F32)] * 2,
        scratch_shapes=[pltpu.VMEM((H, DV, DK), F32), pltpu.VMEM((H, DV, DK), F32)],
        compiler_params=_cparams(("parallel", "arbitrary")),
        name="gla",
    )(qk3, qk3, vgp3, vgp3, r3, r3, wg, bg)
    return o_f.reshape(NT, GLA_W), o_b.reshape(NT, GLA_W)


def _pool_tables():
    band = np.zeros((2, PG, TM, TM), np.float32)
    inv = np.zeros((2, PG, TM, PGW), np.float32)
    for kind, row_len in enumerate((GRID_W, CTX)):
        for gi, w in enumerate(WINDOWS):
            for n in range(TM):
                base, j = (n // row_len) * row_len, n % row_len
                lo = min(max(j - w // 2, 0), row_len)
                hi = min(max(j - w // 2 + w, 0), row_len)
                band[kind, gi, n, base + lo:base + hi] = 1.0
                inv[kind, gi, n, :] = 1.0 / (hi - lo)
    return band, inv


def _mix_kernel(with_router, x_ref, of_ref, ob_ref, g_ref, p_ref, mod_ref, n2_ref, gain_ref,
                band_ref, inv_ref, wp_ref, ps_ref, wo_ref, *rest):
    if with_router:
        wr_ref, xo_ref, h2_ref, lg_ref = rest
    else:
        xo_ref, h2_ref = rest
    mod = mod_ref[0]
    o = of_ref[...] + ob_ref[...]
    on = jnp.concatenate([_rms(o[:, h * DV:(h + 1) * DV]) for h in range(H)], axis=1)
    read = (on * gain_ref[...] * _silu(g_ref[...].astype(F32))).astype(BF16)
    y = _dot(read, wo_ref[0:GLA_W, :])
    pouts = []
    for gi in range(PG):
        pg = p_ref[:, gi * PGW:(gi + 1) * PGW]
        mixed = _dot(band_ref[0, gi], pg) * inv_ref[0, gi] - pg.astype(F32)
        pouts.append(_dot(mixed.astype(BF16), wp_ref[gi]))
    pool = (jnp.concatenate(pouts, axis=1) * ps_ref[...]).astype(BF16)
    y = y + _dot(pool, wo_ref[GLA_W:, :])
    xn = x_ref[...] + mod[:, 2 * D:3 * D] * y
    xo_ref[...] = xn
    h2 = _rms(xn) * n2_ref[...] * (1.0 + mod[:, 4 * D:5 * D]) + mod[:, 3 * D:4 * D]
    h2_ref[...] = h2.astype(h2_ref.dtype)
    if with_router:
        lg_ref[...] = _dot3(h2, wr_ref[...])


def _mix(x, o_f, o_b, vgp, mod3, n2, gain, band, inv, wp, ps, wo, wr=None):
    with_router = wr is not None
    kind = lambda i: (jnp.where(i % TPB == 0, 1, 0), 0, 0, 0)
    const2 = lambda i: (0, 0)
    in_specs = [
        pl.BlockSpec((TM, D), lambda i: (i, 0)),
        pl.BlockSpec((TM, GLA_W), lambda i: (i, 0)),
        pl.BlockSpec((TM, GLA_W), lambda i: (i, 0)),
        pl.BlockSpec((TM, GLA_W), lambda i: (i, 1)),
        pl.BlockSpec((TM, POOL_W), lambda i: (i, 2)),
        pl.BlockSpec((1, 1, 6 * D), lambda i: (_mod_row(i), 0, 0)),
        pl.BlockSpec((1, D), const2),
        pl.BlockSpec((1, GLA_W), const2),
        pl.BlockSpec((1, PG, TM, TM), kind),
        pl.BlockSpec((1, PG, TM, PGW), kind),
        pl.BlockSpec((PG, PGW, PGW), lambda i: (0, 0, 0)),
        pl.BlockSpec((1, POOL_W), const2),
        pl.BlockSpec((D, D), const2),
    ]
    args = [x, o_f, o_b, vgp, vgp, mod3, n2, gain, band, inv, wp, ps, wo]
    out_specs = [pl.BlockSpec((TM, D), lambda i: (i, 0)), pl.BlockSpec((TM, D), lambda i: (i, 0))]
    out_shape = [jax.ShapeDtypeStruct((NT, D), F32),
                 jax.ShapeDtypeStruct((NT, D), F32 if with_router else BF16)]
    if with_router:
        in_specs.append(pl.BlockSpec((D, LANES), const2))
        args.append(wr)
        out_specs.append(pl.BlockSpec((TM, LANES), lambda i: (i, 0)))
        out_shape.append(jax.ShapeDtypeStruct((NT, LANES), F32))
    return pl.pallas_call(
        functools.partial(_mix_kernel, with_router),
        grid=(NTILES,),
        in_specs=in_specs,
        out_specs=out_specs,
        out_shape=out_shape,
        compiler_params=_cparams(("parallel",)),
        name="mix_router" if with_router else "mix",
    )(*args)


def _ffn_kernel(sub, te_ref, ns_ref, blk_ref, xs_ref, w1_ref, w3_ref, w2_ref, o_ref):
    i = pl.program_id(0)
    j = pl.program_id(1)
    nsub = ns_ref[i]

    @pl.when(j == 0)
    def _():
        o_ref[...] = jnp.zeros_like(o_ref)

    @pl.when(nsub > 0)
    def _():
        w1 = w1_ref[0, 0].astype(BF16)
        w3 = w3_ref[0, 0].astype(BF16)
        w2 = w2_ref[0, 0].astype(BF16)

        def body(s, carry):
            rows = pl.ds(pl.multiple_of(s * sub, sub), sub)
            xt = xs_ref[rows, :]
            act = (_silu(_dot(xt, w1)) * _dot(xt, w3)).astype(BF16)
            o_ref[rows, :] += _dot(act, w2)
            return carry

        lax.fori_loop(0, nsub, body, 0)


def _ffn(xs, te, ns, blk, w1, w3, w2, wl, tm, sub):
    n_tiles = xs.shape[0] // tm
    live_j = lambda i, j, te, ns, blk: jnp.where(ns[i] > 0, j, NJ - 1)
    grid_spec = pltpu.PrefetchScalarGridSpec(
        num_scalar_prefetch=3,
        grid=(n_tiles, NJ),
        in_specs=[
            pl.BlockSpec((tm, D), lambda i, j, te, ns, blk: (blk[i], 0)),
            pl.BlockSpec((1, 1, D, TF), lambda i, j, te, ns, blk: (wl, te[i], 0, live_j(i, j, te, ns, blk))),
            pl.BlockSpec((1, 1, D, TF), lambda i, j, te, ns, blk: (wl, te[i], 0, live_j(i, j, te, ns, blk))),
            pl.BlockSpec((1, 1, TF, D), lambda i, j, te, ns, blk: (wl, te[i], live_j(i, j, te, ns, blk), 0)),
        ],
        out_specs=pl.BlockSpec((tm, D), lambda i, j, te, ns, blk: (i, 0)),
    )
    return pl.pallas_call(
        functools.partial(_ffn_kernel, sub),
        grid_spec=grid_spec,
        out_shape=jax.ShapeDtypeStruct((xs.shape[0], D), F32),
        compiler_params=_cparams(("arbitrary", "arbitrary")),
        name="ffn_%d" % tm,
    )(te, ns, blk, xs, w1, w3, w2)


def _resid_kernel(x_ref, y_ref, mod_ref, o_ref):
    o_ref[...] = x_ref[...] + mod_ref[0][:, 5 * D:6 * D] * y_ref[...]


def _resid(x, y, mod3):
    return pl.pallas_call(
        _resid_kernel,
        grid=(NTILES,),
        in_specs=[
            pl.BlockSpec((TM, D), lambda i: (i, 0)),
            pl.BlockSpec((TM, D), lambda i: (i, 0)),
            pl.BlockSpec((1, 1, 6 * D), lambda i: (_mod_row(i), 0, 0)),
        ],
        out_specs=pl.BlockSpec((TM, D), lambda i: (i, 0)),
        out_shape=jax.ShapeDtypeStruct((NT, D), F32),
        compiler_params=_cparams(("parallel",)),
        name="resid",
    )(x, y, mod3)


def _lat_tile(t):
    return t + t // (TPB - 1) + 1


def _router_kernel(lg_ref, br_ref, route_ref, cnt_ref, carry_ref):
    @pl.when(pl.program_id(0) == 0)
    def _():
        carry_ref[...] = jnp.zeros_like(carry_ref)

    lane = lax.broadcasted_iota(jnp.int32, (TM, LANES), 1)
    z = jnp.where(lane < NE, lg_ref[...] + br_ref[...], -jnp.inf)
    m0 = jnp.max(z, axis=-1, keepdims=True)
    lane_f = lane.astype(F32)
    e0 = jnp.min(jnp.where(z == m0, lane_f, float(LANES)), axis=-1, keepdims=True)
    z1 = jnp.where(lane_f == e0, -jnp.inf, z)
    m1 = jnp.max(z1, axis=-1, keepdims=True)
    e1 = jnp.min(jnp.where(z1 == m1, lane_f, float(LANES)), axis=-1, keepdims=True)
    t = jnp.exp(m1 - m0)
    p0 = 1.0 / (1.0 + t)
    p1 = t / (1.0 + t)
    oh0 = lane_f == e0
    oh1 = lane_f == e1
    oh = jnp.where(jnp.logical_or(oh0, oh1), 1.0, 0.0)
    row = lax.broadcasted_iota(jnp.int32, (TM, TM), 0)
    col = lax.broadcasted_iota(jnp.int32, (TM, TM), 1)
    before = jnp.where(col < row, 1.0, 0.0).astype(BF16)
    excl = _dot(before, oh.astype(BF16)) + carry_ref[0:1, :]
    rank0 = jnp.sum(jnp.where(oh0, excl, 0.0), axis=-1, keepdims=True)
    rank1 = jnp.sum(jnp.where(oh1, excl, 0.0), axis=-1, keepdims=True)
    total = carry_ref[0:1, :] + jnp.sum(oh, axis=0, keepdims=True)
    carry_ref[...] = jnp.broadcast_to(total, carry_ref.shape)
    cnt_ref[...] = jnp.broadcast_to(total, cnt_ref.shape)
    route = jnp.where(lane == 0, e0, 0.0)
    route = jnp.where(lane == 1, e1, route)
    route = jnp.where(lane == 2, rank0, route)
    route = jnp.where(lane == 3, rank1, route)
    route = jnp.where(lane == 4, p0, route)
    route = jnp.where(lane == 5, p1, route)
    route_ref[...] = route


def _router(logits, b_r):
    n_lat_tiles = B * SEQ // TM
    return pl.pallas_call(
        _router_kernel,
        grid=(n_lat_tiles,),
        in_specs=[
            pl.BlockSpec((TM, LANES), lambda t: (_lat_tile(t), 0)),
            pl.BlockSpec((1, LANES), lambda t: (0, 0)),
        ],
        out_specs=[
            pl.BlockSpec((TM, LANES), lambda t: (t, 0)),
            pl.BlockSpec((8, LANES), lambda t: (0, 0)),
        ],
        out_shape=[jax.ShapeDtypeStruct((B * SEQ, LANES), F32),
                   jax.ShapeDtypeStruct((8, LANES), F32)],
        scratch_shapes=[pltpu.VMEM((8, LANES), F32)],
        compiler_params=_cparams(("arbitrary",)),
        name="router",
    )(logits, b_r)


def _row_copy(src_hbm, row, dst, r, sem):
    return pltpu.make_async_copy(src_hbm.at[pl.ds(row, 1), :], dst.at[pl.ds(r, 1), :], sem)


def _gather_kernel(idx_ref, h_hbm, o_ref, buf_ref, sem):
    def issue(r, carry):
        _row_copy(h_hbm, idx_ref[0, 0, r], buf_ref, r, sem).start()
        return carry

    lax.fori_loop(0, TM, issue, 0)

    def drain(r, carry):
        _row_copy(h_hbm, 0, buf_ref, r, sem).wait()
        return carry

    lax.fori_loop(0, TM, drain, 0)
    o_ref[...] = buf_ref[...].astype(BF16)


def _gather(h2, src):
    n = src.shape[0] // TM
    return pl.pallas_call(
        _gather_kernel,
        grid=(n,),
        in_specs=[
            pl.BlockSpec((1, 1, TM), lambda i: (i, 0, 0), memory_space=pltpu.SMEM),
            pl.BlockSpec(memory_space=pl.ANY),
        ],
        out_specs=pl.BlockSpec((TM, D), lambda i: (i, 0)),
        out_shape=jax.ShapeDtypeStruct((src.shape[0], D), BF16),
        scratch_shapes=[pltpu.VMEM((TM, D), F32), pltpu.SemaphoreType.DMA(())],
        compiler_params=_cparams(("arbitrary",)),
        name="dispatch",
    )(src.reshape(n, 1, TM), h2)


def _combine_kernel(p0_ref, p1_ref, x_ref, route_ref, mod_ref, fg_ref, y_hbm, o_ref,
                    b0_ref, b1_ref, sem0, sem1):
    def issue(r, carry):
        _row_copy(y_hbm, p0_ref[0, 0, r], b0_ref, r, sem0).start()
        _row_copy(y_hbm, p1_ref[0, 0, r], b1_ref, r, sem1).start()
        return carry

    lax.fori_loop(0, TM, issue, 0)

    def drain(r, carry):
        _row_copy(y_hbm, 0, b0_ref, r, sem0).wait()
        _row_copy(y_hbm, 0, b1_ref, r, sem1).wait()
        return carry

    lax.fori_loop(0, TM, drain, 0)
    route = route_ref[...]
    y = route[:, 4:5] * b0_ref[...] + route[:, 5:6] * b1_ref[...]
    xn = x_ref[...] + mod_ref[0][:, 5 * D:6 * D] * y
    o_ref[...] = _rms(xn) * fg_ref[...]


def _combine(x, ys, pos0, pos1, route, mod3, fg):
    n = B * SEQ // TM
    smem = lambda: pl.BlockSpec((1, 1, TM), lambda t: (t, 0, 0), memory_space=pltpu.SMEM)
    return pl.pallas_call(
        _combine_kernel,
        grid=(n,),
        in_specs=[
            smem(), smem(),
            pl.BlockSpec((TM, D), lambda t: (_lat_tile(t), 0)),
            pl.BlockSpec((TM, LANES), lambda t: (t, 0)),
            pl.BlockSpec((1, 1, 6 * D), lambda t: (t // (TPB - 1), 0, 0)),
            pl.BlockSpec((1, D), lambda t: (0, 0)),
            pl.BlockSpec(memory_space=pl.ANY),
        ],
        out_specs=pl.BlockSpec((TM, D), lambda t: (t, 0)),
        out_shape=jax.ShapeDtypeStruct((B * SEQ, D), F32),
        scratch_shapes=[pltpu.VMEM((TM, D), F32), pltpu.VMEM((TM, D), F32),
                        pltpu.SemaphoreType.DMA(()), pltpu.SemaphoreType.DMA(())],
        compiler_params=_cparams(("arbitrary",)),
        name="combine",
    )(pos0.reshape(n, 1, TM), pos1.reshape(n, 1, TM), x, route, mod3, fg, ys)


def _moe_plan(route, counts):
    e = route[:, 0:2].astype(jnp.int32)
    rank = route[:, 2:4].astype(jnp.int32)
    cnt = counts[0, :NE].astype(jnp.int32)
    ntile = (cnt + TM_MOE - 1) // TM_MOE
    tile_end = jnp.cumsum(ntile)
    tile_start = tile_end - ntile
    pos = tile_start[e] * TM_MOE + rank
    n_used = tile_end[NE - 1]
    tiles = jnp.arange(NT_MOE, dtype=jnp.int32)
    blk = jnp.minimum(tiles, n_used - 1)
    te = jnp.sum((blk[:, None] >= tile_end[None, :NE - 1]).astype(jnp.int32), axis=1)
    left = cnt[te] - (blk - tile_start[te]) * TM_MOE
    ns = jnp.clip((left + SUB_MOE - 1) // SUB_MOE, 0, TM_MOE // SUB_MOE)
    ns = jnp.where(tiles < n_used, ns, 0).astype(jnp.int32)
    tok = jnp.arange(B * SEQ, dtype=jnp.int32)
    tok_row = (tok // SEQ) * LT + CTX + tok % SEQ
    src = jnp.full((R_MOE,), CTX, jnp.int32).at[pos.reshape(-1)].set(jnp.repeat(tok_row, 2))
    return pos[:, 0], pos[:, 1], te, ns, blk, src


def kernel(x, c, ctx, c_ctx, w_ada, b_ada, norm1_g, norm2_g, w_in, w_gate, b_gate, gla_norm_g,
           w_pool, pool_scale, w_out, ffn_w1, ffn_w3, ffn_w2, moe_w_router, moe_b_router,
           moe_w1, moe_w3, moe_w2, final_g):
    assert x.shape == (B, SEQ, D) and ctx.shape == (B, CTX, D) and DEPTH == 2

    cvec = jnp.concatenate([c, c_ctx[None, :], jnp.zeros((8 - B - 1, D), F32)], axis=0)
    mods = _ada(cvec, w_ada, b_ada)
    xs = jnp.concatenate([ctx, x], axis=1).reshape(NT, D)

    band_np, inv_np = _pool_tables()
    band = jnp.asarray(band_np, BF16)
    inv = jnp.asarray(inv_np, F32)

    dense_te = jnp.zeros((NT // TM_DENSE,), jnp.int32)
    dense_ns = jnp.full((NT // TM_DENSE,), TM_DENSE // SUB_DENSE, jnp.int32)
    dense_blk = jnp.arange(NT // TM_DENSE, dtype=jnp.int32)

    out = None
    for l in range(DEPTH):
        mod3 = mods[l, :3].reshape(3, 1, 6 * D)
        w_in_l = jnp.pad(w_in[l], ((0, 0), (0, IN_PAD - w_in.shape[2]))).astype(BF16)
        wg = jnp.zeros((2, LANES, KEYW), F32)
        wg = wg.at[0, 0:RANK].set(w_gate[l, 0]).at[1, RANK:2 * RANK].set(w_gate[l, 1]).astype(BF16)
        bg = b_gate[l].reshape(2, 1, KEYW)

        qk, vgp, r = _inproj(xs, mod3, norm1_g[l].reshape(1, D), w_in_l)
        o_f, o_b = _gla(qk, vgp, r, wg, bg)
        mix_args = (xs, o_f, o_b, vgp, mod3, norm2_g[l].reshape(1, D), gla_norm_g[l].reshape(1, GLA_W),
                    band, inv, w_pool[l].astype(BF16), pool_scale[l].reshape(1, POOL_W),
                    w_out[l].astype(BF16))
        if l % 2 == 0:
            x1, h2 = _mix(*mix_args)
            jl = l // 2
            ys = _ffn(h2, dense_te, dense_ns, dense_blk, ffn_w1[:, None], ffn_w3[:, None],
                      ffn_w2[:, None], jl, TM_DENSE, SUB_DENSE)
            xs = _resid(x1, ys, mod3)
        else:
            jl = l // 2
            wr = jnp.pad(moe_w_router[jl], ((0, 0), (0, LANES - NE)))
            br = jnp.pad(moe_b_router[jl], (0, LANES - NE)).reshape(1, LANES)
            x1, h2, logits = _mix(*mix_args, wr=wr)
            route, counts = _router(logits, br)
            pos0, pos1, te, ns, blk, src = _moe_plan(route, counts)
            xg = _gather(h2, src)
            ys = _ffn(xg, te, ns, blk, moe_w1, moe_w3, moe_w2, jl, TM_MOE, SUB_MOE)
            out = _combine(x1, ys, pos0, pos1, route, mod3, final_g.reshape(1, D))
    return out.reshape(B, SEQ, D)
```

```python
import functools

import numpy as np
import jax
import jax.numpy as jnp
from jax import lax
from jax.experimental import pallas as pl
from jax.experimental.pallas import tpu as pltpu

F32 = jnp.float32
BF16 = jnp.bfloat16

D = 2048
B = 2
SEQ = 4096
CTX = 256
DEPTH = 2
GRID_W = 64
EPS = 1e-6
GLA_W = 1024
POOL_W = 1024
H = 4
DV = 256
DK = 128
KEYW = H * DK
RANK = 16
TAU = 16.0
WINDOWS = (2, 4, 8, 16)
PG = 4
PGW = 256
D_FF = 5632
NE = 8

LT = CTX + SEQ
NT = B * LT
TM = 256
TPB = LT // TM
NTILES = NT // TM
LANES = 128
SLAB = D // LANES
SLAB_PITCH = SLAB + 4

CH = 64
SB = 16
NCH_CTX = CTX // CH
NCH = LT // CH

TF = 256
NJ = D_FF // TF
TM_DENSE = NT // 8
SUB_DENSE = TM_DENSE // 4
TM_MOE = 1024
SUB_MOE = 256
N_ASSIGN = B * SEQ * 2
NT_MOE = N_ASSIGN // TM_MOE + NE
R_MOE = NT_MOE * TM_MOE
assert SUB_MOE == TM

VMEM_LIMIT = 56 * 1024 * 1024


def _cparams(sem, vmem=VMEM_LIMIT):
    return pltpu.CompilerParams(dimension_semantics=sem, vmem_limit_bytes=vmem)


def _split2(a):
    hi = a.astype(BF16)
    lo = (a - hi.astype(F32)).astype(BF16)
    return hi, lo


def _dot(a, b):
    return jnp.dot(a, b, preferred_element_type=F32)


def _dot3(a, b):
    ah, al = _split2(a)
    bh, bl = _split2(b)
    return _dot(ah, bh) + (_dot(al, bh) + _dot(ah, bl))


def _silu(a):
    return a / (1.0 + jnp.exp(-a))


def _rms(x):
    return x * lax.rsqrt(jnp.mean(x * x, axis=-1, keepdims=True) + EPS)


def _mod_row(i):
    return jnp.where(i % TPB == 0, 2, i // TPB)


ADA_TN = 1024


def _ada_kernel(c_ref, w_ref, b_ref, o_ref):
    s = _silu(c_ref[...])
    o_ref[0] = _dot3(s, w_ref[0]) + b_ref[0]


def _ada(cvec, w_ada, b_ada):
    n = 6 * D
    return pl.pallas_call(
        _ada_kernel,
        grid=(DEPTH, n // ADA_TN),
        in_specs=[
            pl.BlockSpec((8, D), lambda l, j: (0, 0)),
            pl.BlockSpec((1, D, ADA_TN), lambda l, j: (l, 0, j)),
            pl.BlockSpec((1, 1, ADA_TN), lambda l, j: (l, 0, j)),
        ],
        out_specs=pl.BlockSpec((1, 8, ADA_TN), lambda l, j: (l, 0, j)),
        out_shape=jax.ShapeDtypeStruct((DEPTH, 8, n), F32),
        compiler_params=_cparams(("parallel", "parallel")),
        name="ada",
    )(cvec, w_ada, b_ada.reshape(DEPTH, 1, n))


IN_MAIN = 2 * KEYW + 2 * GLA_W + POOL_W
IN_PAD = IN_MAIN + LANES


def _inproj_kernel(x_ref, mod_ref, g_ref, w_ref, qk_ref, vgp_ref, r_ref):
    mod = mod_ref[0]
    h = _rms(x_ref[...]) * g_ref[...] * (1.0 + mod[:, D:2 * D]) + mod[:, 0:D]
    u = _dot(h.astype(BF16), w_ref[...])
    qk_ref[...] = u[:, :2 * KEYW]
    vgp_ref[...] = u[:, 2 * KEYW:IN_MAIN].astype(BF16)
    r_ref[...] = u[:, IN_MAIN:]


def _inproj(x, mod3, g, w):
    return pl.pallas_call(
        _inproj_kernel,
        grid=(NTILES,),
        in_specs=[
            pl.BlockSpec((TM, D), lambda i: (i, 0)),
            pl.BlockSpec((1, 1, 6 * D), lambda i: (_mod_row(i), 0, 0)),
            pl.BlockSpec((1, D), lambda i: (0, 0)),
            pl.BlockSpec((D, IN_PAD), lambda i: (0, 0)),
        ],
        out_specs=[
            pl.BlockSpec((TM, 2 * KEYW), lambda i: (i, 0)),
            pl.BlockSpec((TM, IN_MAIN - 2 * KEYW), lambda i: (i, 0)),
            pl.BlockSpec((TM, LANES), lambda i: (i, 0)),
        ],
        out_shape=[
            jax.ShapeDtypeStruct((NT, 2 * KEYW), F32),
            jax.ShapeDtypeStruct((NT, IN_MAIN - 2 * KEYW), BF16),
            jax.ShapeDtypeStruct((NT, LANES), F32),
        ],
        compiler_params=_cparams(("parallel",)),
        name="inproj",
    )(x, mod3, g, w)


_NT_DIMS = (((1,), (1,)), ((), ()))
_TN_DIMS = (((0,), (0,)), ((), ()))


def _gla_dir(qk_ref, v_ref, r_ref, wg_ref, bg_ref, o_ref, st_ref, rev):
    d = 1 if rev else 0
    pre = _dot(r_ref[...].astype(BF16), wg_ref[d]) + bg_ref[d]
    lg = (jnp.minimum(pre, 0.0) - jnp.log(1.0 + jnp.exp(-jnp.abs(pre)))) * (1.0 / TAU)
    row = lax.broadcasted_iota(jnp.int32, (CH, CH), 0)
    col = lax.broadcasted_iota(jnp.int32, (CH, CH), 1)
    inside = (col >= row) if rev else (col <= row)
    tri = jnp.where(inside, 1.0, 0.0).astype(BF16)
    lg_hi, lg_lo = _split2(lg)
    cum = _dot(tri, lg_hi) + _dot(tri, lg_lo)
    cex = cum - lg
    rows_k = lax.broadcasted_iota(jnp.int32, (CH, DK), 0)
    t_idx = lax.broadcasted_iota(jnp.int32, (SB, CH), 0)
    s_idx = lax.broadcasted_iota(jnp.int32, (SB, CH), 1)
    for h in range(H):
        q = qk_ref[:, h * DK:(h + 1) * DK] * (DK ** -0.5)
        k = qk_ref[:, KEYW + h * DK:KEYW + (h + 1) * DK]
        v = v_ref[:, h * DV:(h + 1) * DV]
        c = cum[:, h * DK:(h + 1) * DK]
        ce = cex[:, h * DK:(h + 1) * DK]
        st = st_ref[h]
        o = lax.dot_general((q * jnp.exp(c)).astype(BF16), st.astype(BF16), _NT_DIMS,
                            preferred_element_type=F32)
        blocks = []
        for blk in range(CH // SB):
            r0 = blk * SB
            if rev:
                a = ce[r0 + SB - 1:r0 + SB, :]
                live = rows_k >= r0
                keep = (s_idx >= t_idx + r0)
            else:
                a = ce[r0:r0 + 1, :]
                live = rows_k < r0 + SB
                keep = (s_idx <= t_idx + r0)
            qb = (q[r0:r0 + SB] * jnp.exp(c[r0:r0 + SB] - a)).astype(BF16)
            kb = (k * jnp.exp(jnp.where(live, a - c, 0.0))).astype(BF16)
            s = lax.dot_general(qb, kb, _NT_DIMS, preferred_element_type=F32)
            blocks.append(jnp.where(keep, s, 0.0))
        scores = jnp.concatenate(blocks, axis=0).astype(BF16)
        o_ref[:, h * DV:(h + 1) * DV] = o + _dot(scores, v)
        tot = c[0:1, :] if rev else c[CH - 1:CH, :]
        kd = (k * jnp.exp(tot - c)).astype(BF16)
        st_ref[h] = st * jnp.exp(tot) + lax.dot_general(v, kd, _TN_DIMS,
                                                        preferred_element_type=F32)


def _gla_kernel(qkf_ref, qkb_ref, vf_ref, vb_ref, rf_ref, rb_ref, wg_ref, bg_ref,
                of_ref, ob_ref, stf_ref, stb_ref):
    @pl.when(pl.program_id(1) == 0)
    def _():
        stf_ref[...] = jnp.zeros_like(stf_ref)
        stb_ref[...] = jnp.zeros_like(stb_ref)

    _gla_dir(qkf_ref, vf_ref, rf_ref, wg_ref, bg_ref, of_ref, stf_ref, rev=False)
    _gla_dir(qkb_ref, vb_ref, rb_ref, wg_ref, bg_ref, ob_ref, stb_ref, rev=True)


def _bwd_chunk(j):
    return jnp.where(j < NCH_CTX, NCH_CTX - 1 - j, NCH + NCH_CTX - 1 - j)


def _gla(qk, vgp, r, wg, bg):
    qk3 = qk.reshape(B, LT, 2 * KEYW)
    vgp3 = vgp.reshape(B, LT, IN_MAIN - 2 * KEYW)
    r3 = r.reshape(B, LT, LANES)
    fwd = lambda b, j: (b, j, 0)
    bwd = lambda b, j: (b, _bwd_chunk(j), 0)
    o_f, o_b = pl.pallas_call(
        _gla_kernel,
        grid=(B, NCH),
        in_specs=[
            pl.BlockSpec((None, CH, 2 * KEYW), fwd),
            pl.BlockSpec((None, CH, 2 * KEYW), bwd),
            pl.BlockSpec((None, CH, GLA_W), fwd),
            pl.BlockSpec((None, CH, GLA_W), bwd),
            pl.BlockSpec((None, CH, LANES), fwd),
            pl.BlockSpec((None, CH, LANES), bwd),
            pl.BlockSpec((2, LANES, KEYW), lambda b, j: (0, 0, 0)),
            pl.BlockSpec((2, 1, KEYW), lambda b, j: (0, 0, 0)),
        ],
        out_specs=[
            pl.BlockSpec((None, CH, GLA_W), fwd),
            pl.BlockSpec((None, CH, GLA_W), bwd),
        ],
        out_shape=[jax.ShapeDtypeStruct((B, LT, GLA_W), F32)] * 2,
        scratch_shapes=[pltpu.VMEM((H, DV, DK), F32), pltpu.VMEM((H, DV, DK), F32)],
        compiler_params=_cparams(("parallel", "arbitrary")),
        name="gla",
    )(qk3, qk3, vgp3, vgp3, r3, r3, wg, bg)
    return o_f.reshape(NT, GLA_W), o_b.reshape(NT, GLA_W)


def _pool_tables():
    band = np.zeros((2, PG, TM, TM), np.float32)
    inv = np.zeros((2, PG, TM, PGW), np.float32)
    for kind, row_len in enumerate((GRID_W, CTX)):
        for gi, w in enumerate(WINDOWS):
            for n in range(TM):
                base, j = (n // row_len) * row_len, n % row_len
                lo = min(max(j - w // 2, 0), row_len)
                hi = min(max(j - w // 2 + w, 0), row_len)
                band[kind, gi, n, base + lo:base + hi] = 1.0
                inv[kind, gi, n, :] = 1.0 / (hi - lo)
    return band, inv


def _mix_kernel(with_router, x_ref, of_ref, ob_ref, g_ref, p_ref, mod_ref, n2_ref, gain_ref,
                band_ref, inv_ref, wp_ref, ps_ref, wo_ref, *rest):
    if with_router:
        wr_ref, xo_ref, h2_ref, lg_ref = rest
    else:
        xo_ref, h2_ref = rest
    mod = mod_ref[0]
    o = of_ref[...] + ob_ref[...]
    on = jnp.concatenate([_rms(o[:, h * DV:(h + 1) * DV]) for h in range(H)], axis=1)
    read = (on * gain_ref[...] * _silu(g_ref[...].astype(F32))).astype(BF16)
    y = _dot(read, wo_ref[0:GLA_W, :])
    pouts = []
    for gi in range(PG):
        pg = p_ref[:, gi * PGW:(gi + 1) * PGW]
        mixed = _dot(band_ref[0, gi], pg) * inv_ref[0, gi] - pg.astype(F32)
        pouts.append(_dot(mixed.astype(BF16), wp_ref[gi]))
    pool = (jnp.concatenate(pouts, axis=1) * ps_ref[...]).astype(BF16)
    y = y + _dot(pool, wo_ref[GLA_W:, :])
    xn = x_ref[...] + mod[:, 2 * D:3 * D] * y
    xo_ref[...] = xn
    h2 = _rms(xn) * n2_ref[...] * (1.0 + mod[:, 4 * D:5 * D]) + mod[:, 3 * D:4 * D]
    if with_router:
        for cb in range(SLAB):
            h2_ref[pl.ds(cb, TM, stride=SLAB), :] = h2[:, cb * LANES:(cb + 1) * LANES]
        lg_ref[...] = _dot3(h2, wr_ref[...])
    else:
        h2_ref[...] = h2.astype(BF16)


def _mix(x, o_f, o_b, vgp, mod3, n2, gain, band, inv, wp, ps, wo, wr=None):
    with_router = wr is not None
    kind = lambda i: (jnp.where(i % TPB == 0, 1, 0), 0, 0, 0)
    const2 = lambda i: (0, 0)
    in_specs = [
        pl.BlockSpec((TM, D), lambda i: (i, 0)),
        pl.BlockSpec((TM, GLA_W), lambda i: (i, 0)),
        pl.BlockSpec((TM, GLA_W), lambda i: (i, 0)),
        pl.BlockSpec((TM, GLA_W), lambda i: (i, 1)),
        pl.BlockSpec((TM, POOL_W), lambda i: (i, 2)),
        pl.BlockSpec((1, 1, 6 * D), lambda i: (_mod_row(i), 0, 0)),
        pl.BlockSpec((1, D), const2),
        pl.BlockSpec((1, GLA_W), const2),
        pl.BlockSpec((1, PG, TM, TM), kind),
        pl.BlockSpec((1, PG, TM, PGW), kind),
        pl.BlockSpec((PG, PGW, PGW), lambda i: (0, 0, 0)),
        pl.BlockSpec((1, POOL_W), const2),
        pl.BlockSpec((D, D), const2),
    ]
    args = [x, o_f, o_b, vgp, vgp, mod3, n2, gain, band, inv, wp, ps, wo]
    if with_router:
        h2_spec = pl.BlockSpec((TM * SLAB, LANES), lambda i: (i, 0))
        h2_shape = jax.ShapeDtypeStruct((NT * SLAB, LANES), F32)
    else:
        h2_spec = pl.BlockSpec((TM, D), lambda i: (i, 0))
        h2_shape = jax.ShapeDtypeStruct((NT, D), BF16)
    out_specs = [pl.BlockSpec((TM, D), lambda i: (i, 0)), h2_spec]
    out_shape = [jax.ShapeDtypeStruct((NT, D), F32), h2_shape]
    if with_router:
        in_specs.append(pl.BlockSpec((D, LANES), const2))
        args.append(wr)
        out_specs.append(pl.BlockSpec((TM, LANES), lambda i: (i, 0)))
        out_shape.append(jax.ShapeDtypeStruct((NT, LANES), F32))
    return pl.pallas_call(
        functools.partial(_mix_kernel, with_router),
        grid=(NTILES,),
        in_specs=in_specs,
        out_specs=out_specs,
        out_shape=out_shape,
        compiler_params=_cparams(("parallel",)),
        name="mix_router" if with_router else "mix",
    )(*args)


def _ffn_kernel(sub, te_ref, ns_ref, blk_ref, xs_ref, w1_ref, w3_ref, w2_ref, o_ref):
    i = pl.program_id(0)
    j = pl.program_id(1)
    nsub = ns_ref[i]

    @pl.when(j == 0)
    def _():
        o_ref[...] = jnp.zeros_like(o_ref)

    def rows_update(rows):
        xt = xs_ref[rows, :]
        h1 = _dot(xt, w1_ref[0, 0].astype(BF16))
        h3 = _dot(xt, w3_ref[0, 0].astype(BF16))
        act = (_silu(h1) * h3).astype(BF16)
        o_ref[rows, :] += _dot(act, w2_ref[0, 0].astype(BF16))

    n_full = xs_ref.shape[0] // sub

    @pl.when(nsub == n_full)
    def _():
        rows_update(pl.ds(0, xs_ref.shape[0]))

    @pl.when(jnp.logical_and(nsub > 0, nsub < n_full))
    def _():
        def body(s, carry):
            rows_update(pl.ds(pl.multiple_of(s * sub, sub), sub))
            return carry

        lax.fori_loop(0, nsub, body, 0)


def _ffn(xs, te, ns, blk, w1, w3, w2, wl, tm, sub):
    n_tiles = xs.shape[0] // tm
    live_j = lambda i, j, te, ns, blk: jnp.where(ns[i] > 0, j, NJ - 1)
    grid_spec = pltpu.PrefetchScalarGridSpec(
        num_scalar_prefetch=3,
        grid=(n_tiles, NJ),
        in_specs=[
            pl.BlockSpec((tm, D), lambda i, j, te, ns, blk: (blk[i], 0)),
            pl.BlockSpec((1, 1, D, TF), lambda i, j, te, ns, blk: (wl, te[i], 0, live_j(i, j, te, ns, blk))),
            pl.BlockSpec((1, 1, D, TF), lambda i, j, te, ns, blk: (wl, te[i], 0, live_j(i, j, te, ns, blk))),
            pl.BlockSpec((1, 1, TF, D), lambda i, j, te, ns, blk: (wl, te[i], live_j(i, j, te, ns, blk), 0)),
        ],
        out_specs=pl.BlockSpec((tm, D), lambda i, j, te, ns, blk: (i, 0)),
    )
    return pl.pallas_call(
        functools.partial(_ffn_kernel, sub),
        grid_spec=grid_spec,
        out_shape=jax.ShapeDtypeStruct((xs.shape[0], D), F32),
        compiler_params=_cparams(("arbitrary", "arbitrary")),
        name="ffn_%d" % tm,
    )(te, ns, blk, xs, w1, w3, w2)


def _resid_kernel(x_ref, y_ref, mod_ref, o_ref):
    o_ref[...] = x_ref[...] + mod_ref[0][:, 5 * D:6 * D] * y_ref[...]


def _resid(x, y, mod3):
    return pl.pallas_call(
        _resid_kernel,
        grid=(NTILES,),
        in_specs=[
            pl.BlockSpec((TM, D), lambda i: (i, 0)),
            pl.BlockSpec((TM, D), lambda i: (i, 0)),
            pl.BlockSpec((1, 1, 6 * D), lambda i: (_mod_row(i), 0, 0)),
        ],
        out_specs=pl.BlockSpec((TM, D), lambda i: (i, 0)),
        out_shape=jax.ShapeDtypeStruct((NT, D), F32),
        compiler_params=_cparams(("parallel",)),
        name="resid",
    )(x, y, mod3)


def _lat_tile(t):
    return t + t // (TPB - 1) + 1


def _router_kernel(lg_ref, br_ref, route_ref, cnt_ref, carry_ref):
    @pl.when(pl.program_id(0) == 0)
    def _():
        carry_ref[...] = jnp.zeros_like(carry_ref)

    lane = lax.broadcasted_iota(jnp.int32, (TM, LANES), 1)
    z = jnp.where(lane < NE, lg_ref[...] + br_ref[...], -jnp.inf)
    m0 = jnp.max(z, axis=-1, keepdims=True)
    lane_f = lane.astype(F32)
    e0 = jnp.min(jnp.where(z == m0, lane_f, float(LANES)), axis=-1, keepdims=True)
    z1 = jnp.where(lane_f == e0, -jnp.inf, z)
    m1 = jnp.max(z1, axis=-1, keepdims=True)
    e1 = jnp.min(jnp.where(z1 == m1, lane_f, float(LANES)), axis=-1, keepdims=True)
    t = jnp.exp(m1 - m0)
    p0 = 1.0 / (1.0 + t)
    p1 = t / (1.0 + t)
    oh0 = lane_f == e0
    oh1 = lane_f == e1
    oh = jnp.where(jnp.logical_or(oh0, oh1), 1.0, 0.0)
    row = lax.broadcasted_iota(jnp.int32, (TM, TM), 0)
    col = lax.broadcasted_iota(jnp.int32, (TM, TM), 1)
    before = jnp.where(col < row, 1.0, 0.0).astype(BF16)
    excl = _dot(before, oh.astype(BF16)) + carry_ref[0:1, :]
    rank0 = jnp.sum(jnp.where(oh0, excl, 0.0), axis=-1, keepdims=True)
    rank1 = jnp.sum(jnp.where(oh1, excl, 0.0), axis=-1, keepdims=True)
    total = carry_ref[0:1, :] + jnp.sum(oh, axis=0, keepdims=True)
    carry_ref[...] = jnp.broadcast_to(total, carry_ref.shape)
    cnt_ref[...] = jnp.broadcast_to(total, cnt_ref.shape)
    route = jnp.where(lane == 0, e0, 0.0)
    route = jnp.where(lane == 1, e1, route)
    route = jnp.where(lane == 2, rank0, route)
    route = jnp.where(lane == 3, rank1, route)
    route = jnp.where(lane == 4, p0, route)
    route = jnp.where(lane == 5, p1, route)
    route_ref[...] = route


def _router(logits, b_r):
    n_lat_tiles = B * SEQ // TM
    return pl.pallas_call(
        _router_kernel,
        grid=(n_lat_tiles,),
        in_specs=[
            pl.BlockSpec((TM, LANES), lambda t: (_lat_tile(t), 0)),
            pl.BlockSpec((1, LANES), lambda t: (0, 0)),
        ],
        out_specs=[
            pl.BlockSpec((TM, LANES), lambda t: (t, 0)),
            pl.BlockSpec((8, LANES), lambda t: (0, 0)),
        ],
        out_shape=[jax.ShapeDtypeStruct((B * SEQ, LANES), F32),
                   jax.ShapeDtypeStruct((8, LANES), F32)],
        scratch_shapes=[pltpu.VMEM((8, LANES), F32)],
        compiler_params=_cparams(("arbitrary",)),
        name="router",
    )(logits, b_r)


def _row_copy(src_hbm, row, dst, r, sem):
    return pltpu.make_async_copy(src_hbm.at[pl.ds(row, 1), :], dst.at[pl.ds(r, 1), :], sem)


def _slab_copy(src_hbm, tok, dst, r, sem):
    return pltpu.make_async_copy(src_hbm.at[pl.ds(pl.multiple_of(tok * SLAB, SLAB), SLAB), :],
                                 dst.at[pl.ds(r * SLAB_PITCH, SLAB), :], sem)


def _slab_wait_all(src_hbm, dst, sem):
    pltpu.make_async_copy(src_hbm.at[pl.ds(0, TM * SLAB), :], dst.at[pl.ds(0, TM * SLAB), :], sem).wait()


def _gather_kernel(live_ref, idx_ref, h_hbm, o_ref, buf_ref, sem):
    live = live_ref[pl.program_id(0)]

    @pl.when(live == 0)
    def _():
        o_ref[...] = jnp.zeros_like(o_ref)

    @pl.when(live != 0)
    def _():
        def issue(r, carry):
            _slab_copy(h_hbm, idx_ref[0, 0, r], buf_ref, r, sem).start()
            return carry

        lax.fori_loop(0, TM, issue, 0, unroll=8)
        _slab_wait_all(h_hbm, buf_ref, sem)
        for cb in range(SLAB):
            o_ref[:, cb * LANES:(cb + 1) * LANES] = buf_ref[pl.ds(cb, TM, stride=SLAB_PITCH), :].astype(BF16)


def _gather(h2_slabs, src, live):
    n = src.shape[0] // TM
    grid_spec = pltpu.PrefetchScalarGridSpec(
        num_scalar_prefetch=1,
        grid=(n,),
        in_specs=[
            pl.BlockSpec((1, 1, TM), lambda i, live: (i, 0, 0), memory_space=pltpu.SMEM),
            pl.BlockSpec(memory_space=pl.ANY),
        ],
        out_specs=pl.BlockSpec((TM, D), lambda i, live: (i, 0)),
        scratch_shapes=[pltpu.VMEM((TM * SLAB_PITCH, LANES), F32), pltpu.SemaphoreType.DMA(())],
    )
    return pl.pallas_call(
        _gather_kernel,
        grid_spec=grid_spec,
        out_shape=jax.ShapeDtypeStruct((src.shape[0], D), BF16),
        compiler_params=_cparams(("arbitrary",)),
        name="dispatch",
    )(live, src.reshape(n, 1, TM), h2_slabs)


def _combine_kernel(p0_ref, p1_ref, x_ref, route_ref, mod_ref, fg_ref, y_hbm, o_ref,
                    b0_ref, b1_ref, sem0, sem1):
    def issue(r, carry):
        _row_copy(y_hbm, p0_ref[0, 0, r], b0_ref, r, sem0).start()
        _row_copy(y_hbm, p1_ref[0, 0, r], b1_ref, r, sem1).start()
        return carry

    lax.fori_loop(0, TM, issue, 0)

    def drain(r, carry):
        _row_copy(y_hbm, 0, b0_ref, r, sem0).wait()
        _row_copy(y_hbm, 0, b1_ref, r, sem1).wait()
        return carry

    lax.fori_loop(0, TM, drain, 0)
    route = route_ref[...]
    y = route[:, 4:5] * b0_ref[...] + route[:, 5:6] * b1_ref[...]
    xn = x_ref[...] + mod_ref[0][:, 5 * D:6 * D] * y
    o_ref[...] = _rms(xn) * fg_ref[...]


def _combine(x, ys, pos0, pos1, route, mod3, fg):
    n = B * SEQ // TM
    smem = lambda: pl.BlockSpec((1, 1, TM), lambda t: (t, 0, 0), memory_space=pltpu.SMEM)
    return pl.pallas_call(
        _combine_kernel,
        grid=(n,),
        in_specs=[
            smem(), smem(),
            pl.BlockSpec((TM, D), lambda t: (_lat_tile(t), 0)),
            pl.BlockSpec((TM, LANES), lambda t: (t, 0)),
            pl.BlockSpec((1, 1, 6 * D), lambda t: (t // (TPB - 1), 0, 0)),
            pl.BlockSpec((1, D), lambda t: (0, 0)),
            pl.BlockSpec(memory_space=pl.ANY),
        ],
        out_specs=pl.BlockSpec((TM, D), lambda t: (t, 0)),
        out_shape=jax.ShapeDtypeStruct((B * SEQ, D), F32),
        scratch_shapes=[pltpu.VMEM((TM, D), F32), pltpu.VMEM((TM, D), F32),
                        pltpu.SemaphoreType.DMA(()), pltpu.SemaphoreType.DMA(())],
        compiler_params=_cparams(("arbitrary",)),
        name="combine",
    )(pos0.reshape(n, 1, TM), pos1.reshape(n, 1, TM), x, route, mod3, fg, ys)


def _moe_plan(route, counts):
    e = route[:, 0:2].astype(jnp.int32)
    rank = route[:, 2:4].astype(jnp.int32)
    cnt = counts[0, :NE].astype(jnp.int32)
    ntile = (cnt + TM_MOE - 1) // TM_MOE
    tile_end = jnp.cumsum(ntile)
    tile_start = tile_end - ntile
    pos = tile_start[e] * TM_MOE + rank
    n_used = tile_end[NE - 1]
    tiles = jnp.arange(NT_MOE, dtype=jnp.int32)
    blk = jnp.minimum(tiles, n_used - 1)
    te = jnp.sum((blk[:, None] >= tile_end[None, :NE - 1]).astype(jnp.int32), axis=1)
    left = cnt[te] - (blk - tile_start[te]) * TM_MOE
    ns = jnp.clip((left + SUB_MOE - 1) // SUB_MOE, 0, TM_MOE // SUB_MOE)
    ns = jnp.where(tiles < n_used, ns, 0).astype(jnp.int32)
    tok = jnp.arange(B * SEQ, dtype=jnp.int32)
    tok_row = (tok // SEQ) * LT + CTX + tok % SEQ
    src = jnp.full((R_MOE,), CTX, jnp.int32).at[pos.reshape(-1)].set(jnp.repeat(tok_row, 2))
    subs = jnp.arange(TM_MOE // SUB_MOE, dtype=jnp.int32)
    live = (subs[None, :] < ns[:, None]).astype(jnp.int32).reshape(-1)
    return pos[:, 0], pos[:, 1], te, ns, blk, src, live


def kernel(x, c, ctx, c_ctx, w_ada, b_ada, norm1_g, norm2_g, w_in, w_gate, b_gate, gla_norm_g,
           w_pool, pool_scale, w_out, ffn_w1, ffn_w3, ffn_w2, moe_w_router, moe_b_router,
           moe_w1, moe_w3, moe_w2, final_g):
    assert x.shape == (B, SEQ, D) and ctx.shape == (B, CTX, D) and DEPTH == 2

    cvec = jnp.concatenate([c, c_ctx[None, :], jnp.zeros((8 - B - 1, D), F32)], axis=0)
    mods = _ada(cvec, w_ada, b_ada)
    xs = jnp.concatenate([ctx, x], axis=1).reshape(NT, D)

    band_np, inv_np = _pool_tables()
    band = jnp.asarray(band_np, BF16)
    inv = jnp.asarray(inv_np, F32)

    dense_te = jnp.zeros((NT // TM_DENSE,), jnp.int32)
    dense_ns = jnp.full((NT // TM_DENSE,), TM_DENSE // SUB_DENSE, jnp.int32)
    dense_blk = jnp.arange(NT // TM_DENSE, dtype=jnp.int32)

    out = None
    for l in range(DEPTH):
        mod3 = mods[l, :3].reshape(3, 1, 6 * D)
        w_in_l = jnp.pad(w_in[l], ((0, 0), (0, IN_PAD - w_in.shape[2]))).astype(BF16)
        wg = jnp.zeros((2, LANES, KEYW), F32)
        wg = wg.at[0, 0:RANK].set(w_gate[l, 0]).at[1, RANK:2 * RANK].set(w_gate[l, 1]).astype(BF16)
        bg = b_gate[l].reshape(2, 1, KEYW)

        qk, vgp, r = _inproj(xs, mod3, norm1_g[l].reshape(1, D), w_in_l)
        o_f, o_b = _gla(qk, vgp, r, wg, bg)
        mix_args = (xs, o_f, o_b, vgp, mod3, norm2_g[l].reshape(1, D), gla_norm_g[l].reshape(1, GLA_W),
                    band, inv, w_pool[l].astype(BF16), pool_scale[l].reshape(1, POOL_W),
                    w_out[l].astype(BF16))
        if l % 2 == 0:
            x1, h2 = _mix(*mix_args)
            jl = l // 2
            ys = _ffn(h2, dense_te, dense_ns, dense_blk, ffn_w1[:, None], ffn_w3[:, None],
                      ffn_w2[:, None], jl, TM_DENSE, SUB_DENSE)
            xs = _resid(x1, ys, mod3)
        else:
            jl = l // 2
            wr = jnp.pad(moe_w_router[jl], ((0, 0), (0, LANES - NE)))
            br = jnp.pad(moe_b_router[jl], (0, LANES - NE)).reshape(1, LANES)
            x1, h2, logits = _mix(*mix_args, wr=wr)
            route, counts = _router(logits, br)
            pos0, pos1, te, ns, blk, src, live = _moe_plan(route, counts)
            xg = _gather(h2, src, live)
            ys = _ffn(xg, te, ns, blk, moe_w1, moe_w3, moe_w2, jl, TM_MOE, SUB_MOE)
            out = _combine(x1, ys, pos0, pos1, route, mod3, final_g.reshape(1, D))
    return out.reshape(B, SEQ, D)
```

```python
import functools

import numpy as np
import jax
import jax.numpy as jnp
from jax import lax
from jax.experimental import pallas as pl
from jax.experimental.pallas import tpu as pltpu

F32 = jnp.float32
BF16 = jnp.bfloat16

D = 2048
B = 2
SEQ = 4096
CTX = 256
DEPTH = 2
GRID_W = 64
EPS = 1e-6
GLA_W = 1024
POOL_W = 1024
H = 4
DV = 256
DK = 128
KEYW = H * DK
RANK = 16
TAU = 16.0
WINDOWS = (2, 4, 8, 16)
PG = 4
PGW = 256
D_FF = 5632
NE = 8

LT = CTX + SEQ
NT = B * LT
TM = 256
TPB = LT // TM
NTILES = NT // TM
LANES = 128
SLAB = D // LANES
SLAB_PITCH = SLAB + 4

CH = 64
SB = 16
RB = 256
NRB = LT // RB
assert RB == CTX and RB % CH == 0

TF_DENSE = 512
TF_MOE = 256
TM_DENSE = NT // 8
SUB_DENSE = TM_DENSE // 4
TM_MOE = 1024
SUB_MOE = 256
N_ASSIGN = B * SEQ * 2
NT_MOE = N_ASSIGN // TM_MOE + NE
R_MOE = NT_MOE * TM_MOE
assert SUB_MOE == TM

VMEM_LIMIT = 56 * 1024 * 1024


def _cparams(sem, vmem=VMEM_LIMIT):
    return pltpu.CompilerParams(dimension_semantics=sem, vmem_limit_bytes=vmem)


def _split2(a):
    hi = a.astype(BF16)
    lo = (a - hi.astype(F32)).astype(BF16)
    return hi, lo


def _dot(a, b):
    return jnp.dot(a, b, preferred_element_type=F32)


def _dot3(a, b):
    ah, al = _split2(a)
    bh, bl = _split2(b)
    return _dot(ah, bh) + (_dot(al, bh) + _dot(ah, bl))


def _silu(a):
    return a / (1.0 + jnp.exp(-a))


def _rms(x):
    return x * lax.rsqrt(jnp.mean(x * x, axis=-1, keepdims=True) + EPS)


def _mod_row(i):
    return jnp.where(i % TPB == 0, 2, i // TPB)


ADA_TN = 1024


def _ada_kernel(c_ref, w_ref, b_ref, o_ref):
    s = _silu(c_ref[...])
    o_ref[0] = _dot3(s, w_ref[0]) + b_ref[0]


def _ada(cvec, w_ada, b_ada):
    n = 6 * D
    return pl.pallas_call(
        _ada_kernel,
        grid=(DEPTH, n // ADA_TN),
        in_specs=[
            pl.BlockSpec((8, D), lambda l, j: (0, 0)),
            pl.BlockSpec((1, D, ADA_TN), lambda l, j: (l, 0, j)),
            pl.BlockSpec((1, 1, ADA_TN), lambda l, j: (l, 0, j)),
        ],
        out_specs=pl.BlockSpec((1, 8, ADA_TN), lambda l, j: (l, 0, j)),
        out_shape=jax.ShapeDtypeStruct((DEPTH, 8, n), F32),
        compiler_params=_cparams(("parallel", "parallel")),
        name="ada",
    )(cvec, w_ada, b_ada.reshape(DEPTH, 1, n))


IN_MAIN = 2 * KEYW + 2 * GLA_W + POOL_W
IN_PAD = IN_MAIN + LANES


def _inproj_kernel(x_ref, mod_ref, g_ref, w_ref, qk_ref, vgp_ref, r_ref):
    mod = mod_ref[0]
    h = _rms(x_ref[...]) * g_ref[...] * (1.0 + mod[:, D:2 * D]) + mod[:, 0:D]
    u = _dot(h.astype(BF16), w_ref[...])
    qk_ref[...] = u[:, :2 * KEYW]
    vgp_ref[...] = u[:, 2 * KEYW:IN_MAIN].astype(BF16)
    r_ref[...] = u[:, IN_MAIN:]


def _inproj(x, mod3, g, w, l):
    return pl.pallas_call(
        _inproj_kernel,
        grid=(NTILES,),
        in_specs=[
            pl.BlockSpec((TM, D), lambda i: (i, 0)),
            pl.BlockSpec((1, 1, 6 * D), lambda i: (_mod_row(i), 0, 0)),
            pl.BlockSpec((1, D), lambda i: (0, 0)),
            pl.BlockSpec((None, D, IN_PAD), lambda i: (l, 0, 0)),
        ],
        out_specs=[
            pl.BlockSpec((TM, 2 * KEYW), lambda i: (i, 0)),
            pl.BlockSpec((TM, IN_MAIN - 2 * KEYW), lambda i: (i, 0)),
            pl.BlockSpec((TM, LANES), lambda i: (i, 0)),
        ],
        out_shape=[
            jax.ShapeDtypeStruct((NT, 2 * KEYW), F32),
            jax.ShapeDtypeStruct((NT, IN_MAIN - 2 * KEYW), BF16),
            jax.ShapeDtypeStruct((NT, LANES), F32),
        ],
        compiler_params=_cparams(("parallel",)),
        name="inproj",
    )(x, mod3, g, w)


_NT_DIMS = (((1,), (1,)), ((), ()))
_TN_DIMS = (((0,), (0,)), ((), ()))


def _gla_gates(r_ref, wg_ref, bg_ref, rev):
    d = 1 if rev else 0
    pre = _dot(r_ref[...].astype(BF16), wg_ref[d]) + bg_ref[d]
    lg = (jnp.minimum(pre, 0.0) - jnp.log(1.0 + jnp.exp(-jnp.abs(pre)))) * (1.0 / TAU)
    row = lax.broadcasted_iota(jnp.int32, (RB, RB), 0)
    col = lax.broadcasted_iota(jnp.int32, (RB, RB), 1)
    inside = jnp.logical_and(row // CH == col // CH, (col >= row) if rev else (col <= row))
    tri = jnp.where(inside, 1.0, 0.0).astype(BF16)
    lg_hi, lg_lo = _split2(lg)
    cum = _dot(tri, lg_hi) + _dot(tri, lg_lo)
    return cum, cum - lg


def _gla_kernel(qkf_ref, qkb_ref, vf_ref, vb_ref, rf_ref, rb_ref, wg_ref, bg_ref,
                of_ref, ob_ref, stf_ref, stb_ref):
    @pl.when(pl.program_id(1) == 0)
    def _():
        stf_ref[...] = jnp.zeros_like(stf_ref)
        stb_ref[...] = jnp.zeros_like(stb_ref)

    nc = RB // CH
    dirs = ((qkf_ref, vf_ref, rf_ref, of_ref, stf_ref, False),
            (qkb_ref, vb_ref, rb_ref, ob_ref, stb_ref, True))
    rows3 = lax.broadcasted_iota(jnp.int32, (nc, CH, KEYW), 1)
    t_idx = lax.broadcasted_iota(jnp.int32, (SB, CH), 0)
    s_idx = lax.broadcasted_iota(jnp.int32, (SB, CH), 1)

    gates = [_gla_gates(r_ref, wg_ref, bg_ref, rev) for (_, _, r_ref, _, _, rev) in dirs]

    scores = {}
    for blk in range(CH // SB):
        r0 = blk * SB
        for d, (qk_ref, _, _, _, _, rev) in enumerate(dirs):
            cum, cex = gates[d]
            c3 = cum.reshape(nc, CH, KEYW)
            ce3 = cex.reshape(nc, CH, KEYW)
            q3 = (qk_ref[:, 0:KEYW] * (DK ** -0.5)).reshape(nc, CH, KEYW)
            k3 = qk_ref[:, KEYW:2 * KEYW].reshape(nc, CH, KEYW)
            if rev:
                a = ce3[:, r0 + SB - 1:r0 + SB, :]
                live = rows3 >= r0
                keep = s_idx >= t_idx + r0
            else:
                a = ce3[:, r0:r0 + 1, :]
                live = rows3 < r0 + SB
                keep = s_idx <= t_idx + r0
            qb = (q3[:, r0:r0 + SB, :] * jnp.exp(c3[:, r0:r0 + SB, :] - a)).astype(BF16)
            kb = (k3 * jnp.exp(jnp.where(live, a - c3, 0.0))).astype(BF16)
            for ci in range(nc):
                for h in range(H):
                    hk = slice(h * DK, (h + 1) * DK)
                    s = lax.dot_general(qb[ci][:, hk], kb[ci][:, hk], _NT_DIMS,
                                        preferred_element_type=F32)
                    scores.setdefault((d, ci, h), []).append(jnp.where(keep, s, 0.0))

    for d, (_, v_ref, _, o_ref, _, _) in enumerate(dirs):
        for ci in range(nc):
            rs = slice(ci * CH, (ci + 1) * CH)
            for h in range(H):
                hv = slice(h * DV, (h + 1) * DV)
                sc = jnp.concatenate(scores[(d, ci, h)], axis=0).astype(BF16)
                o_ref[rs, hv] = _dot(sc, v_ref[rs, hv])

    for d, (qk_ref, v_ref, _, o_ref, st_ref, rev) in enumerate(dirs):
        cum = gates[d][0]
        qe = (qk_ref[:, 0:KEYW] * (DK ** -0.5) * jnp.exp(cum)).astype(BF16)
        for h in range(H):
            hk = slice(h * DK, (h + 1) * DK)
            hv = slice(h * DV, (h + 1) * DV)
            st = st_ref[h]
            for ci in (reversed(range(nc)) if rev else range(nc)):
                rs = slice(ci * CH, (ci + 1) * CH)
                c = cum[rs, hk]
                o_ref[rs, hv] += lax.dot_general(qe[rs, hk], st.astype(BF16), _NT_DIMS,
                                                 preferred_element_type=F32)
                tot = c[0:1, :] if rev else c[CH - 1:CH, :]
                kd = (qk_ref[rs, KEYW + h * DK:KEYW + (h + 1) * DK] * jnp.exp(tot - c)).astype(BF16)
                st = st * jnp.exp(tot) + lax.dot_general(v_ref[rs, hv], kd, _TN_DIMS,
                                                         preferred_element_type=F32)
            st_ref[h] = st


def _bwd_block(j):
    return jnp.where(j == 0, 0, NRB - j)


def _gla(qk, vgp, r, wg, bg):
    qk3 = qk.reshape(B, LT, 2 * KEYW)
    vgp3 = vgp.reshape(B, LT, IN_MAIN - 2 * KEYW)
    r3 = r.reshape(B, LT, LANES)
    fwd = lambda b, j: (b, j, 0)
    bwd = lambda b, j: (b, _bwd_block(j), 0)
    o_f, o_b = pl.pallas_call(
        _gla_kernel,
        grid=(B, NRB),
        in_specs=[
            pl.BlockSpec((None, RB, 2 * KEYW), fwd),
            pl.BlockSpec((None, RB, 2 * KEYW), bwd),
            pl.BlockSpec((None, RB, GLA_W), fwd),
            pl.BlockSpec((None, RB, GLA_W), bwd),
            pl.BlockSpec((None, RB, LANES), fwd),
            pl.BlockSpec((None, RB, LANES), bwd),
            pl.BlockSpec((2, LANES, KEYW), lambda b, j: (0, 0, 0)),
            pl.BlockSpec((2, 1, KEYW), lambda b, j: (0, 0, 0)),
        ],
        out_specs=[
            pl.BlockSpec((None, RB, GLA_W), fwd),
            pl.BlockSpec((None, RB, GLA_W), bwd),
        ],
        out_shape=[jax.ShapeDtypeStruct((B, LT, GLA_W), F32)] * 2,
        scratch_shapes=[pltpu.VMEM((H, DV, DK), F32), pltpu.VMEM((H, DV, DK), F32)],
        compiler_params=_cparams(("parallel", "arbitrary")),
        name="gla",
    )(qk3, qk3, vgp3, vgp3, r3, r3, wg, bg)
    return o_f.reshape(NT, GLA_W), o_b.reshape(NT, GLA_W)


def _pool_tables():
    band = np.zeros((2, PG, TM, TM), np.float32)
    inv = np.zeros((2, PG, TM, PGW), np.float32)
    for kind, row_len in enumerate((GRID_W, CTX)):
        for gi, w in enumerate(WINDOWS):
            for n in range(TM):
                base, j = (n // row_len) * row_len, n % row_len
                lo = min(max(j - w // 2, 0), row_len)
                hi = min(max(j - w // 2 + w, 0), row_len)
                band[kind, gi, n, base + lo:base + hi] = 1.0
                inv[kind, gi, n, :] = 1.0 / (hi - lo)
    return band, inv


def _mix_kernel(with_router, x_ref, of_ref, ob_ref, g_ref, p_ref, mod_ref, n2_ref, gain_ref,
                band_ref, inv_ref, wp_ref, ps_ref, wo_ref, *rest):
    if with_router:
        wr_ref, xo_ref, h2_ref, lg_ref = rest
    else:
        xo_ref, h2_ref = rest
    mod = mod_ref[0]
    o = of_ref[...] + ob_ref[...]
    on = jnp.concatenate([_rms(o[:, h * DV:(h + 1) * DV]) for h in range(H)], axis=1)
    read = (on * gain_ref[...] * _silu(g_ref[...].astype(F32))).astype(BF16)
    y = _dot(read, wo_ref[0:GLA_W, :])
    pouts = []
    for gi in range(PG):
        pg = p_ref[:, gi * PGW:(gi + 1) * PGW]
        mixed = _dot(band_ref[0, gi], pg) * inv_ref[0, gi] - pg.astype(F32)
        pouts.append(_dot(mixed.astype(BF16), wp_ref[gi]))
    pool = (jnp.concatenate(pouts, axis=1) * ps_ref[...]).astype(BF16)
    y = y + _dot(pool, wo_ref[GLA_W:, :])
    xn = x_ref[...] + mod[:, 2 * D:3 * D] * y
    xo_ref[...] = xn
    h2 = _rms(xn) * n2_ref[...] * (1.0 + mod[:, 4 * D:5 * D]) + mod[:, 3 * D:4 * D]
    if with_router:
        for cb in range(SLAB):
            h2_ref[pl.ds(cb, TM, stride=SLAB), :] = h2[:, cb * LANES:(cb + 1) * LANES]
        lg_ref[...] = _dot3(h2, wr_ref[...])
    else:
        h2_ref[...] = h2.astype(BF16)


def _mix(x, o_f, o_b, vgp, mod3, n2, gain, band, inv, wp, ps, wo, l, wr=None):
    with_router = wr is not None
    kind = lambda i: (jnp.where(i % TPB == 0, 1, 0), 0, 0, 0)
    const2 = lambda i: (0, 0)
    in_specs = [
        pl.BlockSpec((TM, D), lambda i: (i, 0)),
        pl.BlockSpec((TM, GLA_W), lambda i: (i, 0)),
        pl.BlockSpec((TM, GLA_W), lambda i: (i, 0)),
        pl.BlockSpec((TM, GLA_W), lambda i: (i, 1)),
        pl.BlockSpec((TM, POOL_W), lambda i: (i, 2)),
        pl.BlockSpec((1, 1, 6 * D), lambda i: (_mod_row(i), 0, 0)),
        pl.BlockSpec((1, D), const2),
        pl.BlockSpec((1, GLA_W), const2),
        pl.BlockSpec((1, PG, TM, TM), kind),
        pl.BlockSpec((1, PG, TM, PGW), kind),
        pl.BlockSpec((PG, PGW, PGW), lambda i: (0, 0, 0)),
        pl.BlockSpec((1, POOL_W), const2),
        pl.BlockSpec((None, D, D), lambda i: (l, 0, 0)),
    ]
    args = [x, o_f, o_b, vgp, vgp, mod3, n2, gain, band, inv, wp, ps, wo]
    if with_router:
        h2_spec = pl.BlockSpec((TM * SLAB, LANES), lambda i: (i, 0))
        h2_shape = jax.ShapeDtypeStruct((NT * SLAB, LANES), F32)
    else:
        h2_spec = pl.BlockSpec((TM, D), lambda i: (i, 0))
        h2_shape = jax.ShapeDtypeStruct((NT, D), BF16)
    out_specs = [pl.BlockSpec((TM, D), lambda i: (i, 0)), h2_spec]
    out_shape = [jax.ShapeDtypeStruct((NT, D), F32), h2_shape]
    if with_router:
        in_specs.append(pl.BlockSpec((D, LANES), const2))
        args.append(wr)
        out_specs.append(pl.BlockSpec((TM, LANES), lambda i: (i, 0)))
        out_shape.append(jax.ShapeDtypeStruct((NT, LANES), F32))
    return pl.pallas_call(
        functools.partial(_mix_kernel, with_router),
        grid=(NTILES,),
        in_specs=in_specs,
        out_specs=out_specs,
        out_shape=out_shape,
        compiler_params=_cparams(("parallel",)),
        name="mix_router" if with_router else "mix",
    )(*args)


def _ffn_kernel(sub, slab_out, te_ref, ns_ref, blk_ref, xs_ref, w1_ref, w3_ref, w2_ref, o_ref, *scratch):
    i = pl.program_id(0)
    j = pl.program_id(1)
    nsub = ns_ref[i]
    acc_ref = scratch[0] if slab_out else o_ref

    @pl.when(j == 0)
    def _():
        acc_ref[...] = jnp.zeros_like(acc_ref)

    def rows_update(rows):
        xt = xs_ref[rows, :]
        h1 = _dot(xt, w1_ref[0, 0].astype(BF16))
        h3 = _dot(xt, w3_ref[0, 0].astype(BF16))
        act = (_silu(h1) * h3).astype(BF16)
        acc_ref[rows, :] += _dot(act, w2_ref[0, 0].astype(BF16))

    n_full = xs_ref.shape[0] // sub

    @pl.when(nsub == n_full)
    def _():
        rows_update(pl.ds(0, xs_ref.shape[0]))

    @pl.when(jnp.logical_and(nsub > 0, nsub < n_full))
    def _():
        def body(s, carry):
            rows_update(pl.ds(pl.multiple_of(s * sub, sub), sub))
            return carry

        lax.fori_loop(0, nsub, body, 0)

    if slab_out:
        @pl.when(j == pl.num_programs(1) - 1)
        def _():
            tm = acc_ref.shape[0]
            for cb in range(SLAB):
                o_ref[pl.ds(cb, tm, stride=SLAB), :] = acc_ref[:, cb * LANES:(cb + 1) * LANES]


def _ffn(xs, te, ns, blk, w1, w3, w2, wl, tm, sub, tf, slab_out):
    n_tiles = xs.shape[0] // tm
    nj = D_FF // tf
    if slab_out:
        out_spec = pl.BlockSpec((tm * SLAB, LANES), lambda i, j, te, ns, blk: (i, 0))
        out_shape = jax.ShapeDtypeStruct((xs.shape[0] * SLAB, LANES), F32)
        scratch = [pltpu.VMEM((tm, D), F32)]
    else:
        out_spec = pl.BlockSpec((tm, D), lambda i, j, te, ns, blk: (i, 0))
        out_shape = jax.ShapeDtypeStruct((xs.shape[0], D), F32)
        scratch = []
    live_j = lambda i, j, te, ns, blk: jnp.where(ns[i] > 0, j, nj - 1)
    grid_spec = pltpu.PrefetchScalarGridSpec(
        num_scalar_prefetch=3,
        grid=(n_tiles, nj),
        in_specs=[
            pl.BlockSpec((tm, D), lambda i, j, te, ns, blk: (blk[i], 0)),
            pl.BlockSpec((1, 1, D, tf), lambda i, j, te, ns, blk: (wl, te[i], 0, live_j(i, j, te, ns, blk))),
            pl.BlockSpec((1, 1, D, tf), lambda i, j, te, ns, blk: (wl, te[i], 0, live_j(i, j, te, ns, blk))),
            pl.BlockSpec((1, 1, tf, D), lambda i, j, te, ns, blk: (wl, te[i], live_j(i, j, te, ns, blk), 0)),
        ],
        out_specs=out_spec,
        scratch_shapes=scratch,
    )
    return pl.pallas_call(
        functools.partial(_ffn_kernel, sub, slab_out),
        grid_spec=grid_spec,
        out_shape=out_shape,
        compiler_params=_cparams(("arbitrary", "arbitrary")),
        name="ffn_%d" % tm,
    )(te, ns, blk, xs, w1, w3, w2)


def _resid_kernel(x_ref, y_ref, mod_ref, o_ref):
    o_ref[...] = x_ref[...] + mod_ref[0][:, 5 * D:6 * D] * y_ref[...]


def _resid(x, y, mod3):
    return pl.pallas_call(
        _resid_kernel,
        grid=(NTILES,),
        in_specs=[
            pl.BlockSpec((TM, D), lambda i: (i, 0)),
            pl.BlockSpec((TM, D), lambda i: (i, 0)),
            pl.BlockSpec((1, 1, 6 * D), lambda i: (_mod_row(i), 0, 0)),
        ],
        out_specs=pl.BlockSpec((TM, D), lambda i: (i, 0)),
        out_shape=jax.ShapeDtypeStruct((NT, D), F32),
        compiler_params=_cparams(("parallel",)),
        name="resid",
    )(x, y, mod3)


def _lat_tile(t):
    return t + t // (TPB - 1) + 1


def _router_kernel(lg_ref, br_ref, route_ref, cnt_ref, carry_ref):
    @pl.when(pl.program_id(0) == 0)
    def _():
        carry_ref[...] = jnp.zeros_like(carry_ref)

    lane = lax.broadcasted_iota(jnp.int32, (TM, LANES), 1)
    z = jnp.where(lane < NE, lg_ref[...] + br_ref[...], -jnp.inf)
    m0 = jnp.max(z, axis=-1, keepdims=True)
    lane_f = lane.astype(F32)
    e0 = jnp.min(jnp.where(z == m0, lane_f, float(LANES)), axis=-1, keepdims=True)
    z1 = jnp.where(lane_f == e0, -jnp.inf, z)
    m1 = jnp.max(z1, axis=-1, keepdims=True)
    e1 = jnp.min(jnp.where(z1 == m1, lane_f, float(LANES)), axis=-1, keepdims=True)
    t = jnp.exp(m1 - m0)
    p0 = 1.0 / (1.0 + t)
    p1 = t / (1.0 + t)
    oh0 = lane_f == e0
    oh1 = lane_f == e1
    oh = jnp.where(jnp.logical_or(oh0, oh1), 1.0, 0.0)
    row = lax.broadcasted_iota(jnp.int32, (TM, TM), 0)
    col = lax.broadcasted_iota(jnp.int32, (TM, TM), 1)
    before = jnp.where(col < row, 1.0, 0.0).astype(BF16)
    excl = _dot(before, oh.astype(BF16)) + carry_ref[0:1, :]
    rank0 = jnp.sum(jnp.where(oh0, excl, 0.0), axis=-1, keepdims=True)
    rank1 = jnp.sum(jnp.where(oh1, excl, 0.0), axis=-1, keepdims=True)
    total = carry_ref[0:1, :] + jnp.sum(oh, axis=0, keepdims=True)
    carry_ref[...] = jnp.broadcast_to(total, carry_ref.shape)
    cnt_ref[...] = jnp.broadcast_to(total, cnt_ref.shape)
    route = jnp.where(lane == 0, e0, 0.0)
    route = jnp.where(lane == 1, e1, route)
    route = jnp.where(lane == 2, rank0, route)
    route = jnp.where(lane == 3, rank1, route)
    route = jnp.where(lane == 4, p0, route)
    route = jnp.where(lane == 5, p1, route)
    route_ref[...] = route


def _router(logits, b_r):
    n_lat_tiles = B * SEQ // TM
    return pl.pallas_call(
        _router_kernel,
        grid=(n_lat_tiles,),
        in_specs=[
            pl.BlockSpec((TM, LANES), lambda t: (_lat_tile(t), 0)),
            pl.BlockSpec((1, LANES), lambda t: (0, 0)),
        ],
        out_specs=[
            pl.BlockSpec((TM, LANES), lambda t: (t, 0)),
            pl.BlockSpec((8, LANES), lambda t: (0, 0)),
        ],
        out_shape=[jax.ShapeDtypeStruct((B * SEQ, LANES), F32),
                   jax.ShapeDtypeStruct((8, LANES), F32)],
        scratch_shapes=[pltpu.VMEM((8, LANES), F32)],
        compiler_params=_cparams(("arbitrary",)),
        name="router",
    )(logits, b_r)


ISSUE_UNROLL = 8


def _slab_copy(src_hbm, tok, dst, r, sem):
    return pltpu.make_async_copy(src_hbm.at[pl.ds(pl.multiple_of(tok * SLAB, SLAB), SLAB), :],
                                 dst.at[pl.ds(r * SLAB_PITCH, SLAB), :], sem)


def _slab_wait_all(src_hbm, dst, sem):
    pltpu.make_async_copy(src_hbm.at[pl.ds(0, TM * SLAB), :], dst.at[pl.ds(0, TM * SLAB), :], sem).wait()


def _gather_kernel(live_ref, idx_ref, h_hbm, o_ref, buf_ref, sem):
    live = live_ref[pl.program_id(0)]

    @pl.when(live == 0)
    def _():
        o_ref[...] = jnp.zeros_like(o_ref)

    @pl.when(live != 0)
    def _():
        def issue(g, carry):
            for u in range(ISSUE_UNROLL):
                r = g * ISSUE_UNROLL + u
                _slab_copy(h_hbm, idx_ref[0, 0, r], buf_ref, r, sem).start(priority=u % 2)
            return carry

        lax.fori_loop(0, TM // ISSUE_UNROLL, issue, 0)
        _slab_wait_all(h_hbm, buf_ref, sem)
        for cb in range(SLAB):
            o_ref[:, cb * LANES:(cb + 1) * LANES] = buf_ref[pl.ds(cb, TM, stride=SLAB_PITCH), :].astype(BF16)


def _gather(h2_slabs, src, live):
    n = src.shape[0] // TM
    grid_spec = pltpu.PrefetchScalarGridSpec(
        num_scalar_prefetch=1,
        grid=(n,),
        in_specs=[
            pl.BlockSpec((1, 1, TM), lambda i, live: (i, 0, 0), memory_space=pltpu.SMEM),
            pl.BlockSpec(memory_space=pl.ANY),
        ],
        out_specs=pl.BlockSpec((TM, D), lambda i, live: (i, 0)),
        scratch_shapes=[pltpu.VMEM((TM * SLAB_PITCH, LANES), F32), pltpu.SemaphoreType.DMA(())],
    )
    return pl.pallas_call(
        _gather_kernel,
        grid_spec=grid_spec,
        out_shape=jax.ShapeDtypeStruct((src.shape[0], D), BF16),
        compiler_params=_cparams(("arbitrary",)),
        name="dispatch",
    )(live, src.reshape(n, 1, TM), h2_slabs)


def _combine_kernel(p0_ref, p1_ref, x_ref, route_ref, mod_ref, fg_ref, y_hbm, o_ref,
                    b0_ref, b1_ref, sem0, sem1):
    def issue(g, carry):
        for u in range(ISSUE_UNROLL):
            r = g * ISSUE_UNROLL + u
            _slab_copy(y_hbm, p0_ref[0, 0, r], b0_ref, r, sem0).start(priority=0)
            _slab_copy(y_hbm, p1_ref[0, 0, r], b1_ref, r, sem1).start(priority=1)
        return carry

    lax.fori_loop(0, TM // ISSUE_UNROLL, issue, 0)
    _slab_wait_all(y_hbm, b0_ref, sem0)
    _slab_wait_all(y_hbm, b1_ref, sem1)
    route = route_ref[...]
    p0 = route[:, 4:5]
    p1 = route[:, 5:6]
    g2 = mod_ref[0][:, 5 * D:6 * D]
    ssq = jnp.zeros((TM, 1), F32)
    for cb in range(SLAB):
        cs = slice(cb * LANES, (cb + 1) * LANES)
        slab_rows = pl.ds(cb, TM, stride=SLAB_PITCH)
        xn = x_ref[:, cs] + g2[:, cs] * (p0 * b0_ref[slab_rows, :] + p1 * b1_ref[slab_rows, :])
        o_ref[:, cs] = xn
        ssq = ssq + jnp.sum(xn * xn, axis=-1, keepdims=True)
    o_ref[...] = o_ref[...] * lax.rsqrt(ssq * (1.0 / D) + EPS) * fg_ref[...]


def _combine(x, ys, pos0, pos1, route, mod3, fg):
    n = B * SEQ // TM
    smem = lambda: pl.BlockSpec((1, 1, TM), lambda t: (t, 0, 0), memory_space=pltpu.SMEM)
    return pl.pallas_call(
        _combine_kernel,
        grid=(n,),
        in_specs=[
            smem(), smem(),
            pl.BlockSpec((TM, D), lambda t: (_lat_tile(t), 0)),
            pl.BlockSpec((TM, LANES), lambda t: (t, 0)),
            pl.BlockSpec((1, 1, 6 * D), lambda t: (t // (TPB - 1), 0, 0)),
            pl.BlockSpec((1, D), lambda t: (0, 0)),
            pl.BlockSpec(memory_space=pl.ANY),
        ],
        out_specs=pl.BlockSpec((TM, D), lambda t: (t, 0)),
        out_shape=jax.ShapeDtypeStruct((B * SEQ, D), F32),
        scratch_shapes=[pltpu.VMEM((TM * SLAB_PITCH, LANES), F32), pltpu.VMEM((TM * SLAB_PITCH, LANES), F32),
                        pltpu.SemaphoreType.DMA(()), pltpu.SemaphoreType.DMA(())],
        compiler_params=_cparams(("arbitrary",)),
        name="combine",
    )(pos0.reshape(n, 1, TM), pos1.reshape(n, 1, TM), x, route, mod3, fg, ys)


def _moe_plan(route, counts):
    e = route[:, 0:2].astype(jnp.int32)
    rank = route[:, 2:4].astype(jnp.int32)
    cnt = counts[0, :NE].astype(jnp.int32)
    ntile = (cnt + TM_MOE - 1) // TM_MOE
    tile_end = jnp.cumsum(ntile)
    tile_start = tile_end - ntile
    pos = tile_start[e] * TM_MOE + rank
    n_used = tile_end[NE - 1]
    tiles = jnp.arange(NT_MOE, dtype=jnp.int32)
    blk = jnp.minimum(tiles, n_used - 1)
    te = jnp.sum((blk[:, None] >= tile_end[None, :NE - 1]).astype(jnp.int32), axis=1)
    left = cnt[te] - (blk - tile_start[te]) * TM_MOE
    ns = jnp.clip((left + SUB_MOE - 1) // SUB_MOE, 0, TM_MOE // SUB_MOE)
    ns = jnp.where(tiles < n_used, ns, 0).astype(jnp.int32)
    tok = jnp.arange(B * SEQ, dtype=jnp.int32)
    tok_row = (tok // SEQ) * LT + CTX + tok % SEQ
    src = jnp.full((R_MOE,), CTX, jnp.int32).at[pos.reshape(-1)].set(jnp.repeat(tok_row, 2))
    subs = jnp.arange(TM_MOE // SUB_MOE, dtype=jnp.int32)
    live = (subs[None, :] < ns[:, None]).astype(jnp.int32).reshape(-1)
    return pos[:, 0], pos[:, 1], te, ns, blk, src, live


def kernel(x, c, ctx, c_ctx, w_ada, b_ada, norm1_g, norm2_g, w_in, w_gate, b_gate, gla_norm_g,
           w_pool, pool_scale, w_out, ffn_w1, ffn_w3, ffn_w2, moe_w_router, moe_b_router,
           moe_w1, moe_w3, moe_w2, final_g):
    assert x.shape == (B, SEQ, D) and ctx.shape == (B, CTX, D) and DEPTH == 2

    cvec = jnp.concatenate([c, c_ctx[None, :], jnp.zeros((8 - B - 1, D), F32)], axis=0)
    mods = _ada(cvec, w_ada, b_ada)
    xs = jnp.concatenate([ctx, x], axis=1).reshape(NT, D)

    w_in_b = jnp.pad(w_in, ((0, 0), (0, 0), (0, IN_PAD - w_in.shape[2]))).astype(BF16)
    w_out_b = w_out.astype(BF16)
    w_pool_b = w_pool.astype(BF16)

    band_np, inv_np = _pool_tables()
    band = jnp.asarray(band_np, BF16)
    inv = jnp.asarray(inv_np, F32)

    dense_te = jnp.zeros((NT // TM_DENSE,), jnp.int32)
    dense_ns = jnp.full((NT // TM_DENSE,), TM_DENSE // SUB_DENSE, jnp.int32)
    dense_blk = jnp.arange(NT // TM_DENSE, dtype=jnp.int32)

    out = None
    for l in range(DEPTH):
        mod3 = mods[l, :3].reshape(3, 1, 6 * D)
        wg = jnp.zeros((2, LANES, KEYW), F32)
        wg = wg.at[0, 0:RANK].set(w_gate[l, 0]).at[1, RANK:2 * RANK].set(w_gate[l, 1]).astype(BF16)
        bg = b_gate[l].reshape(2, 1, KEYW)

        qk, vgp, r = _inproj(xs, mod3, norm1_g[l].reshape(1, D), w_in_b, l)
        o_f, o_b = _gla(qk, vgp, r, wg, bg)
        mix_args = (xs, o_f, o_b, vgp, mod3, norm2_g[l].reshape(1, D), gla_norm_g[l].reshape(1, GLA_W),
                    band, inv, w_pool_b[l], pool_scale[l].reshape(1, POOL_W), w_out_b, l)
        if l % 2 == 0:
            x1, h2 = _mix(*mix_args)
            jl = l // 2
            ys = _ffn(h2, dense_te, dense_ns, dense_blk, ffn_w1[:, None], ffn_w3[:, None],
                      ffn_w2[:, None], jl, TM_DENSE, SUB_DENSE, TF_DENSE, slab_out=False)
            xs = _resid(x1, ys, mod3)
        else:
            jl = l // 2
            wr = jnp.pad(moe_w_router[jl], ((0, 0), (0, LANES - NE)))
            br = jnp.pad(moe_b_router[jl], (0, LANES - NE)).reshape(1, LANES)
            x1, h2, logits = _mix(*mix_args, wr=wr)
            route, counts = _router(logits, br)
            pos0, pos1, te, ns, blk, src, live = _moe_plan(route, counts)
            xg = _gather(h2, src, live)
            ys = _ffn(xg, te, ns, blk, moe_w1, moe_w3, moe_w2, jl, TM_MOE, SUB_MOE, TF_MOE, slab_out=True)
            out = _combine(x1, ys, pos0, pos1, route, mod3, final_g.reshape(1, D))
    return out.reshape(B, SEQ, D)
```

```python
import functools

import numpy as np
import jax
import jax.numpy as jnp
from jax import lax
from jax.experimental import pallas as pl
from jax.experimental.pallas import tpu as pltpu

F32 = jnp.float32
BF16 = jnp.bfloat16

D = 2048
B = 2
SEQ = 4096
CTX = 256
DEPTH = 2
GRID_W = 64
EPS = 1e-6
GLA_W = 1024
POOL_W = 1024
H = 4
DV = 256
DK = 128
KEYW = H * DK
RANK = 16
TAU = 16.0
WINDOWS = (2, 4, 8, 16)
PG = 4
PGW = 256
D_FF = 5632
NE = 8

LT = CTX + SEQ
NT = B * LT
TM = 256
TPB = LT // TM
NTILES = NT // TM
LANES = 128
SLAB = D // LANES
SLAB_PITCH = SLAB + 4

CH = 64
SB = 16
RB = 256
NRB = LT // RB
assert RB == CTX and RB % CH == 0

TF_DENSE = 512
TF_MOE = 256
TM_DENSE = NT // 8
SUB_DENSE = TM_DENSE // 4
SUB_MOE = 256
SUBS_MOE = 5
TM_MOE = SUBS_MOE * SUB_MOE
N_ASSIGN = B * SEQ * 2
NT_MOE = (N_ASSIGN // SUB_MOE + NE + (SUBS_MOE - 1) * NE) // SUBS_MOE
R_MOE = NT_MOE * TM_MOE
assert SUB_MOE == TM

VMEM_LIMIT = 56 * 1024 * 1024


def _cparams(sem, vmem=VMEM_LIMIT):
    return pltpu.CompilerParams(dimension_semantics=sem, vmem_limit_bytes=vmem)


def _split2(a):
    hi = a.astype(BF16)
    lo = (a - hi.astype(F32)).astype(BF16)
    return hi, lo


def _dot(a, b):
    return jnp.dot(a, b, preferred_element_type=F32)


def _dot3(a, b):
    ah, al = _split2(a)
    bh, bl = _split2(b)
    return _dot(ah, bh) + (_dot(al, bh) + _dot(ah, bl))


def _silu(a):
    return a / (1.0 + jnp.exp(-a))


def _rms(x):
    return x * lax.rsqrt(jnp.mean(x * x, axis=-1, keepdims=True) + EPS)


def _mod_row(i):
    return jnp.where(i % TPB == 0, 2, i // TPB)


ADA_TN = 1024


def _ada_kernel(c_ref, w_ref, b_ref, o_ref):
    s = _silu(c_ref[...])
    o_ref[0] = _dot3(s, w_ref[0]) + b_ref[0]


def _ada(cvec, w_ada, b_ada):
    n = 6 * D
    return pl.pallas_call(
        _ada_kernel,
        grid=(DEPTH, n // ADA_TN),
        in_specs=[
            pl.BlockSpec((8, D), lambda l, j: (0, 0)),
            pl.BlockSpec((1, D, ADA_TN), lambda l, j: (l, 0, j)),
            pl.BlockSpec((1, 1, ADA_TN), lambda l, j: (l, 0, j)),
        ],
        out_specs=pl.BlockSpec((1, 8, ADA_TN), lambda l, j: (l, 0, j)),
        out_shape=jax.ShapeDtypeStruct((DEPTH, 8, n), F32),
        compiler_params=_cparams(("parallel", "parallel")),
        name="ada",
    )(cvec, w_ada, b_ada.reshape(DEPTH, 1, n))


IN_MAIN = 2 * KEYW + 2 * GLA_W + POOL_W


def _inproj_kernel(x_ref, mod_ref, g_ref, w_ref, wr_ref, qk_ref, vgp_ref, r_ref):
    mod = mod_ref[0]
    h = (_rms(x_ref[...]) * g_ref[...] * (1.0 + mod[:, D:2 * D]) + mod[:, 0:D]).astype(BF16)
    u = _dot(h, w_ref[...])
    qk_ref[...] = u[:, :2 * KEYW]
    vgp_ref[...] = u[:, 2 * KEYW:].astype(BF16)
    r_ref[...] = _dot(h, wr_ref[...])


def _inproj(x, mod3, g, w, wr, l):
    return pl.pallas_call(
        _inproj_kernel,
        grid=(NTILES,),
        in_specs=[
            pl.BlockSpec((TM, D), lambda i: (i, 0)),
            pl.BlockSpec((1, 1, 6 * D), lambda i: (_mod_row(i), 0, 0)),
            pl.BlockSpec((1, D), lambda i: (0, 0)),
            pl.BlockSpec((None, D, IN_MAIN), lambda i: (l, 0, 0)),
            pl.BlockSpec((None, D, LANES), lambda i: (l, 0, 0)),
        ],
        out_specs=[
            pl.BlockSpec((TM, 2 * KEYW), lambda i: (i, 0)),
            pl.BlockSpec((TM, IN_MAIN - 2 * KEYW), lambda i: (i, 0)),
            pl.BlockSpec((TM, LANES), lambda i: (i, 0)),
        ],
        out_shape=[
            jax.ShapeDtypeStruct((NT, 2 * KEYW), F32),
            jax.ShapeDtypeStruct((NT, IN_MAIN - 2 * KEYW), BF16),
            jax.ShapeDtypeStruct((NT, LANES), F32),
        ],
        compiler_params=_cparams(("parallel",)),
        name="inproj",
    )(x, mod3, g, w, wr)


_NT_DIMS = (((1,), (1,)), ((), ()))
_TN_DIMS = (((0,), (0,)), ((), ()))


def _gla_gates(r_ref, wg_ref, bg_ref, rev):
    d = 1 if rev else 0
    pre = _dot(r_ref[...].astype(BF16), wg_ref[d]) + bg_ref[d]
    lg = (jnp.minimum(pre, 0.0) - jnp.log(1.0 + jnp.exp(-jnp.abs(pre)))) * (1.0 / TAU)
    row = lax.broadcasted_iota(jnp.int32, (RB, RB), 0)
    col = lax.broadcasted_iota(jnp.int32, (RB, RB), 1)
    inside = jnp.logical_and(row // CH == col // CH, (col >= row) if rev else (col <= row))
    tri = jnp.where(inside, 1.0, 0.0).astype(BF16)
    lg_hi, lg_lo = _split2(lg)
    cum = _dot(tri, lg_hi) + _dot(tri, lg_lo)
    return cum, cum - lg


def _gla_kernel(qkf_ref, qkb_ref, vf_ref, vb_ref, rf_ref, rb_ref, wg_ref, bg_ref,
                of_ref, ob_ref, stf_ref, stb_ref):
    @pl.when(pl.program_id(1) == 0)
    def _():
        stf_ref[...] = jnp.zeros_like(stf_ref)
        stb_ref[...] = jnp.zeros_like(stb_ref)

    nc = RB // CH
    dirs = ((qkf_ref, vf_ref, rf_ref, of_ref, stf_ref, False),
            (qkb_ref, vb_ref, rb_ref, ob_ref, stb_ref, True))
    rows3 = lax.broadcasted_iota(jnp.int32, (nc, CH, KEYW), 1)
    t_idx = lax.broadcasted_iota(jnp.int32, (SB, CH), 0)
    s_idx = lax.broadcasted_iota(jnp.int32, (SB, CH), 1)

    gates = [_gla_gates(r_ref, wg_ref, bg_ref, rev) for (_, _, r_ref, _, _, rev) in dirs]

    scores = {}
    for blk in range(CH // SB):
        r0 = blk * SB
        for d, (qk_ref, _, _, _, _, rev) in enumerate(dirs):
            cum, cex = gates[d]
            c3 = cum.reshape(nc, CH, KEYW)
            ce3 = cex.reshape(nc, CH, KEYW)
            q3 = (qk_ref[:, 0:KEYW] * (DK ** -0.5)).reshape(nc, CH, KEYW)
            k3 = qk_ref[:, KEYW:2 * KEYW].reshape(nc, CH, KEYW)
            if rev:
                a = ce3[:, r0 + SB - 1:r0 + SB, :]
                live = rows3 >= r0
                keep = s_idx >= t_idx + r0
            else:
                a = ce3[:, r0:r0 + 1, :]
                live = rows3 < r0 + SB
                keep = s_idx <= t_idx + r0
            qb = (q3[:, r0:r0 + SB, :] * jnp.exp(c3[:, r0:r0 + SB, :] - a)).astype(BF16)
            kb = (k3 * jnp.exp(jnp.where(live, a - c3, 0.0))).astype(BF16)
            for ci in range(nc):
                for h in range(H):
                    hk = slice(h * DK, (h + 1) * DK)
                    s = lax.dot_general(qb[ci][:, hk], kb[ci][:, hk], _NT_DIMS,
                                        preferred_element_type=F32)
                    scores.setdefault((d, ci, h), []).append(jnp.where(keep, s, 0.0))

    for d, (_, v_ref, _, o_ref, _, _) in enumerate(dirs):
        for ci in range(nc):
            rs = slice(ci * CH, (ci + 1) * CH)
            for h in range(H):
                hv = slice(h * DV, (h + 1) * DV)
                sc = jnp.concatenate(scores[(d, ci, h)], axis=0).astype(BF16)
                o_ref[rs, hv] = _dot(sc, v_ref[rs, hv])

    for d, (qk_ref, v_ref, _, o_ref, st_ref, rev) in enumerate(dirs):
        cum = gates[d][0]
        qe = (qk_ref[:, 0:KEYW] * (DK ** -0.5) * jnp.exp(cum)).astype(BF16)
        for h in range(H):
            hk = slice(h * DK, (h + 1) * DK)
            hv = slice(h * DV, (h + 1) * DV)
            st = st_ref[h]
            for ci in (reversed(range(nc)) if rev else range(nc)):
                rs = slice(ci * CH, (ci + 1) * CH)
                c = cum[rs, hk]
                o_ref[rs, hv] += lax.dot_general(qe[rs, hk], st.astype(BF16), _NT_DIMS,
                                                 preferred_element_type=F32)
                tot = c[0:1, :] if rev else c[CH - 1:CH, :]
                kd = (qk_ref[rs, KEYW + h * DK:KEYW + (h + 1) * DK] * jnp.exp(tot - c)).astype(BF16)
                st = st * jnp.exp(tot) + lax.dot_general(v_ref[rs, hv], kd, _TN_DIMS,
                                                         preferred_element_type=F32)
            st_ref[h] = st


def _bwd_block(j):
    return jnp.where(j == 0, 0, NRB - j)


def _gla(qk, vgp, r, wg, bg):
    qk3 = qk.reshape(B, LT, 2 * KEYW)
    vgp3 = vgp.reshape(B, LT, IN_MAIN - 2 * KEYW)
    r3 = r.reshape(B, LT, LANES)
    fwd = lambda b, j: (b, j, 0)
    bwd = lambda b, j: (b, _bwd_block(j), 0)
    o_f, o_b = pl.pallas_call(
        _gla_kernel,
        grid=(B, NRB),
        in_specs=[
            pl.BlockSpec((None, RB, 2 * KEYW), fwd),
            pl.BlockSpec((None, RB, 2 * KEYW), bwd),
            pl.BlockSpec((None, RB, GLA_W), fwd),
            pl.BlockSpec((None, RB, GLA_W), bwd),
            pl.BlockSpec((None, RB, LANES), fwd),
            pl.BlockSpec((None, RB, LANES), bwd),
            pl.BlockSpec((2, LANES, KEYW), lambda b, j: (0, 0, 0)),
            pl.BlockSpec((2, 1, KEYW), lambda b, j: (0, 0, 0)),
        ],
        out_specs=[
            pl.BlockSpec((None, RB, GLA_W), fwd),
            pl.BlockSpec((None, RB, GLA_W), bwd),
        ],
        out_shape=[jax.ShapeDtypeStruct((B, LT, GLA_W), F32)] * 2,
        scratch_shapes=[pltpu.VMEM((H, DV, DK), F32), pltpu.VMEM((H, DV, DK), F32)],
        compiler_params=_cparams(("parallel", "arbitrary")),
        name="gla",
    )(qk3, qk3, vgp3, vgp3, r3, r3, wg, bg)
    return o_f.reshape(NT, GLA_W), o_b.reshape(NT, GLA_W)


def _pool_tables():
    band = np.zeros((2, PG, TM, TM), np.float32)
    inv = np.zeros((2, PG, TM, PGW), np.float32)
    for kind, row_len in enumerate((GRID_W, CTX)):
        for gi, w in enumerate(WINDOWS):
            for n in range(TM):
                base, j = (n // row_len) * row_len, n % row_len
                lo = min(max(j - w // 2, 0), row_len)
                hi = min(max(j - w // 2 + w, 0), row_len)
                band[kind, gi, n, base + lo:base + hi] = 1.0
                inv[kind, gi, n, :] = 1.0 / (hi - lo)
    return band, inv


def _mix_kernel(with_router, x_ref, of_ref, ob_ref, g_ref, p_ref, mod_ref, n2_ref, gain_ref,
                band_ref, inv_ref, wp_ref, ps_ref, wo_ref, *rest):
    if with_router:
        wr_ref, xo_ref, h2_ref, lg_ref = rest
    else:
        xo_ref, h2_ref = rest
    mod = mod_ref[0]
    o = of_ref[...] + ob_ref[...]
    on = jnp.concatenate([_rms(o[:, h * DV:(h + 1) * DV]) for h in range(H)], axis=1)
    read = (on * gain_ref[...] * _silu(g_ref[...].astype(F32))).astype(BF16)
    y = _dot(read, wo_ref[0:GLA_W, :])
    pouts = []
    for gi in range(PG):
        pg = p_ref[:, gi * PGW:(gi + 1) * PGW]
        mixed = _dot(band_ref[0, gi], pg) * inv_ref[0, gi] - pg.astype(F32)
        pouts.append(_dot(mixed.astype(BF16), wp_ref[gi]))
    pool = (jnp.concatenate(pouts, axis=1) * ps_ref[...]).astype(BF16)
    y = y + _dot(pool, wo_ref[GLA_W:, :])
    xn = x_ref[...] + mod[:, 2 * D:3 * D] * y
    xo_ref[...] = xn
    h2 = _rms(xn) * n2_ref[...] * (1.0 + mod[:, 4 * D:5 * D]) + mod[:, 3 * D:4 * D]
    if with_router:
        for cb in range(SLAB):
            h2_ref[pl.ds(cb, TM, stride=SLAB), :] = h2[:, cb * LANES:(cb + 1) * LANES]
        lg_ref[...] = _dot3(h2, wr_ref[...])
    else:
        h2_ref[...] = h2.astype(BF16)


def _mix(x, o_f, o_b, vgp, mod3, n2, gain, band, inv, wp, ps, wo, l, wr=None):
    with_router = wr is not None
    kind = lambda i: (jnp.where(i % TPB == 0, 1, 0), 0, 0, 0)
    const2 = lambda i: (0, 0)
    in_specs = [
        pl.BlockSpec((TM, D), lambda i: (i, 0)),
        pl.BlockSpec((TM, GLA_W), lambda i: (i, 0)),
        pl.BlockSpec((TM, GLA_W), lambda i: (i, 0)),
        pl.BlockSpec((TM, GLA_W), lambda i: (i, 1)),
        pl.BlockSpec((TM, POOL_W), lambda i: (i, 2)),
        pl.BlockSpec((1, 1, 6 * D), lambda i: (_mod_row(i), 0, 0)),
        pl.BlockSpec((1, D), const2),
        pl.BlockSpec((1, GLA_W), const2),
        pl.BlockSpec((1, PG, TM, TM), kind),
        pl.BlockSpec((1, PG, TM, PGW), kind),
        pl.BlockSpec((PG, PGW, PGW), lambda i: (0, 0, 0)),
        pl.BlockSpec((1, POOL_W), const2),
        pl.BlockSpec((None, D, D), lambda i: (l, 0, 0)),
    ]
    args = [x, o_f, o_b, vgp, vgp, mod3, n2, gain, band, inv, wp, ps, wo]
    if with_router:
        h2_spec = pl.BlockSpec((TM * SLAB, LANES), lambda i: (i, 0))
        h2_shape = jax.ShapeDtypeStruct((NT * SLAB, LANES), F32)
    else:
        h2_spec = pl.BlockSpec((TM, D), lambda i: (i, 0))
        h2_shape = jax.ShapeDtypeStruct((NT, D), BF16)
    out_specs = [pl.BlockSpec((TM, D), lambda i: (i, 0)), h2_spec]
    out_shape = [jax.ShapeDtypeStruct((NT, D), F32), h2_shape]
    if with_router:
        in_specs.append(pl.BlockSpec((D, LANES), const2))
        args.append(wr)
        out_specs.append(pl.BlockSpec((TM, LANES), lambda i: (i, 0)))
        out_shape.append(jax.ShapeDtypeStruct((NT, LANES), F32))
    return pl.pallas_call(
        functools.partial(_mix_kernel, with_router),
        grid=(NTILES,),
        in_specs=in_specs,
        out_specs=out_specs,
        out_shape=out_shape,
        compiler_params=_cparams(("parallel",)),
        name="mix_router" if with_router else "mix",
    )(*args)


def _ffn_kernel(sub, straight, te_ref, ns_ref, blk_ref, xs_ref, w1_ref, w3_ref, w2_ref, o_ref):
    i = pl.program_id(0)
    j = pl.program_id(1)
    nsub = ns_ref[i]

    @pl.when(j == 0)
    def _():
        o_ref[...] = jnp.zeros_like(o_ref)

    def rows_update(rows):
        xt = xs_ref[rows, :]
        h1 = _dot(xt, w1_ref[0, 0].astype(BF16))
        h3 = _dot(xt, w3_ref[0, 0].astype(BF16))
        act = (_silu(h1) * h3).astype(BF16)
        o_ref[rows, :] += _dot(act, w2_ref[0, 0].astype(BF16))

    is_straight = False
    for n in straight:
        is_straight = jnp.logical_or(is_straight, nsub == n)

        @pl.when(nsub == n)
        def _(n=n):
            rows_update(pl.ds(0, n * sub))

    @pl.when(jnp.logical_and(nsub > 0, jnp.logical_not(is_straight)))
    def _():
        def body(s, carry):
            rows_update(pl.ds(pl.multiple_of(s * sub, sub), sub))
            return carry

        lax.fori_loop(0, nsub, body, 0)


def _ffn(xs, te, ns, blk, w1, w3, w2, wl, tm, sub, tf, straight):
    n_tiles = xs.shape[0] // tm
    nj = D_FF // tf
    out_spec = pl.BlockSpec((tm, D), lambda i, j, te, ns, blk: (i, 0))
    out_shape = jax.ShapeDtypeStruct((xs.shape[0], D), F32)
    live_j = lambda i, j, te, ns, blk: jnp.where(ns[i] > 0, j, nj - 1)
    grid_spec = pltpu.PrefetchScalarGridSpec(
        num_scalar_prefetch=3,
        grid=(n_tiles, nj),
        in_specs=[
            pl.BlockSpec((tm, D), lambda i, j, te, ns, blk: (blk[i], 0)),
            pl.BlockSpec((1, 1, D, tf), lambda i, j, te, ns, blk: (wl, te[i], 0, live_j(i, j, te, ns, blk))),
            pl.BlockSpec((1, 1, D, tf), lambda i, j, te, ns, blk: (wl, te[i], 0, live_j(i, j, te, ns, blk))),
            pl.BlockSpec((1, 1, tf, D), lambda i, j, te, ns, blk: (wl, te[i], live_j(i, j, te, ns, blk), 0)),
        ],
        out_specs=out_spec,
    )
    return pl.pallas_call(
        functools.partial(_ffn_kernel, sub, straight),
        grid_spec=grid_spec,
        out_shape=out_shape,
        compiler_params=_cparams(("arbitrary", "arbitrary")),
        name="ffn_%d" % tm,
    )(te, ns, blk, xs, w1, w3, w2)


def _resid_kernel(x_ref, y_ref, mod_ref, o_ref):
    o_ref[...] = x_ref[...] + mod_ref[0][:, 5 * D:6 * D] * y_ref[...]


def _resid(x, y, mod3):
    return pl.pallas_call(
        _resid_kernel,
        grid=(NTILES,),
        in_specs=[
            pl.BlockSpec((TM, D), lambda i: (i, 0)),
            pl.BlockSpec((TM, D), lambda i: (i, 0)),
            pl.BlockSpec((1, 1, 6 * D), lambda i: (_mod_row(i), 0, 0)),
        ],
        out_specs=pl.BlockSpec((TM, D), lambda i: (i, 0)),
        out_shape=jax.ShapeDtypeStruct((NT, D), F32),
        compiler_params=_cparams(("parallel",)),
        name="resid",
    )(x, y, mod3)


def _lat_tile(t):
    return t + t // (TPB - 1) + 1


def _router_kernel(lg_ref, br_ref, route_ref, cnt_ref, carry_ref):
    @pl.when(pl.program_id(0) == 0)
    def _():
        carry_ref[...] = jnp.zeros_like(carry_ref)

    lane = lax.broadcasted_iota(jnp.int32, (TM, LANES), 1)
    z = jnp.where(lane < NE, lg_ref[...] + br_ref[...], -jnp.inf)
    m0 = jnp.max(z, axis=-1, keepdims=True)
    lane_f = lane.astype(F32)
    e0 = jnp.min(jnp.where(z == m0, lane_f, float(LANES)), axis=-1, keepdims=True)
    z1 = jnp.where(lane_f == e0, -jnp.inf, z)
    m1 = jnp.max(z1, axis=-1, keepdims=True)
    e1 = jnp.min(jnp.where(z1 == m1, lane_f, float(LANES)), axis=-1, keepdims=True)
    t = jnp.exp(m1 - m0)
    p0 = 1.0 / (1.0 + t)
    p1 = t / (1.0 + t)
    oh0 = lane_f == e0
    oh1 = lane_f == e1
    oh = jnp.where(jnp.logical_or(oh0, oh1), 1.0, 0.0)
    row = lax.broadcasted_iota(jnp.int32, (TM, TM), 0)
    col = lax.broadcasted_iota(jnp.int32, (TM, TM), 1)
    before = jnp.where(col < row, 1.0, 0.0).astype(BF16)
    excl = _dot(before, oh.astype(BF16)) + carry_ref[0:1, :]
    rank0 = jnp.sum(jnp.where(oh0, excl, 0.0), axis=-1, keepdims=True)
    rank1 = jnp.sum(jnp.where(oh1, excl, 0.0), axis=-1, keepdims=True)
    total = carry_ref[0:1, :] + jnp.sum(oh, axis=0, keepdims=True)
    carry_ref[...] = jnp.broadcast_to(total, carry_ref.shape)
    cnt_ref[...] = jnp.broadcast_to(total, cnt_ref.shape)
    route = jnp.where(lane == 0, e0, 0.0)
    route = jnp.where(lane == 1, e1, route)
    route = jnp.where(lane == 2, rank0, route)
    route = jnp.where(lane == 3, rank1, route)
    route = jnp.where(lane == 4, p0, route)
    route = jnp.where(lane == 5, p1, route)
    route_ref[...] = route


def _router(logits, b_r):
    n_lat_tiles = B * SEQ // TM
    return pl.pallas_call(
        _router_kernel,
        grid=(n_lat_tiles,),
        in_specs=[
            pl.BlockSpec((TM, LANES), lambda t: (_lat_tile(t), 0)),
            pl.BlockSpec((1, LANES), lambda t: (0, 0)),
        ],
        out_specs=[
            pl.BlockSpec((TM, LANES), lambda t: (t, 0)),
            pl.BlockSpec((8, LANES), lambda t: (0, 0)),
        ],
        out_shape=[jax.ShapeDtypeStruct((B * SEQ, LANES), F32),
                   jax.ShapeDtypeStruct((8, LANES), F32)],
        scratch_shapes=[pltpu.VMEM((8, LANES), F32)],
        compiler_params=_cparams(("arbitrary",)),
        name="router",
    )(logits, b_r)


ISSUE_UNROLL = 8


def _slab_copy(src_hbm, tok, dst, r, sem):
    return pltpu.make_async_copy(src_hbm.at[pl.ds(pl.multiple_of(tok * SLAB, SLAB), SLAB), :],
                                 dst.at[pl.ds(r * SLAB_PITCH, SLAB), :], sem)


def _slab_wait_all(src_hbm, dst, sem):
    pltpu.make_async_copy(src_hbm.at[pl.ds(0, TM * SLAB), :], dst.at[pl.ds(0, TM * SLAB), :], sem).wait()


def _gather_kernel(live_ref, idx_ref, h_hbm, o_ref, buf_ref, sem):
    live = live_ref[pl.program_id(0)]

    @pl.when(live == 0)
    def _():
        o_ref[...] = jnp.zeros_like(o_ref)

    @pl.when(live != 0)
    def _():
        def issue(g, carry):
            for u in range(ISSUE_UNROLL):
                r = g * ISSUE_UNROLL + u
                _slab_copy(h_hbm, idx_ref[0, 0, r], buf_ref, r, sem).start(priority=u % 2)
            return carry

        lax.fori_loop(0, TM // ISSUE_UNROLL, issue, 0)
        _slab_wait_all(h_hbm, buf_ref, sem)
        for cb in range(SLAB):
            o_ref[:, cb * LANES:(cb + 1) * LANES] = buf_ref[pl.ds(cb, TM, stride=SLAB_PITCH), :].astype(BF16)


def _gather(h2_slabs, src, live):
    n = src.shape[0] // TM
    grid_spec = pltpu.PrefetchScalarGridSpec(
        num_scalar_prefetch=1,
        grid=(n,),
        in_specs=[
            pl.BlockSpec((1, 1, TM), lambda i, live: (i, 0, 0), memory_space=pltpu.SMEM),
            pl.BlockSpec(memory_space=pl.ANY),
        ],
        out_specs=pl.BlockSpec((TM, D), lambda i, live: (i, 0)),
        scratch_shapes=[pltpu.VMEM((TM * SLAB_PITCH, LANES), F32), pltpu.SemaphoreType.DMA(())],
    )
    return pl.pallas_call(
        _gather_kernel,
        grid_spec=grid_spec,
        out_shape=jax.ShapeDtypeStruct((src.shape[0], D), BF16),
        compiler_params=_cparams(("arbitrary",)),
        name="dispatch",
    )(live, src.reshape(n, 1, TM), h2_slabs)


def _combine_kernel(p0_ref, p1_ref, x_ref, route_ref, mod_ref, fg_ref, y_hbm, o_ref,
                    b0_ref, b1_ref, sem0, sem1):
    def row_copy(row, dst, r, sem):
        return pltpu.make_async_copy(y_hbm.at[pl.ds(row, 1), :], dst.at[pl.ds(r, 1), :], sem)

    def issue(g, carry):
        for u in range(ISSUE_UNROLL):
            r = g * ISSUE_UNROLL + u
            row_copy(p0_ref[0, 0, r], b0_ref, r, sem0).start(priority=0)
            row_copy(p1_ref[0, 0, r], b1_ref, r, sem1).start(priority=1)
        return carry

    lax.fori_loop(0, TM // ISSUE_UNROLL, issue, 0)
    pltpu.make_async_copy(y_hbm.at[pl.ds(0, TM), :], b0_ref, sem0).wait()
    pltpu.make_async_copy(y_hbm.at[pl.ds(0, TM), :], b1_ref, sem1).wait()
    route = route_ref[...]
    y = route[:, 4:5] * b0_ref[...] + route[:, 5:6] * b1_ref[...]
    xn = x_ref[...] + mod_ref[0][:, 5 * D:6 * D] * y
    o_ref[...] = _rms(xn) * fg_ref[...]


def _combine(x, ys, pos0, pos1, route, mod3, fg):
    n = B * SEQ // TM
    smem = lambda: pl.BlockSpec((1, 1, TM), lambda t: (t, 0, 0), memory_space=pltpu.SMEM)
    return pl.pallas_call(
        _combine_kernel,
        grid=(n,),
        in_specs=[
            smem(), smem(),
            pl.BlockSpec((TM, D), lambda t: (_lat_tile(t), 0)),
            pl.BlockSpec((TM, LANES), lambda t: (t, 0)),
            pl.BlockSpec((1, 1, 6 * D), lambda t: (t // (TPB - 1), 0, 0)),
            pl.BlockSpec((1, D), lambda t: (0, 0)),
            pl.BlockSpec(memory_space=pl.ANY),
        ],
        out_specs=pl.BlockSpec((TM, D), lambda t: (t, 0)),
        out_shape=jax.ShapeDtypeStruct((B * SEQ, D), F32),
        scratch_shapes=[pltpu.VMEM((TM, D), F32), pltpu.VMEM((TM, D), F32),
                        pltpu.SemaphoreType.DMA(()), pltpu.SemaphoreType.DMA(())],
        compiler_params=_cparams(("arbitrary",)),
        name="combine",
    )(pos0.reshape(n, 1, TM), pos1.reshape(n, 1, TM), x, route, mod3, fg, ys)


def _moe_plan(route, counts):
    e = route[:, 0:2].astype(jnp.int32)
    rank = route[:, 2:4].astype(jnp.int32)
    cnt = counts[0, :NE].astype(jnp.int32)
    nsub = (cnt + SUB_MOE - 1) // SUB_MOE
    ntile = (nsub + SUBS_MOE - 1) // SUBS_MOE
    base = nsub // jnp.maximum(ntile, 1)
    rem = nsub - base * ntile
    tile_end = jnp.cumsum(ntile)
    tile_start = tile_end - ntile
    sub_r = rank // SUB_MOE
    big = base[e] + 1
    cut = rem[e] * big
    small = jnp.maximum(base[e], 1)
    in_big = sub_r < cut
    t_loc = jnp.where(in_big, sub_r // big, rem[e] + (sub_r - cut) // small)
    s_loc = jnp.where(in_big, sub_r % big, (sub_r - cut) % small)
    pos = (tile_start[e] + t_loc) * TM_MOE + s_loc * SUB_MOE + rank % SUB_MOE
    n_used = tile_end[NE - 1]
    tiles = jnp.arange(NT_MOE, dtype=jnp.int32)
    blk = jnp.minimum(tiles, n_used - 1)
    te = jnp.sum((blk[:, None] >= tile_end[None, :NE - 1]).astype(jnp.int32), axis=1)
    ns = base[te] + ((blk - tile_start[te]) < rem[te]).astype(jnp.int32)
    ns = jnp.where(tiles < n_used, ns, 0).astype(jnp.int32)
    tok = jnp.arange(B * SEQ, dtype=jnp.int32)
    tok_row = (tok // SEQ) * LT + CTX + tok % SEQ
    src = jnp.full((R_MOE,), CTX, jnp.int32).at[pos.reshape(-1)].set(jnp.repeat(tok_row, 2))
    subs = jnp.arange(TM_MOE // SUB_MOE, dtype=jnp.int32)
    live = (subs[None, :] < ns[:, None]).astype(jnp.int32).reshape(-1)
    return pos[:, 0], pos[:, 1], te, ns, blk, src, live


def kernel(x, c, ctx, c_ctx, w_ada, b_ada, norm1_g, norm2_g, w_in, w_gate, b_gate, gla_norm_g,
           w_pool, pool_scale, w_out, ffn_w1, ffn_w3, ffn_w2, moe_w_router, moe_b_router,
           moe_w1, moe_w3, moe_w2, final_g):
    assert x.shape == (B, SEQ, D) and ctx.shape == (B, CTX, D) and DEPTH == 2

    cvec = jnp.concatenate([c, c_ctx[None, :], jnp.zeros((8 - B - 1, D), F32)], axis=0)
    mods = _ada(cvec, w_ada, b_ada)
    xs = jnp.concatenate([ctx, x], axis=1).reshape(NT, D)

    w_in_b = w_in.astype(BF16)
    w_r_b = jnp.pad(w_in[:, :, IN_MAIN:], ((0, 0), (0, 0), (0, LANES - 2 * RANK))).astype(BF16)
    w_out_b = w_out.astype(BF16)
    w_pool_b = w_pool.astype(BF16)

    band_np, inv_np = _pool_tables()
    band = jnp.asarray(band_np, BF16)
    inv = jnp.asarray(inv_np, F32)

    dense_te = jnp.zeros((NT // TM_DENSE,), jnp.int32)
    dense_ns = jnp.full((NT // TM_DENSE,), TM_DENSE // SUB_DENSE, jnp.int32)
    dense_blk = jnp.arange(NT // TM_DENSE, dtype=jnp.int32)

    out = None
    for l in range(DEPTH):
        mod3 = mods[l, :3].reshape(3, 1, 6 * D)
        wg = jnp.zeros((2, LANES, KEYW), F32)
        wg = wg.at[0, 0:RANK].set(w_gate[l, 0]).at[1, RANK:2 * RANK].set(w_gate[l, 1]).astype(BF16)
        bg = b_gate[l].reshape(2, 1, KEYW)

        qk, vgp, r = _inproj(xs, mod3, norm1_g[l].reshape(1, D), w_in_b, w_r_b, l)
        o_f, o_b = _gla(qk, vgp, r, wg, bg)
        mix_args = (xs, o_f, o_b, vgp, mod3, norm2_g[l].reshape(1, D), gla_norm_g[l].reshape(1, GLA_W),
                    band, inv, w_pool_b[l], pool_scale[l].reshape(1, POOL_W), w_out_b, l)
        if l % 2 == 0:
            x1, h2 = _mix(*mix_args)
            jl = l // 2
            ys = _ffn(h2, dense_te, dense_ns, dense_blk, ffn_w1[:, None], ffn_w3[:, None],
                      ffn_w2[:, None], jl, TM_DENSE, SUB_DENSE, TF_DENSE,
                      straight=(TM_DENSE // SUB_DENSE,))
            xs = _resid(x1, ys, mod3)
        else:
            jl = l // 2
            wr = jnp.pad(moe_w_router[jl], ((0, 0), (0, LANES - NE)))
            br = jnp.pad(moe_b_router[jl], (0, LANES - NE)).reshape(1, LANES)
            x1, h2, logits = _mix(*mix_args, wr=wr)
            route, counts = _router(logits, br)
            pos0, pos1, te, ns, blk, src, live = _moe_plan(route, counts)
            xg = _gather(h2, src, live)
            ys = _ffn(xg, te, ns, blk, moe_w1, moe_w3, moe_w2, jl, TM_MOE, SUB_MOE, TF_MOE,
                      straight=(SUBS_MOE, SUBS_MOE - 1))
            out = _combine(x1, ys, pos0, pos1, route, mod3, final_g.reshape(1, D))
    return out.reshape(B, SEQ, D)
```

```python
import functools

import numpy as np
import jax
import jax.numpy as jnp
from jax import lax
from jax.experimental import pallas as pl
from jax.experimental.pallas import tpu as pltpu

F32 = jnp.float32
BF16 = jnp.bfloat16

D = 2048
B = 2
SEQ = 4096
CTX = 256
DEPTH = 2
GRID_W = 64
EPS = 1e-6
GLA_W = 1024
POOL_W = 1024
H = 4
DV = 256
DK = 128
KEYW = H * DK
RANK = 16
TAU = 16.0
WINDOWS = (2, 4, 8, 16)
PG = 4
PGW = 256
D_FF = 5632
NE = 8

LT = CTX + SEQ
NT = B * LT
TM = 256
TPB = LT // TM
NTILES = NT // TM
LANES = 128
SLAB = D // LANES
SLAB_PITCH = SLAB + 8

CH = 64
SB = 16
RB = 256
NRB = LT // RB
assert RB == CTX and RB % CH == 0

TF_DENSE = 512
TF_MOE = 256
TM_DENSE = NT // 8
SUB_DENSE = TM_DENSE // 4
SUB_MOE = 256
SUBS_MOE = 5
TM_MOE = SUBS_MOE * SUB_MOE
N_ASSIGN = B * SEQ * 2
NT_MOE = (N_ASSIGN // SUB_MOE + NE + (SUBS_MOE - 1) * NE) // SUBS_MOE
R_MOE = NT_MOE * TM_MOE
assert SUB_MOE == TM

VMEM_LIMIT = 56 * 1024 * 1024


def _cparams(sem, vmem=VMEM_LIMIT):
    return pltpu.CompilerParams(dimension_semantics=sem, vmem_limit_bytes=vmem)


def _split2(a):
    hi = a.astype(BF16)
    lo = (a - hi.astype(F32)).astype(BF16)
    return hi, lo


def _dot(a, b):
    return jnp.dot(a, b, preferred_element_type=F32)


def _dot3(a, b):
    ah, al = _split2(a)
    bh, bl = _split2(b)
    return _dot(ah, bh) + (_dot(al, bh) + _dot(ah, bl))


def _silu(a):
    return a / (1.0 + jnp.exp(-a))


def _rms(x):
    return x * lax.rsqrt(jnp.mean(x * x, axis=-1, keepdims=True) + EPS)


def _mod_row(i):
    return jnp.where(i % TPB == 0, 2, i // TPB)


ADA_TN = 1024


def _ada_kernel(c_ref, w_ref, b_ref, o_ref):
    s = _silu(c_ref[...])
    o_ref[0] = _dot3(s, w_ref[0]) + b_ref[0]


def _ada(cvec, w_ada, b_ada):
    n = 6 * D
    return pl.pallas_call(
        _ada_kernel,
        grid=(DEPTH, n // ADA_TN),
        in_specs=[
            pl.BlockSpec((8, D), lambda l, j: (0, 0)),
            pl.BlockSpec((1, D, ADA_TN), lambda l, j: (l, 0, j)),
            pl.BlockSpec((1, 1, ADA_TN), lambda l, j: (l, 0, j)),
        ],
        out_specs=pl.BlockSpec((1, 8, ADA_TN), lambda l, j: (l, 0, j)),
        out_shape=jax.ShapeDtypeStruct((DEPTH, 8, n), F32),
        compiler_params=_cparams(("parallel", "parallel")),
        name="ada",
    )(cvec, w_ada, b_ada.reshape(DEPTH, 1, n))


IN_MAIN = 2 * KEYW + 2 * GLA_W + POOL_W


def _x_specs(xs):
    if len(xs) == 1:
        return [pl.BlockSpec((TM, D), lambda i: (i, 0))]
    return [pl.BlockSpec((None, TM, D), lambda i: (i // TPB, jnp.maximum(i % TPB - 1, 0), 0)),
            pl.BlockSpec((None, TM, D), lambda i: (i // TPB, 0, 0))]


def _x_tile(x_refs):
    if len(x_refs) == 1:
        return x_refs[0][...]
    return jnp.where(pl.program_id(0) % TPB == 0, x_refs[1][...], x_refs[0][...])


def _inproj_kernel(n_x, *refs):
    x_refs = refs[:n_x]
    mod_ref, g_ref, w_ref, wr_ref, qk_ref, vgp_ref, r_ref = refs[n_x:]
    mod = mod_ref[0]
    h = (_rms(_x_tile(x_refs)) * g_ref[...] * (1.0 + mod[:, D:2 * D]) + mod[:, 0:D]).astype(BF16)
    u = _dot(h, w_ref[...])
    qk_ref[...] = u[:, :2 * KEYW]
    vgp_ref[...] = u[:, 2 * KEYW:].astype(BF16)
    r_ref[...] = _dot(h, wr_ref[...])


def _inproj(xs, mod3, g, w, wr, l):
    return pl.pallas_call(
        functools.partial(_inproj_kernel, len(xs)),
        grid=(NTILES,),
        in_specs=_x_specs(xs) + [
            pl.BlockSpec((1, 1, 6 * D), lambda i: (_mod_row(i), 0, 0)),
            pl.BlockSpec((1, D), lambda i: (0, 0)),
            pl.BlockSpec((None, D, IN_MAIN), lambda i: (l, 0, 0)),
            pl.BlockSpec((None, D, LANES), lambda i: (l, 0, 0)),
        ],
        out_specs=[
            pl.BlockSpec((TM, 2 * KEYW), lambda i: (i, 0)),
            pl.BlockSpec((TM, IN_MAIN - 2 * KEYW), lambda i: (i, 0)),
            pl.BlockSpec((TM, LANES), lambda i: (i, 0)),
        ],
        out_shape=[
            jax.ShapeDtypeStruct((NT, 2 * KEYW), F32),
            jax.ShapeDtypeStruct((NT, IN_MAIN - 2 * KEYW), BF16),
            jax.ShapeDtypeStruct((NT, LANES), F32),
        ],
        compiler_params=_cparams(("parallel",)),
        name="inproj",
    )(*xs, mod3, g, w, wr)


_NT_DIMS = (((1,), (1,)), ((), ()))
_TN_DIMS = (((0,), (0,)), ((), ()))


def _gla_gates(r_ref, wg_ref, bg_ref, rev):
    d = 1 if rev else 0
    pre = _dot(r_ref[...].astype(BF16), wg_ref[d]) + bg_ref[d]
    lg = (jnp.minimum(pre, 0.0) - jnp.log(1.0 + jnp.exp(-jnp.abs(pre)))) * (1.0 / TAU)
    row = lax.broadcasted_iota(jnp.int32, (RB, RB), 0)
    col = lax.broadcasted_iota(jnp.int32, (RB, RB), 1)
    inside = jnp.logical_and(row // CH == col // CH, (col >= row) if rev else (col <= row))
    tri = jnp.where(inside, 1.0, 0.0).astype(BF16)
    lg_hi, lg_lo = _split2(lg)
    cum = _dot(tri, lg_hi) + _dot(tri, lg_lo)
    return cum, cum - lg


def _gla_kernel(qkf_ref, qkb_ref, vf_ref, vb_ref, rf_ref, rb_ref, wg_ref, bg_ref,
                of_ref, ob_ref, stf_ref, stb_ref):
    @pl.when(pl.program_id(1) == 0)
    def _():
        stf_ref[...] = jnp.zeros_like(stf_ref)
        stb_ref[...] = jnp.zeros_like(stb_ref)

    nc = RB // CH
    dirs = ((qkf_ref, vf_ref, rf_ref, of_ref, stf_ref, False),
            (qkb_ref, vb_ref, rb_ref, ob_ref, stb_ref, True))
    rows3 = lax.broadcasted_iota(jnp.int32, (nc, CH, KEYW), 1)
    t_idx = lax.broadcasted_iota(jnp.int32, (SB, CH), 0)
    s_idx = lax.broadcasted_iota(jnp.int32, (SB, CH), 1)

    gates = [_gla_gates(r_ref, wg_ref, bg_ref, rev) for (_, _, r_ref, _, _, rev) in dirs]

    scores = {}
    for blk in range(CH // SB):
        r0 = blk * SB
        for d, (qk_ref, _, _, _, _, rev) in enumerate(dirs):
            cum, cex = gates[d]
            c3 = cum.reshape(nc, CH, KEYW)
            ce3 = cex.reshape(nc, CH, KEYW)
            q3 = (qk_ref[:, 0:KEYW] * (DK ** -0.5)).reshape(nc, CH, KEYW)
            k3 = qk_ref[:, KEYW:2 * KEYW].reshape(nc, CH, KEYW)
            if rev:
                a = ce3[:, r0 + SB - 1:r0 + SB, :]
                live = rows3 >= r0
                keep = s_idx >= t_idx + r0
            else:
                a = ce3[:, r0:r0 + 1, :]
                live = rows3 < r0 + SB
                keep = s_idx <= t_idx + r0
            qb = (q3[:, r0:r0 + SB, :] * jnp.exp(c3[:, r0:r0 + SB, :] - a)).astype(BF16)
            kb = (k3 * jnp.exp(jnp.where(live, a - c3, 0.0))).astype(BF16)
            for ci in range(nc):
                for h in range(H):
                    hk = slice(h * DK, (h + 1) * DK)
                    s = lax.dot_general(qb[ci][:, hk], kb[ci][:, hk], _NT_DIMS,
                                        preferred_element_type=F32)
                    scores.setdefault((d, ci, h), []).append(jnp.where(keep, s, 0.0))

    for d, (_, v_ref, _, o_ref, _, _) in enumerate(dirs):
        for ci in range(nc):
            rs = slice(ci * CH, (ci + 1) * CH)
            for h in range(H):
                hv = slice(h * DV, (h + 1) * DV)
                sc = jnp.concatenate(scores[(d, ci, h)], axis=0).astype(BF16)
                o_ref[rs, hv] = _dot(sc, v_ref[rs, hv])

    for d, (qk_ref, v_ref, _, o_ref, st_ref, rev) in enumerate(dirs):
        cum = gates[d][0]
        qe = (qk_ref[:, 0:KEYW] * (DK ** -0.5) * jnp.exp(cum)).astype(BF16)
        for h in range(H):
            hk = slice(h * DK, (h + 1) * DK)
            hv = slice(h * DV, (h + 1) * DV)
            st = st_ref[h]
            for ci in (reversed(range(nc)) if rev else range(nc)):
                rs = slice(ci * CH, (ci + 1) * CH)
                c = cum[rs, hk]
                o_ref[rs, hv] += lax.dot_general(qe[rs, hk], st.astype(BF16), _NT_DIMS,
                                                 preferred_element_type=F32)
                tot = c[0:1, :] if rev else c[CH - 1:CH, :]
                kd = (qk_ref[rs, KEYW + h * DK:KEYW + (h + 1) * DK] * jnp.exp(tot - c)).astype(BF16)
                st = st * jnp.exp(tot) + lax.dot_general(v_ref[rs, hv], kd, _TN_DIMS,
                                                         preferred_element_type=F32)
            st_ref[h] = st


def _bwd_block(j):
    return jnp.where(j == 0, 0, NRB - j)


def _gla(qk, vgp, r, wg, bg):
    qk3 = qk.reshape(B, LT, 2 * KEYW)
    vgp3 = vgp.reshape(B, LT, IN_MAIN - 2 * KEYW)
    r3 = r.reshape(B, LT, LANES)
    fwd = lambda b, j: (b, j, 0)
    bwd = lambda b, j: (b, _bwd_block(j), 0)
    o_f, o_b = pl.pallas_call(
        _gla_kernel,
        grid=(B, NRB),
        in_specs=[
            pl.BlockSpec((None, RB, 2 * KEYW), fwd),
            pl.BlockSpec((None, RB, 2 * KEYW), bwd),
            pl.BlockSpec((None, RB, GLA_W), fwd),
            pl.BlockSpec((None, RB, GLA_W), bwd),
            pl.BlockSpec((None, RB, LANES), fwd),
            pl.BlockSpec((None, RB, LANES), bwd),
            pl.BlockSpec((2, LANES, KEYW), lambda b, j: (0, 0, 0)),
            pl.BlockSpec((2, 1, KEYW), lambda b, j: (0, 0, 0)),
        ],
        out_specs=[
            pl.BlockSpec((None, RB, GLA_W), fwd),
            pl.BlockSpec((None, RB, GLA_W), bwd),
        ],
        out_shape=[jax.ShapeDtypeStruct((B, LT, GLA_W), F32)] * 2,
        scratch_shapes=[pltpu.VMEM((H, DV, DK), F32), pltpu.VMEM((H, DV, DK), F32)],
        compiler_params=_cparams(("parallel", "arbitrary")),
        name="gla",
    )(qk3, qk3, vgp3, vgp3, r3, r3, wg, bg)
    return o_f.reshape(NT, GLA_W), o_b.reshape(NT, GLA_W)


def _pool_tables():
    band = np.zeros((2, PG, TM, TM), np.float32)
    inv = np.zeros((2, PG, TM, PGW), np.float32)
    for kind, row_len in enumerate((GRID_W, CTX)):
        for gi, w in enumerate(WINDOWS):
            for n in range(TM):
                base, j = (n // row_len) * row_len, n % row_len
                lo = min(max(j - w // 2, 0), row_len)
                hi = min(max(j - w // 2 + w, 0), row_len)
                band[kind, gi, n, base + lo:base + hi] = 1.0
                inv[kind, gi, n, :] = 1.0 / (hi - lo)
    return band, inv


def _mix_kernel(with_router, n_x, *refs):
    x_refs = refs[:n_x]
    (of_ref, ob_ref, g_ref, p_ref, mod_ref, n2_ref, gain_ref,
     band_ref, inv_ref, wp_ref, ps_ref, wo_ref) = refs[n_x:n_x + 12]
    rest = refs[n_x + 12:]
    if with_router:
        wr_ref, xo_ref, h2_ref, lg_ref = rest
    else:
        xo_ref, h2_ref = rest
    mod = mod_ref[0]
    o = of_ref[...] + ob_ref[...]
    on = jnp.concatenate([_rms(o[:, h * DV:(h + 1) * DV]) for h in range(H)], axis=1)
    read = (on * gain_ref[...] * _silu(g_ref[...].astype(F32))).astype(BF16)
    y = _dot(read, wo_ref[0:GLA_W, :])
    pouts = []
    for gi in range(PG):
        pg = p_ref[:, gi * PGW:(gi + 1) * PGW]
        mixed = _dot(band_ref[0, gi], pg) * inv_ref[0, gi] - pg.astype(F32)
        pouts.append(_dot(mixed.astype(BF16), wp_ref[gi]))
    pool = (jnp.concatenate(pouts, axis=1) * ps_ref[...]).astype(BF16)
    y = y + _dot(pool, wo_ref[GLA_W:, :])
    xn = _x_tile(x_refs) + mod[:, 2 * D:3 * D] * y
    xo_ref[...] = xn
    h2 = _rms(xn) * n2_ref[...] * (1.0 + mod[:, 4 * D:5 * D]) + mod[:, 3 * D:4 * D]
    if with_router:
        for cb in range(SLAB):
            h2_ref[pl.ds(cb, TM, stride=SLAB), :] = h2[:, cb * LANES:(cb + 1) * LANES]
        lg_ref[...] = _dot3(h2, wr_ref[...])
    else:
        h2_ref[...] = h2.astype(BF16)


def _mix(xs, o_f, o_b, vgp, mod3, n2, gain, band, inv, wp, ps, wo, l, wr=None):
    with_router = wr is not None
    kind = lambda i: (jnp.where(i % TPB == 0, 1, 0), 0, 0, 0)
    const2 = lambda i: (0, 0)
    in_specs = _x_specs(xs) + [
        pl.BlockSpec((TM, GLA_W), lambda i: (i, 0)),
        pl.BlockSpec((TM, GLA_W), lambda i: (i, 0)),
        pl.BlockSpec((TM, GLA_W), lambda i: (i, 1)),
        pl.BlockSpec((TM, POOL_W), lambda i: (i, 2)),
        pl.BlockSpec((1, 1, 6 * D), lambda i: (_mod_row(i), 0, 0)),
        pl.BlockSpec((1, D), const2),
        pl.BlockSpec((1, GLA_W), const2),
        pl.BlockSpec((1, PG, TM, TM), kind),
        pl.BlockSpec((1, PG, TM, PGW), kind),
        pl.BlockSpec((PG, PGW, PGW), lambda i: (0, 0, 0)),
        pl.BlockSpec((1, POOL_W), const2),
        pl.BlockSpec((None, D, D), lambda i: (l, 0, 0)),
    ]
    args = [*xs, o_f, o_b, vgp, vgp, mod3, n2, gain, band, inv, wp, ps, wo]
    if with_router:
        h2_spec = pl.BlockSpec((TM * SLAB, LANES), lambda i: (i, 0))
        h2_shape = jax.ShapeDtypeStruct((NT * SLAB, LANES), F32)
    else:
        h2_spec = pl.BlockSpec((TM, D), lambda i: (i, 0))
        h2_shape = jax.ShapeDtypeStruct((NT, D), BF16)
    out_specs = [pl.BlockSpec((TM, D), lambda i: (i, 0)), h2_spec]
    out_shape = [jax.ShapeDtypeStruct((NT, D), F32), h2_shape]
    if with_router:
        in_specs.append(pl.BlockSpec((D, LANES), const2))
        args.append(wr)
        out_specs.append(pl.BlockSpec((TM, LANES), lambda i: (i, 0)))
        out_shape.append(jax.ShapeDtypeStruct((NT, LANES), F32))
    return pl.pallas_call(
        functools.partial(_mix_kernel, with_router, len(xs)),
        grid=(NTILES,),
        in_specs=in_specs,
        out_specs=out_specs,
        out_shape=out_shape,
        compiler_params=_cparams(("parallel",)),
        name="mix_router" if with_router else "mix",
    )(*args)


def _ffn_kernel(sub, straight, te_ref, ns_ref, blk_ref, xs_ref, w1_ref, w3_ref, w2_ref, o_ref):
    i = pl.program_id(0)
    j = pl.program_id(1)
    nsub = ns_ref[i]

    @pl.when(j == 0)
    def _():
        o_ref[...] = jnp.zeros_like(o_ref)

    def rows_update(rows):
        xt = xs_ref[rows, :]
        h1 = _dot(xt, w1_ref[0, 0].astype(BF16))
        h3 = _dot(xt, w3_ref[0, 0].astype(BF16))
        act = (_silu(h1) * h3).astype(BF16)
        o_ref[rows, :] += _dot(act, w2_ref[0, 0].astype(BF16))

    is_straight = False
    for n in straight:
        is_straight = jnp.logical_or(is_straight, nsub == n)

        @pl.when(nsub == n)
        def _(n=n):
            rows_update(pl.ds(0, n * sub))

    @pl.when(jnp.logical_and(nsub > 0, jnp.logical_not(is_straight)))
    def _():
        def body(s, carry):
            rows_update(pl.ds(pl.multiple_of(s * sub, sub), sub))
            return carry

        lax.fori_loop(0, nsub, body, 0)


def _ffn(xs, te, ns, blk, w1, w3, w2, wl, tm, sub, tf, straight):
    n_tiles = xs.shape[0] // tm
    nj = D_FF // tf
    out_spec = pl.BlockSpec((tm, D), lambda i, j, te, ns, blk: (i, 0))
    out_shape = jax.ShapeDtypeStruct((xs.shape[0], D), F32)
    live_j = lambda i, j, te, ns, blk: jnp.where(ns[i] > 0, j, nj - 1)
    grid_spec = pltpu.PrefetchScalarGridSpec(
        num_scalar_prefetch=3,
        grid=(n_tiles, nj),
        in_specs=[
            pl.BlockSpec((tm, D), lambda i, j, te, ns, blk: (blk[i], 0)),
            pl.BlockSpec((1, 1, D, tf), lambda i, j, te, ns, blk: (wl, te[i], 0, live_j(i, j, te, ns, blk))),
            pl.BlockSpec((1, 1, D, tf), lambda i, j, te, ns, blk: (wl, te[i], 0, live_j(i, j, te, ns, blk))),
            pl.BlockSpec((1, 1, tf, D), lambda i, j, te, ns, blk: (wl, te[i], live_j(i, j, te, ns, blk), 0)),
        ],
        out_specs=out_spec,
    )
    return pl.pallas_call(
        functools.partial(_ffn_kernel, sub, straight),
        grid_spec=grid_spec,
        out_shape=out_shape,
        compiler_params=_cparams(("arbitrary", "arbitrary")),
        name="ffn_%d" % tm,
    )(te, ns, blk, xs, w1, w3, w2)


def _resid_kernel(x_ref, y_ref, mod_ref, o_ref):
    o_ref[...] = x_ref[...] + mod_ref[0][:, 5 * D:6 * D] * y_ref[...]


def _resid(x, y, mod3):
    return pl.pallas_call(
        _resid_kernel,
        grid=(NTILES,),
        in_specs=[
            pl.BlockSpec((TM, D), lambda i: (i, 0)),
            pl.BlockSpec((TM, D), lambda i: (i, 0)),
            pl.BlockSpec((1, 1, 6 * D), lambda i: (_mod_row(i), 0, 0)),
        ],
        out_specs=pl.BlockSpec((TM, D), lambda i: (i, 0)),
        out_shape=jax.ShapeDtypeStruct((NT, D), F32),
        compiler_params=_cparams(("parallel",)),
        name="resid",
    )(x, y, mod3)


def _lat_tile(t):
    return t + t // (TPB - 1) + 1


def _router_kernel(lg_ref, br_ref, route_ref, cnt_ref, carry_ref):
    @pl.when(pl.program_id(0) == 0)
    def _():
        carry_ref[...] = jnp.zeros_like(carry_ref)

    lane = lax.broadcasted_iota(jnp.int32, (TM, LANES), 1)
    z = jnp.where(lane < NE, lg_ref[...] + br_ref[...], -jnp.inf)
    m0 = jnp.max(z, axis=-1, keepdims=True)
    lane_f = lane.astype(F32)
    e0 = jnp.min(jnp.where(z == m0, lane_f, float(LANES)), axis=-1, keepdims=True)
    z1 = jnp.where(lane_f == e0, -jnp.inf, z)
    m1 = jnp.max(z1, axis=-1, keepdims=True)
    e1 = jnp.min(jnp.where(z1 == m1, lane_f, float(LANES)), axis=-1, keepdims=True)
    t = jnp.exp(m1 - m0)
    p0 = 1.0 / (1.0 + t)
    p1 = t / (1.0 + t)
    oh0 = lane_f == e0
    oh1 = lane_f == e1
    oh = jnp.where(jnp.logical_or(oh0, oh1), 1.0, 0.0)
    row = lax.broadcasted_iota(jnp.int32, (TM, TM), 0)
    col = lax.broadcasted_iota(jnp.int32, (TM, TM), 1)
    before = jnp.where(col < row, 1.0, 0.0).astype(BF16)
    excl = _dot(before, oh.astype(BF16)) + carry_ref[0:1, :]
    rank0 = jnp.sum(jnp.where(oh0, excl, 0.0), axis=-1, keepdims=True)
    rank1 = jnp.sum(jnp.where(oh1, excl, 0.0), axis=-1, keepdims=True)
    total = carry_ref[0:1, :] + jnp.sum(oh, axis=0, keepdims=True)
    carry_ref[...] = jnp.broadcast_to(total, carry_ref.shape)
    cnt_ref[...] = jnp.broadcast_to(total, cnt_ref.shape)
    route = jnp.where(lane == 0, e0, 0.0)
    route = jnp.where(lane == 1, e1, route)
    route = jnp.where(lane == 2, rank0, route)
    route = jnp.where(lane == 3, rank1, route)
    route = jnp.where(lane == 4, p0, route)
    route = jnp.where(lane == 5, p1, route)
    route_ref[...] = route


def _router(logits, b_r):
    n_lat_tiles = B * SEQ // TM
    return pl.pallas_call(
        _router_kernel,
        grid=(n_lat_tiles,),
        in_specs=[
            pl.BlockSpec((TM, LANES), lambda t: (_lat_tile(t), 0)),
            pl.BlockSpec((1, LANES), lambda t: (0, 0)),
        ],
        out_specs=[
            pl.BlockSpec((TM, LANES), lambda t: (t, 0)),
            pl.BlockSpec((8, LANES), lambda t: (0, 0)),
        ],
        out_shape=[jax.ShapeDtypeStruct((B * SEQ, LANES), F32),
                   jax.ShapeDtypeStruct((8, LANES), F32)],
        scratch_shapes=[pltpu.VMEM((8, LANES), F32)],
        compiler_params=_cparams(("arbitrary",)),
        name="router",
    )(logits, b_r)


ISSUE_UNROLL = 8


def _slab_copy(src_hbm, tok, dst, r, sem):
    return pltpu.make_async_copy(src_hbm.at[pl.ds(pl.multiple_of(tok * SLAB, SLAB), SLAB), :],
                                 dst.at[pl.ds(r * SLAB_PITCH, SLAB), :], sem)


def _slab_wait_all(src_hbm, dst, sem):
    pltpu.make_async_copy(src_hbm.at[pl.ds(0, TM * SLAB), :], dst.at[pl.ds(0, TM * SLAB), :], sem).wait()


def _gather_kernel(live_ref, idx_ref, h_hbm, o_ref, buf_ref, sem):
    live = live_ref[pl.program_id(0)]

    @pl.when(live == 0)
    def _():
        o_ref[...] = jnp.zeros_like(o_ref)

    @pl.when(live != 0)
    def _():
        def issue(g, carry):
            for u in range(ISSUE_UNROLL):
                r = g * ISSUE_UNROLL + u
                _slab_copy(h_hbm, idx_ref[0, 0, r], buf_ref, r, sem).start(priority=u % 2)
            return carry

        lax.fori_loop(0, TM // ISSUE_UNROLL, issue, 0)
        _slab_wait_all(h_hbm, buf_ref, sem)
        for cb in range(SLAB):
            o_ref[:, cb * LANES:(cb + 1) * LANES] = buf_ref[pl.ds(cb, TM, stride=SLAB_PITCH), :].astype(BF16)


def _gather(h2_slabs, src, live):
    n = src.shape[0] // TM
    grid_spec = pltpu.PrefetchScalarGridSpec(
        num_scalar_prefetch=1,
        grid=(n,),
        in_specs=[
            pl.BlockSpec((1, 1, TM), lambda i, live: (i, 0, 0), memory_space=pltpu.SMEM),
            pl.BlockSpec(memory_space=pl.ANY),
        ],
        out_specs=pl.BlockSpec((TM, D), lambda i, live: (i, 0)),
        scratch_shapes=[pltpu.VMEM((TM * SLAB_PITCH, LANES), F32), pltpu.SemaphoreType.DMA(())],
    )
    return pl.pallas_call(
        _gather_kernel,
        grid_spec=grid_spec,
        out_shape=jax.ShapeDtypeStruct((src.shape[0], D), BF16),
        compiler_params=_cparams(("arbitrary",)),
        name="dispatch",
    )(live, src.reshape(n, 1, TM), h2_slabs)


def _combine_kernel(p0_ref, p1_ref, x_ref, route_ref, mod_ref, fg_ref, y_hbm, o_ref,
                    b0_ref, b1_ref, sem0, sem1):
    def row_copy(row, dst, r, sem):
        return pltpu.make_async_copy(y_hbm.at[pl.ds(row, 1), :], dst.at[pl.ds(r, 1), :], sem)

    def issue(g, carry):
        for u in range(ISSUE_UNROLL):
            r = g * ISSUE_UNROLL + u
            row_copy(p0_ref[0, 0, r], b0_ref, r, sem0).start(priority=0)
            row_copy(p1_ref[0, 0, r], b1_ref, r, sem1).start(priority=1)
        return carry

    lax.fori_loop(0, TM // ISSUE_UNROLL, issue, 0)
    pltpu.make_async_copy(y_hbm.at[pl.ds(0, TM), :], b0_ref, sem0).wait()
    pltpu.make_async_copy(y_hbm.at[pl.ds(0, TM), :], b1_ref, sem1).wait()
    route = route_ref[...]
    y = route[:, 4:5] * b0_ref[...] + route[:, 5:6] * b1_ref[...]
    xn = x_ref[...] + mod_ref[0][:, 5 * D:6 * D] * y
    o_ref[...] = _rms(xn) * fg_ref[...]


def _combine(x, ys, pos0, pos1, route, mod3, fg):
    n = B * SEQ // TM
    smem = lambda: pl.BlockSpec((1, 1, TM), lambda t: (t, 0, 0), memory_space=pltpu.SMEM)
    return pl.pallas_call(
        _combine_kernel,
        grid=(n,),
        in_specs=[
            smem(), smem(),
            pl.BlockSpec((TM, D), lambda t: (_lat_tile(t), 0)),
            pl.BlockSpec((TM, LANES), lambda t: (t, 0)),
            pl.BlockSpec((1, 1, 6 * D), lambda t: (t // (TPB - 1), 0, 0)),
            pl.BlockSpec((1, D), lambda t: (0, 0)),
            pl.BlockSpec(memory_space=pl.ANY),
        ],
        out_specs=pl.BlockSpec((TM, D), lambda t: (t, 0)),
        out_shape=jax.ShapeDtypeStruct((B * SEQ, D), F32),
        scratch_shapes=[pltpu.VMEM((TM, D), F32), pltpu.VMEM((TM, D), F32),
                        pltpu.SemaphoreType.DMA(()), pltpu.SemaphoreType.DMA(())],
        compiler_params=_cparams(("arbitrary",)),
        name="combine",
    )(pos0.reshape(n, 1, TM), pos1.reshape(n, 1, TM), x, route, mod3, fg, ys)


def _moe_plan(route, counts):
    cnt = counts[0, :NE].astype(jnp.int32)
    nsub = (cnt + SUB_MOE - 1) // SUB_MOE
    ntile = (nsub + SUBS_MOE - 1) // SUBS_MOE
    base = nsub // jnp.maximum(ntile, 1)
    rem = nsub - base * ntile
    tile_end = jnp.cumsum(ntile)
    tile_start = tile_end - ntile
    max_sub = B * SEQ // SUB_MOE
    s = jnp.arange(max_sub, dtype=jnp.int32)[None, :]
    big = base[:, None] + 1
    cut = rem[:, None] * big
    small = jnp.maximum(base[:, None], 1)
    t_loc = jnp.where(s < cut, s // big, rem[:, None] + (s - cut) // small)
    s_loc = jnp.where(s < cut, s % big, (s - cut) % small)
    slot_row = ((tile_start[:, None] + t_loc) * TM_MOE + s_loc * SUB_MOE).reshape(-1)

    def slot_of(k):
        e_k = route[:, k].astype(jnp.int32)
        r_k = route[:, 2 + k].astype(jnp.int32)
        return slot_row[e_k * max_sub + r_k // SUB_MOE] + r_k % SUB_MOE

    pos0, pos1 = slot_of(0), slot_of(1)
    n_used = tile_end[NE - 1]
    tiles = jnp.arange(NT_MOE, dtype=jnp.int32)
    blk = jnp.minimum(tiles, n_used - 1)
    te = jnp.sum((blk[:, None] >= tile_end[None, :NE - 1]).astype(jnp.int32), axis=1)
    ns = base[te] + ((blk - tile_start[te]) < rem[te]).astype(jnp.int32)
    ns = jnp.where(tiles < n_used, ns, 0).astype(jnp.int32)
    tok = jnp.arange(B * SEQ, dtype=jnp.int32)
    tok_row = (tok // SEQ) * LT + CTX + tok % SEQ
    src = jnp.full((R_MOE,), CTX, jnp.int32).at[jnp.concatenate([pos0, pos1])].set(
        jnp.concatenate([tok_row, tok_row]))
    subs = jnp.arange(TM_MOE // SUB_MOE, dtype=jnp.int32)
    live = (subs[None, :] < ns[:, None]).astype(jnp.int32).reshape(-1)
    return pos0, pos1, te, ns, blk, src, live


def kernel(x, c, ctx, c_ctx, w_ada, b_ada, norm1_g, norm2_g, w_in, w_gate, b_gate, gla_norm_g,
           w_pool, pool_scale, w_out, ffn_w1, ffn_w3, ffn_w2, moe_w_router, moe_b_router,
           moe_w1, moe_w3, moe_w2, final_g):
    assert x.shape == (B, SEQ, D) and ctx.shape == (B, CTX, D) and DEPTH == 2

    cvec = jnp.concatenate([c, c_ctx[None, :], jnp.zeros((8 - B - 1, D), F32)], axis=0)
    mods = _ada(cvec, w_ada, b_ada)
    xs = (x, ctx)

    w_in_b = w_in.astype(BF16)
    w_r_b = jnp.pad(w_in[:, :, IN_MAIN:], ((0, 0), (0, 0), (0, LANES - 2 * RANK))).astype(BF16)
    w_out_b = w_out.astype(BF16)
    w_pool_b = w_pool.astype(BF16)

    band_np, inv_np = _pool_tables()
    band = jnp.asarray(band_np, BF16)
    inv = jnp.asarray(inv_np, F32)

    dense_te = jnp.zeros((NT // TM_DENSE,), jnp.int32)
    dense_ns = jnp.full((NT // TM_DENSE,), TM_DENSE // SUB_DENSE, jnp.int32)
    dense_blk = jnp.arange(NT // TM_DENSE, dtype=jnp.int32)

    out = None
    for l in range(DEPTH):
        mod3 = mods[l, :3].reshape(3, 1, 6 * D)
        wg = jnp.zeros((2, LANES, KEYW), F32)
        wg = wg.at[0, 0:RANK].set(w_gate[l, 0]).at[1, RANK:2 * RANK].set(w_gate[l, 1]).astype(BF16)
        bg = b_gate[l].reshape(2, 1, KEYW)

        qk, vgp, r = _inproj(xs, mod3, norm1_g[l].reshape(1, D), w_in_b, w_r_b, l)
        o_f, o_b = _gla(qk, vgp, r, wg, bg)
        mix_args = (xs, o_f, o_b, vgp, mod3, norm2_g[l].reshape(1, D), gla_norm_g[l].reshape(1, GLA_W),
                    band, inv, w_pool_b[l], pool_scale[l].reshape(1, POOL_W), w_out_b, l)
        if l % 2 == 0:
            x1, h2 = _mix(*mix_args)
            jl = l // 2
            ys = _ffn(h2, dense_te, dense_ns, dense_blk, ffn_w1[:, None], ffn_w3[:, None],
                      ffn_w2[:, None], jl, TM_DENSE, SUB_DENSE, TF_DENSE,
                      straight=(TM_DENSE // SUB_DENSE,))
            xs = (_resid(x1, ys, mod3),)
        else:
            jl = l // 2
            wr = jnp.pad(moe_w_router[jl], ((0, 0), (0, LANES - NE)))
            br = jnp.pad(moe_b_router[jl], (0, LANES - NE)).reshape(1, LANES)
            x1, h2, logits = _mix(*mix_args, wr=wr)
            route, counts = _router(logits, br)
            pos0, pos1, te, ns, blk, src, live = _moe_plan(route, counts)
            xg = _gather(h2, src, live)
            ys = _ffn(xg, te, ns, blk, moe_w1, moe_w3, moe_w2, jl, TM_MOE, SUB_MOE, TF_MOE,
                      straight=(SUBS_MOE, SUBS_MOE - 1))
            out = _combine(x1, ys, pos0, pos1, route, mod3, final_g.reshape(1, D))
    return out.reshape(B, SEQ, D)
```

```python
import functools

import numpy as np
import jax
import jax.numpy as jnp
from jax import lax
from jax.experimental import pallas as pl
from jax.experimental.pallas import tpu as pltpu

F32 = jnp.float32
BF16 = jnp.bfloat16

D = 2048
B = 2
SEQ = 4096
CTX = 256
DEPTH = 2
GRID_W = 64
EPS = 1e-6
GLA_W = 1024
POOL_W = 1024
H = 4
DV = 256
DK = 128
KEYW = H * DK
RANK = 16
TAU = 16.0
WINDOWS = (2, 4, 8, 16)
PG = 4
PGW = 256
D_FF = 5632
NE = 8

LT = CTX + SEQ
NT = B * LT
TM = 256
TPB = LT // TM
NTILES = NT // TM
LANES = 128
SLAB = D // LANES
SLAB_PITCH = SLAB + 4

CH = 64
SB = 16
RB = 256
NRB = LT // RB
assert RB == CTX and RB % CH == 0

TF_DENSE = 512
TF_MOE = 256
TM_DENSE = NT // 8
SUB_DENSE = TM_DENSE // 4
SUB_MOE = 256
SUBS_MOE = 5
TM_MOE = SUBS_MOE * SUB_MOE
N_ASSIGN = B * SEQ * 2
NT_MOE = (N_ASSIGN // SUB_MOE + NE + (SUBS_MOE - 1) * NE) // SUBS_MOE
R_MOE = NT_MOE * TM_MOE
assert SUB_MOE == TM

VMEM_LIMIT = 56 * 1024 * 1024


def _cparams(sem, vmem=VMEM_LIMIT):
    return pltpu.CompilerParams(dimension_semantics=sem, vmem_limit_bytes=vmem)


def _split2(a):
    hi = a.astype(BF16)
    lo = (a - hi.astype(F32)).astype(BF16)
    return hi, lo


def _dot(a, b):
    return jnp.dot(a, b, preferred_element_type=F32)


def _dot3(a, b):
    ah, al = _split2(a)
    bh, bl = _split2(b)
    return _dot(ah, bh) + (_dot(al, bh) + _dot(ah, bl))


def _silu(a):
    return a / (1.0 + jnp.exp(-a))


def _rms(x):
    return x * lax.rsqrt(jnp.mean(x * x, axis=-1, keepdims=True) + EPS)


def _mod_row(i):
    return jnp.where(i % TPB == 0, 2, i // TPB)


ADA_TN = 1024


def _ada_kernel(c_ref, w_ref, b_ref, o_ref):
    s = _silu(c_ref[...])
    o_ref[0] = _dot3(s, w_ref[0]) + b_ref[0]


def _ada(cvec, w_ada, b_ada):
    n = 6 * D
    return pl.pallas_call(
        _ada_kernel,
        grid=(DEPTH, n // ADA_TN),
        in_specs=[
            pl.BlockSpec((8, D), lambda l, j: (0, 0)),
            pl.BlockSpec((1, D, ADA_TN), lambda l, j: (l, 0, j)),
            pl.BlockSpec((1, 1, ADA_TN), lambda l, j: (l, 0, j)),
        ],
        out_specs=pl.BlockSpec((1, 8, ADA_TN), lambda l, j: (l, 0, j)),
        out_shape=jax.ShapeDtypeStruct((DEPTH, 8, n), F32),
        compiler_params=_cparams(("parallel", "parallel")),
        name="ada",
    )(cvec, w_ada, b_ada.reshape(DEPTH, 1, n))


IN_MAIN = 2 * KEYW + 2 * GLA_W + POOL_W


def _x_specs(xs):
    if len(xs) == 2:
        return [pl.BlockSpec((None, TM, D), lambda i: (i // TPB, jnp.maximum(i % TPB - 1, 0), 0)),
                pl.BlockSpec((None, TM, D), lambda i: (i // TPB, 0, 0))]
    return [pl.BlockSpec((TM, D), lambda i: (i, 0)),
            pl.BlockSpec((TM, D), lambda i: (i, 0)),
            pl.BlockSpec((1, 1, 6 * D), lambda i: (_mod_row(i), 0, 0))]


def _x_tile(x_refs):
    if len(x_refs) == 2:
        return jnp.where(pl.program_id(0) % TPB == 0, x_refs[1][...], x_refs[0][...])
    x_ref, y_ref, mod_ref = x_refs
    return x_ref[...] + mod_ref[0][:, 5 * D:6 * D] * y_ref[...]


def _inproj_kernel(n_x, *refs):
    x_refs = refs[:n_x]
    mod_ref, g_ref, w_ref, wr_ref, qk_ref, vgp_ref, r_ref = refs[n_x:]
    mod = mod_ref[0]
    h = (_rms(_x_tile(x_refs)) * g_ref[...] * (1.0 + mod[:, D:2 * D]) + mod[:, 0:D]).astype(BF16)
    u = _dot(h, w_ref[...])
    qk_ref[...] = u[:, :2 * KEYW]
    vgp_ref[...] = u[:, 2 * KEYW:].astype(BF16)
    r_ref[...] = _dot(h, wr_ref[...])


def _inproj(xs, mod3, g, w, wr, l):
    return pl.pallas_call(
        functools.partial(_inproj_kernel, len(xs)),
        grid=(NTILES,),
        in_specs=_x_specs(xs) + [
            pl.BlockSpec((1, 1, 6 * D), lambda i: (_mod_row(i), 0, 0)),
            pl.BlockSpec((1, D), lambda i: (0, 0)),
            pl.BlockSpec((None, D, IN_MAIN), lambda i: (l, 0, 0)),
            pl.BlockSpec((None, D, LANES), lambda i: (l, 0, 0)),
        ],
        out_specs=[
            pl.BlockSpec((TM, 2 * KEYW), lambda i: (i, 0)),
            pl.BlockSpec((TM, IN_MAIN - 2 * KEYW), lambda i: (i, 0)),
            pl.BlockSpec((TM, LANES), lambda i: (i, 0)),
        ],
        out_shape=[
            jax.ShapeDtypeStruct((NT, 2 * KEYW), F32),
            jax.ShapeDtypeStruct((NT, IN_MAIN - 2 * KEYW), BF16),
            jax.ShapeDtypeStruct((NT, LANES), F32),
        ],
        compiler_params=_cparams(("parallel",)),
        name="inproj",
    )(*xs, mod3, g, w, wr)


_NT_DIMS = (((1,), (1,)), ((), ()))
_TN_DIMS = (((0,), (0,)), ((), ()))


def _gla_gates(r_ref, wg_ref, bg_ref, rev):
    d = 1 if rev else 0
    pre = _dot(r_ref[...].astype(BF16), wg_ref[d]) + bg_ref[d]
    lg = (jnp.minimum(pre, 0.0) - jnp.log(1.0 + jnp.exp(-jnp.abs(pre)))) * (1.0 / TAU)
    row = lax.broadcasted_iota(jnp.int32, (RB, RB), 0)
    col = lax.broadcasted_iota(jnp.int32, (RB, RB), 1)
    inside = jnp.logical_and(row // CH == col // CH, (col >= row) if rev else (col <= row))
    tri = jnp.where(inside, 1.0, 0.0).astype(BF16)
    lg_hi, lg_lo = _split2(lg)
    cum = _dot(tri, lg_hi) + _dot(tri, lg_lo)
    return cum, cum - lg


def _gla_kernel(qkf_ref, qkb_ref, vf_ref, vb_ref, rf_ref, rb_ref, wg_ref, bg_ref,
                of_ref, ob_ref, stf_ref, stb_ref):
    @pl.when(pl.program_id(1) == 0)
    def _():
        stf_ref[...] = jnp.zeros_like(stf_ref)
        stb_ref[...] = jnp.zeros_like(stb_ref)

    nc = RB // CH
    dirs = ((qkf_ref, vf_ref, rf_ref, of_ref, stf_ref, False),
            (qkb_ref, vb_ref, rb_ref, ob_ref, stb_ref, True))
    rows3 = lax.broadcasted_iota(jnp.int32, (nc, CH, KEYW), 1)
    t_idx = lax.broadcasted_iota(jnp.int32, (SB, CH), 0)
    s_idx = lax.broadcasted_iota(jnp.int32, (SB, CH), 1)

    gates = [_gla_gates(r_ref, wg_ref, bg_ref, rev) for (_, _, r_ref, _, _, rev) in dirs]

    scores = {}
    for blk in range(CH // SB):
        r0 = blk * SB
        for d, (qk_ref, _, _, _, _, rev) in enumerate(dirs):
            cum, cex = gates[d]
            c3 = cum.reshape(nc, CH, KEYW)
            ce3 = cex.reshape(nc, CH, KEYW)
            q3 = (qk_ref[:, 0:KEYW] * (DK ** -0.5)).reshape(nc, CH, KEYW)
            k3 = qk_ref[:, KEYW:2 * KEYW].reshape(nc, CH, KEYW)
            if rev:
                a = ce3[:, r0 + SB - 1:r0 + SB, :]
                live = rows3 >= r0
                keep = s_idx >= t_idx + r0
            else:
                a = ce3[:, r0:r0 + 1, :]
                live = rows3 < r0 + SB
                keep = s_idx <= t_idx + r0
            qb = (q3[:, r0:r0 + SB, :] * jnp.exp(c3[:, r0:r0 + SB, :] - a)).astype(BF16)
            kb = (k3 * jnp.exp(jnp.where(live, a - c3, 0.0))).astype(BF16)
            for ci in range(nc):
                for h in range(H):
                    hk = slice(h * DK, (h + 1) * DK)
                    s = lax.dot_general(qb[ci][:, hk], kb[ci][:, hk], _NT_DIMS,
                                        preferred_element_type=F32)
                    scores.setdefault((d, ci, h), []).append(jnp.where(keep, s, 0.0))

    for d, (_, v_ref, _, o_ref, _, _) in enumerate(dirs):
        for ci in range(nc):
            rs = slice(ci * CH, (ci + 1) * CH)
            for h in range(H):
                hv = slice(h * DV, (h + 1) * DV)
                sc = jnp.concatenate(scores[(d, ci, h)], axis=0).astype(BF16)
                o_ref[rs, hv] = _dot(sc, v_ref[rs, hv])

    for d, (qk_ref, v_ref, _, o_ref, st_ref, rev) in enumerate(dirs):
        cum = gates[d][0]
        qe = (qk_ref[:, 0:KEYW] * (DK ** -0.5) * jnp.exp(cum)).astype(BF16)
        for h in range(H):
            hk = slice(h * DK, (h + 1) * DK)
            hv = slice(h * DV, (h + 1) * DV)
            st = st_ref[h]
            for ci in (reversed(range(nc)) if rev else range(nc)):
                rs = slice(ci * CH, (ci + 1) * CH)
                c = cum[rs, hk]
                o_ref[rs, hv] += lax.dot_general(qe[rs, hk], st.astype(BF16), _NT_DIMS,
                                                 preferred_element_type=F32)
                tot = c[0:1, :] if rev else c[CH - 1:CH, :]
                kd = (qk_ref[rs, KEYW + h * DK:KEYW + (h + 1) * DK] * jnp.exp(tot - c)).astype(BF16)
                st = st * jnp.exp(tot) + lax.dot_general(v_ref[rs, hv], kd, _TN_DIMS,
                                                         preferred_element_type=F32)
            st_ref[h] = st


def _bwd_block(j):
    return jnp.where(j == 0, 0, NRB - j)


def _gla(qk, vgp, r, wg, bg):
    qk3 = qk.reshape(B, LT, 2 * KEYW)
    vgp3 = vgp.reshape(B, LT, IN_MAIN - 2 * KEYW)
    r3 = r.reshape(B, LT, LANES)
    fwd = lambda b, j: (b, j, 0)
    bwd = lambda b, j: (b, _bwd_block(j), 0)
    o_f, o_b = pl.pallas_call(
        _gla_kernel,
        grid=(B, NRB),
        in_specs=[
            pl.BlockSpec((None, RB, 2 * KEYW), fwd),
            pl.BlockSpec((None, RB, 2 * KEYW), bwd),
            pl.BlockSpec((None, RB, GLA_W), fwd),
            pl.BlockSpec((None, RB, GLA_W), bwd),
            pl.BlockSpec((None, RB, LANES), fwd),
            pl.BlockSpec((None, RB, LANES), bwd),
            pl.BlockSpec((2, LANES, KEYW), lambda b, j: (0, 0, 0)),
            pl.BlockSpec((2, 1, KEYW), lambda b, j: (0, 0, 0)),
        ],
        out_specs=[
            pl.BlockSpec((None, RB, GLA_W), fwd),
            pl.BlockSpec((None, RB, GLA_W), bwd),
        ],
        out_shape=[jax.ShapeDtypeStruct((B, LT, GLA_W), F32)] * 2,
        scratch_shapes=[pltpu.VMEM((H, DV, DK), F32), pltpu.VMEM((H, DV, DK), F32)],
        compiler_params=_cparams(("parallel", "arbitrary")),
        name="gla",
    )(qk3, qk3, vgp3, vgp3, r3, r3, wg, bg)
    return o_f.reshape(NT, GLA_W), o_b.reshape(NT, GLA_W)


def _pool_tables():
    band = np.zeros((2, PG, TM, TM), np.float32)
    inv = np.zeros((2, PG, TM, PGW), np.float32)
    for kind, row_len in enumerate((GRID_W, CTX)):
        for gi, w in enumerate(WINDOWS):
            for n in range(TM):
                base, j = (n // row_len) * row_len, n % row_len
                lo = min(max(j - w // 2, 0), row_len)
                hi = min(max(j - w // 2 + w, 0), row_len)
                band[kind, gi, n, base + lo:base + hi] = 1.0
                inv[kind, gi, n, :] = 1.0 / (hi - lo)
    return band, inv


def _mix_kernel(with_router, n_x, *refs):
    x_refs = refs[:n_x]
    (of_ref, ob_ref, g_ref, p_ref, mod_ref, n2_ref, gain_ref,
     band_ref, inv_ref, wp_ref, ps_ref, wo_ref) = refs[n_x:n_x + 12]
    rest = refs[n_x + 12:]
    if with_router:
        wr_ref, xo_ref, h2_ref, lg_ref = rest
    else:
        xo_ref, h2_ref = rest
    mod = mod_ref[0]
    o = of_ref[...] + ob_ref[...]
    on = jnp.concatenate([_rms(o[:, h * DV:(h + 1) * DV]) for h in range(H)], axis=1)
    read = (on * gain_ref[...] * _silu(g_ref[...].astype(F32))).astype(BF16)
    y = _dot(read, wo_ref[0:GLA_W, :])
    pouts = []
    for gi in range(PG):
        pg = p_ref[:, gi * PGW:(gi + 1) * PGW]
        mixed = _dot(band_ref[0, gi], pg) * inv_ref[0, gi] - pg.astype(F32)
        pouts.append(_dot(mixed.astype(BF16), wp_ref[gi]))
    pool = (jnp.concatenate(pouts, axis=1) * ps_ref[...]).astype(BF16)
    y = y + _dot(pool, wo_ref[GLA_W:, :])
    xn = _x_tile(x_refs) + mod[:, 2 * D:3 * D] * y
    xo_ref[...] = xn
    h2 = _rms(xn) * n2_ref[...] * (1.0 + mod[:, 4 * D:5 * D]) + mod[:, 3 * D:4 * D]
    if with_router:
        for cb in range(SLAB):
            h2_ref[pl.ds(cb, TM, stride=SLAB), :] = h2[:, cb * LANES:(cb + 1) * LANES]
        lg_ref[...] = _dot3(h2, wr_ref[...])
    else:
        h2_ref[...] = h2.astype(BF16)


def _mix(xs, o_f, o_b, vgp, mod3, n2, gain, band, inv, wp, ps, wo, l, wr=None):
    with_router = wr is not None
    kind = lambda i: (jnp.where(i % TPB == 0, 1, 0), 0, 0, 0)
    const2 = lambda i: (0, 0)
    in_specs = _x_specs(xs) + [
        pl.BlockSpec((TM, GLA_W), lambda i: (i, 0)),
        pl.BlockSpec((TM, GLA_W), lambda i: (i, 0)),
        pl.BlockSpec((TM, GLA_W), lambda i: (i, 1)),
        pl.BlockSpec((TM, POOL_W), lambda i: (i, 2)),
        pl.BlockSpec((1, 1, 6 * D), lambda i: (_mod_row(i), 0, 0)),
        pl.BlockSpec((1, D), const2),
        pl.BlockSpec((1, GLA_W), const2),
        pl.BlockSpec((1, PG, TM, TM), kind),
        pl.BlockSpec((1, PG, TM, PGW), kind),
        pl.BlockSpec((PG, PGW, PGW), lambda i: (0, 0, 0)),
        pl.BlockSpec((1, POOL_W), const2),
        pl.BlockSpec((None, D, D), lambda i: (l, 0, 0)),
    ]
    args = [*xs, o_f, o_b, vgp, vgp, mod3, n2, gain, band, inv, wp, ps, wo]
    if with_router:
        h2_spec = pl.BlockSpec((TM * SLAB, LANES), lambda i: (i, 0))
        h2_shape = jax.ShapeDtypeStruct((NT * SLAB, LANES), F32)
    else:
        h2_spec = pl.BlockSpec((TM, D), lambda i: (i, 0))
        h2_shape = jax.ShapeDtypeStruct((NT, D), BF16)
    out_specs = [pl.BlockSpec((TM, D), lambda i: (i, 0)), h2_spec]
    out_shape = [jax.ShapeDtypeStruct((NT, D), F32), h2_shape]
    if with_router:
        in_specs.append(pl.BlockSpec((D, LANES), const2))
        args.append(wr)
        out_specs.append(pl.BlockSpec((TM, LANES), lambda i: (i, 0)))
        out_shape.append(jax.ShapeDtypeStruct((NT, LANES), F32))
    return pl.pallas_call(
        functools.partial(_mix_kernel, with_router, len(xs)),
        grid=(NTILES,),
        in_specs=in_specs,
        out_specs=out_specs,
        out_shape=out_shape,
        compiler_params=_cparams(("parallel",)),
        name="mix_router" if with_router else "mix",
    )(*args)


def _ffn_kernel(sub, straight, te_ref, ns_ref, blk_ref, xs_ref, w1_ref, w3_ref, w2_ref, o_ref):
    i = pl.program_id(0)
    j = pl.program_id(1)
    nsub = ns_ref[i]

    @pl.when(j == 0)
    def _():
        o_ref[...] = jnp.zeros_like(o_ref)

    def rows_update(rows):
        xt = xs_ref[rows, :]
        h1 = _dot(xt, w1_ref[0, 0].astype(BF16))
        h3 = _dot(xt, w3_ref[0, 0].astype(BF16))
        act = (_silu(h1) * h3).astype(BF16)
        o_ref[rows, :] += _dot(act, w2_ref[0, 0].astype(BF16))

    is_straight = False
    for n in straight:
        is_straight = jnp.logical_or(is_straight, nsub == n)

        @pl.when(nsub == n)
        def _(n=n):
            rows_update(pl.ds(0, n * sub))

    @pl.when(jnp.logical_and(nsub > 0, jnp.logical_not(is_straight)))
    def _():
        def body(s, carry):
            rows_update(pl.ds(pl.multiple_of(s * sub, sub), sub))
            return carry

        lax.fori_loop(0, nsub, body, 0)


def _ffn(xs, te, ns, blk, w1, w3, w2, wl, tm, sub, tf, straight):
    n_tiles = xs.shape[0] // tm
    nj = D_FF // tf
    out_spec = pl.BlockSpec((tm, D), lambda i, j, te, ns, blk: (i, 0))
    out_shape = jax.ShapeDtypeStruct((xs.shape[0], D), F32)
    live_j = lambda i, j, te, ns, blk: jnp.where(ns[i] > 0, j, nj - 1)
    grid_spec = pltpu.PrefetchScalarGridSpec(
        num_scalar_prefetch=3,
        grid=(n_tiles, nj),
        in_specs=[
            pl.BlockSpec((tm, D), lambda i, j, te, ns, blk: (blk[i], 0)),
            pl.BlockSpec((1, 1, D, tf), lambda i, j, te, ns, blk: (wl, te[i], 0, live_j(i, j, te, ns, blk))),
            pl.BlockSpec((1, 1, D, tf), lambda i, j, te, ns, blk: (wl, te[i], 0, live_j(i, j, te, ns, blk))),
            pl.BlockSpec((1, 1, tf, D), lambda i, j, te, ns, blk: (wl, te[i], live_j(i, j, te, ns, blk), 0)),
        ],
        out_specs=out_spec,
    )
    return pl.pallas_call(
        functools.partial(_ffn_kernel, sub, straight),
        grid_spec=grid_spec,
        out_shape=out_shape,
        compiler_params=_cparams(("arbitrary", "arbitrary")),
        name="ffn_%d" % tm,
    )(te, ns, blk, xs, w1, w3, w2)


def _lat_tile(t):
    return t + t // (TPB - 1) + 1


def _router_kernel(lg_ref, br_ref, route_ref, cnt_ref, carry_ref):
    @pl.when(pl.program_id(0) == 0)
    def _():
        carry_ref[...] = jnp.zeros_like(carry_ref)

    lane = lax.broadcasted_iota(jnp.int32, (TM, LANES), 1)
    z = jnp.where(lane < NE, lg_ref[...] + br_ref[...], -jnp.inf)
    m0 = jnp.max(z, axis=-1, keepdims=True)
    lane_f = lane.astype(F32)
    e0 = jnp.min(jnp.where(z == m0, lane_f, float(LANES)), axis=-1, keepdims=True)
    z1 = jnp.where(lane_f == e0, -jnp.inf, z)
    m1 = jnp.max(z1, axis=-1, keepdims=True)
    e1 = jnp.min(jnp.where(z1 == m1, lane_f, float(LANES)), axis=-1, keepdims=True)
    t = jnp.exp(m1 - m0)
    p0 = 1.0 / (1.0 + t)
    p1 = t / (1.0 + t)
    oh0 = lane_f == e0
    oh1 = lane_f == e1
    oh = jnp.where(jnp.logical_or(oh0, oh1), 1.0, 0.0)
    row = lax.broadcasted_iota(jnp.int32, (TM, TM), 0)
    col = lax.broadcasted_iota(jnp.int32, (TM, TM), 1)
    before = jnp.where(col < row, 1.0, 0.0).astype(BF16)
    excl = _dot(before, oh.astype(BF16)) + carry_ref[0:1, :]
    rank0 = jnp.sum(jnp.where(oh0, excl, 0.0), axis=-1, keepdims=True)
    rank1 = jnp.sum(jnp.where(oh1, excl, 0.0), axis=-1, keepdims=True)
    total = carry_ref[0:1, :] + jnp.sum(oh, axis=0, keepdims=True)
    carry_ref[...] = jnp.broadcast_to(total, carry_ref.shape)
    cnt_ref[...] = jnp.broadcast_to(total, cnt_ref.shape)
    route = jnp.where(lane == 0, e0, 0.0)
    route = jnp.where(lane == 1, e1, route)
    route = jnp.where(lane == 2, rank0, route)
    route = jnp.where(lane == 3, rank1, route)
    route = jnp.where(lane == 4, p0, route)
    route = jnp.where(lane == 5, p1, route)
    route_ref[...] = route


def _router(logits, b_r):
    n_lat_tiles = B * SEQ // TM
    return pl.pallas_call(
        _router_kernel,
        grid=(n_lat_tiles,),
        in_specs=[
            pl.BlockSpec((TM, LANES), lambda t: (_lat_tile(t), 0)),
            pl.BlockSpec((1, LANES), lambda t: (0, 0)),
        ],
        out_specs=[
            pl.BlockSpec((TM, LANES), lambda t: (t, 0)),
            pl.BlockSpec((8, LANES), lambda t: (0, 0)),
        ],
        out_shape=[jax.ShapeDtypeStruct((B * SEQ, LANES), F32),
                   jax.ShapeDtypeStruct((8, LANES), F32)],
        scratch_shapes=[pltpu.VMEM((8, LANES), F32)],
        compiler_params=_cparams(("arbitrary",)),
        name="router",
    )(logits, b_r)


ISSUE_UNROLL = 8


def _slab_copy(src_hbm, tok, dst, r, sem):
    return pltpu.make_async_copy(src_hbm.at[pl.ds(pl.multiple_of(tok * SLAB, SLAB), SLAB), :],
                                 dst.at[pl.ds(r * SLAB_PITCH, SLAB), :], sem)


def _slab_wait_all(src_hbm, dst, sem):
    pltpu.make_async_copy(src_hbm.at[pl.ds(0, TM * SLAB), :], dst.at[pl.ds(0, TM * SLAB), :], sem).wait()


def _gather_kernel(live_ref, idx_ref, h_hbm, o_ref, buf_ref, sem):
    live = live_ref[pl.program_id(0)]

    @pl.when(live == 0)
    def _():
        o_ref[...] = jnp.zeros_like(o_ref)

    @pl.when(live != 0)
    def _():
        def issue(g, carry):
            for u in range(ISSUE_UNROLL):
                r = g * ISSUE_UNROLL + u
                _slab_copy(h_hbm, idx_ref[0, 0, r], buf_ref, r, sem).start(priority=u % 2)
            return carry

        lax.fori_loop(0, TM // ISSUE_UNROLL, issue, 0)
        _slab_wait_all(h_hbm, buf_ref, sem)
        for cb in range(SLAB):
            o_ref[:, cb * LANES:(cb + 1) * LANES] = buf_ref[pl.ds(cb, TM, stride=SLAB_PITCH), :].astype(BF16)


def _gather(h2_slabs, src, live):
    n = src.shape[0] // TM
    grid_spec = pltpu.PrefetchScalarGridSpec(
        num_scalar_prefetch=1,
        grid=(n,),
        in_specs=[
            pl.BlockSpec((1, 1, TM), lambda i, live: (i, 0, 0), memory_space=pltpu.SMEM),
            pl.BlockSpec(memory_space=pl.ANY),
        ],
        out_specs=pl.BlockSpec((TM, D), lambda i, live: (i, 0)),
        scratch_shapes=[pltpu.VMEM((TM * SLAB_PITCH, LANES), F32), pltpu.SemaphoreType.DMA(())],
    )
    return pl.pallas_call(
        _gather_kernel,
        grid_spec=grid_spec,
        out_shape=jax.ShapeDtypeStruct((src.shape[0], D), BF16),
        compiler_params=_cparams(("arbitrary",)),
        name="dispatch",
    )(live, src.reshape(n, 1, TM), h2_slabs)


def _combine_kernel(p0_ref, p1_ref, x_ref, route_ref, mod_ref, fg_ref, y_hbm, o_ref,
                    b0_ref, b1_ref, sem0, sem1):
    def row_copy(row, dst, r, sem):
        return pltpu.make_async_copy(y_hbm.at[pl.ds(row, 1), :], dst.at[pl.ds(r, 1), :], sem)

    def issue(g, carry):
        for u in range(ISSUE_UNROLL):
            r = g * ISSUE_UNROLL + u
            row_copy(p0_ref[0, 0, r], b0_ref, r, sem0).start(priority=0)
            row_copy(p1_ref[0, 0, r], b1_ref, r, sem1).start(priority=1)
        return carry

    lax.fori_loop(0, TM // ISSUE_UNROLL, issue, 0)
    pltpu.make_async_copy(y_hbm.at[pl.ds(0, TM), :], b0_ref, sem0).wait()
    pltpu.make_async_copy(y_hbm.at[pl.ds(0, TM), :], b1_ref, sem1).wait()
    route = route_ref[...]
    y = route[:, 4:5] * b0_ref[...] + route[:, 5:6] * b1_ref[...]
    xn = x_ref[...] + mod_ref[0][:, 5 * D:6 * D] * y
    o_ref[...] = _rms(xn) * fg_ref[...]


def _combine(x, ys, pos0, pos1, route, mod3, fg):
    n = B * SEQ // TM
    smem = lambda: pl.BlockSpec((1, 1, TM), lambda t: (t, 0, 0), memory_space=pltpu.SMEM)
    return pl.pallas_call(
        _combine_kernel,
        grid=(n,),
        in_specs=[
            smem(), smem(),
            pl.BlockSpec((TM, D), lambda t: (_lat_tile(t), 0)),
            pl.BlockSpec((TM, LANES), lambda t: (t, 0)),
            pl.BlockSpec((1, 1, 6 * D), lambda t: (t // (TPB - 1), 0, 0)),
            pl.BlockSpec((1, D), lambda t: (0, 0)),
            pl.BlockSpec(memory_space=pl.ANY),
        ],
        out_specs=pl.BlockSpec((TM, D), lambda t: (t, 0)),
        out_shape=jax.ShapeDtypeStruct((B * SEQ, D), F32),
        scratch_shapes=[pltpu.VMEM((TM, D), F32), pltpu.VMEM((TM, D), F32),
                        pltpu.SemaphoreType.DMA(()), pltpu.SemaphoreType.DMA(())],
        compiler_params=_cparams(("arbitrary",)),
        name="combine",
    )(pos0.reshape(n, 1, TM), pos1.reshape(n, 1, TM), x, route, mod3, fg, ys)


def _moe_plan(route, counts):
    cnt = counts[0, :NE].astype(jnp.int32)
    nsub = (cnt + SUB_MOE - 1) // SUB_MOE
    ntile = (nsub + SUBS_MOE - 1) // SUBS_MOE
    tile_end = jnp.cumsum(ntile)
    tile_start = tile_end - ntile

    def slot_of(k):
        e_k = route[:, k].astype(jnp.int32)
        first = jnp.zeros_like(e_k)
        for ex in range(NE):
            first = jnp.where(e_k == ex, tile_start[ex] * TM_MOE, first)
        return first + route[:, 2 + k].astype(jnp.int32)

    pos0, pos1 = slot_of(0), slot_of(1)
    n_used = tile_end[NE - 1]
    tiles = jnp.arange(NT_MOE, dtype=jnp.int32)
    blk = jnp.minimum(tiles, n_used - 1)
    te = jnp.sum((blk[:, None] >= tile_end[None, :NE - 1]).astype(jnp.int32), axis=1)
    ns = jnp.clip(nsub[te] - (blk - tile_start[te]) * SUBS_MOE, 0, SUBS_MOE)
    ns = jnp.where(tiles < n_used, ns, 0).astype(jnp.int32)
    tok = jnp.arange(B * SEQ, dtype=jnp.int32)
    tok_row = tok + (tok // SEQ + 1) * CTX
    src = jnp.full((R_MOE,), CTX, jnp.int32).at[jnp.concatenate([pos0, pos1])].set(
        jnp.concatenate([tok_row, tok_row]))
    subs = jnp.arange(TM_MOE // SUB_MOE, dtype=jnp.int32)
    live = (subs[None, :] < ns[:, None]).astype(jnp.int32).reshape(-1)
    return pos0, pos1, te, ns, blk, src, live


def kernel(x, c, ctx, c_ctx, w_ada, b_ada, norm1_g, norm2_g, w_in, w_gate, b_gate, gla_norm_g,
           w_pool, pool_scale, w_out, ffn_w1, ffn_w3, ffn_w2, moe_w_router, moe_b_router,
           moe_w1, moe_w3, moe_w2, final_g):
    assert x.shape == (B, SEQ, D) and ctx.shape == (B, CTX, D) and DEPTH == 2

    cvec = jnp.concatenate([c, c_ctx[None, :], jnp.zeros((8 - B - 1, D), F32)], axis=0)
    mods = _ada(cvec, w_ada, b_ada)
    xs = (x, ctx)

    w_in_b = w_in.astype(BF16)
    w_r_b = jnp.pad(w_in[:, :, IN_MAIN:], ((0, 0), (0, 0), (0, LANES - 2 * RANK))).astype(BF16)
    w_out_b = w_out.astype(BF16)
    w_pool_b = w_pool.astype(BF16)

    band_np, inv_np = _pool_tables()
    band = jnp.asarray(band_np, BF16)
    inv = jnp.asarray(inv_np, F32)

    dense_te = jnp.zeros((NT // TM_DENSE,), jnp.int32)
    dense_ns = jnp.full((NT // TM_DENSE,), TM_DENSE // SUB_DENSE, jnp.int32)
    dense_blk = jnp.arange(NT // TM_DENSE, dtype=jnp.int32)

    out = None
    for l in range(DEPTH):
        mod3 = mods[l, :3].reshape(3, 1, 6 * D)
        wg = jnp.zeros((2, LANES, KEYW), F32)
        wg = wg.at[0, 0:RANK].set(w_gate[l, 0]).at[1, RANK:2 * RANK].set(w_gate[l, 1]).astype(BF16)
        bg = b_gate[l].reshape(2, 1, KEYW)

        qk, vgp, r = _inproj(xs, mod3, norm1_g[l].reshape(1, D), w_in_b, w_r_b, l)
        o_f, o_b = _gla(qk, vgp, r, wg, bg)
        mix_args = (xs, o_f, o_b, vgp, mod3, norm2_g[l].reshape(1, D), gla_norm_g[l].reshape(1, GLA_W),
                    band, inv, w_pool_b[l], pool_scale[l].reshape(1, POOL_W), w_out_b, l)
        if l % 2 == 0:
            x1, h2 = _mix(*mix_args)
            jl = l // 2
            ys = _ffn(h2, dense_te, dense_ns, dense_blk, ffn_w1[:, None], ffn_w3[:, None],
                      ffn_w2[:, None], jl, TM_DENSE, SUB_DENSE, TF_DENSE,
                      straight=(TM_DENSE // SUB_DENSE,))
            xs = (x1, ys, mod3)
        else:
            jl = l // 2
            wr = jnp.pad(moe_w_router[jl], ((0, 0), (0, LANES - NE)))
            br = jnp.pad(moe_b_router[jl], (0, LANES - NE)).reshape(1, LANES)
            x1, h2, logits = _mix(*mix_args, wr=wr)
            route, counts = _router(logits, br)
            pos0, pos1, te, ns, blk, src, live = _moe_plan(route, counts)
            xg = _gather(h2, src, live)
            ys = _ffn(xg, te, ns, blk, moe_w1, moe_w3, moe_w2, jl, TM_MOE, SUB_MOE, TF_MOE,
                      straight=(SUBS_MOE, SUBS_MOE - 1, SUBS_MOE - 2))
            out = _combine(x1, ys, pos0, pos1, route, mod3, final_g.reshape(1, D))
    return out.reshape(B, SEQ, D)
```

```python
import functools

import numpy as np
import jax
import jax.numpy as jnp
from jax import lax
from jax.experimental import pallas as pl
from jax.experimental.pallas import tpu as pltpu

F32 = jnp.float32
BF16 = jnp.bfloat16

D = 2048
B = 2
SEQ = 4096
CTX = 256
DEPTH = 2
GRID_W = 64
EPS = 1e-6
GLA_W = 1024
POOL_W = 1024
H = 4
DV = 256
DK = 128
KEYW = H * DK
RANK = 16
TAU = 16.0
WINDOWS = (2, 4, 8, 16)
PG = 4
PGW = 256
D_FF = 5632
NE = 8

LT = CTX + SEQ
NT = B * LT
TM = 256
TPB = LT // TM
NTILES = NT // TM
LANES = 128
SLAB = D // LANES
SLAB_PITCH = SLAB + 4

CH = 64
SB = 16
RB = 256
NRB = LT // RB
assert RB == CTX and RB % CH == 0

TF_DENSE = 512
TF_MOE = 256
TM_DENSE = NT // 8
SUB_DENSE = TM_DENSE // 4
SUB_MOE = 256
SUBS_MOE = 5
TM_MOE = SUBS_MOE * SUB_MOE
N_ASSIGN = B * SEQ * 2
NT_MOE = (N_ASSIGN // SUB_MOE + NE + (SUBS_MOE - 1) * NE) // SUBS_MOE
R_MOE = NT_MOE * TM_MOE
assert SUB_MOE == TM

VMEM_LIMIT = 56 * 1024 * 1024


def _cparams(sem, vmem=VMEM_LIMIT):
    return pltpu.CompilerParams(dimension_semantics=sem, vmem_limit_bytes=vmem)


def _split2(a):
    hi = a.astype(BF16)
    lo = (a - hi.astype(F32)).astype(BF16)
    return hi, lo


def _dot(a, b):
    return jnp.dot(a, b, preferred_element_type=F32)


def _dot3(a, b):
    ah, al = _split2(a)
    bh, bl = _split2(b)
    return _dot(ah, bh) + (_dot(al, bh) + _dot(ah, bl))


def _silu(a):
    return a / (1.0 + jnp.exp(-a))


def _rms(x):
    return x * lax.rsqrt(jnp.mean(x * x, axis=-1, keepdims=True) + EPS)


def _mod_row(i):
    return jnp.where(i % TPB == 0, 2, i // TPB)


ADA_TN = 1024


def _ada_kernel(c_ref, w_ref, b_ref, o_ref):
    s = _silu(c_ref[...])
    o_ref[0] = _dot3(s, w_ref[0]) + b_ref[0]


def _ada(cvec, w_ada, b_ada):
    n = 6 * D
    return pl.pallas_call(
        _ada_kernel,
        grid=(DEPTH, n // ADA_TN),
        in_specs=[
            pl.BlockSpec((8, D), lambda l, j: (0, 0)),
            pl.BlockSpec((1, D, ADA_TN), lambda l, j: (l, 0, j)),
            pl.BlockSpec((1, 1, ADA_TN), lambda l, j: (l, 0, j)),
        ],
        out_specs=pl.BlockSpec((1, 8, ADA_TN), lambda l, j: (l, 0, j)),
        out_shape=jax.ShapeDtypeStruct((DEPTH, 8, n), F32),
        compiler_params=_cparams(("parallel", "parallel")),
        name="ada",
    )(cvec, w_ada, b_ada.reshape(DEPTH, 1, n))


IN_MAIN = 2 * KEYW + 2 * GLA_W + POOL_W


def _x_specs(xs):
    if len(xs) == 2:
        return [pl.BlockSpec((None, TM, D), lambda i: (i // TPB, jnp.maximum(i % TPB - 1, 0), 0)),
                pl.BlockSpec((None, TM, D), lambda i: (i // TPB, 0, 0))]
    return [pl.BlockSpec((TM, D), lambda i: (i, 0)),
            pl.BlockSpec((TM, D), lambda i: (i, 0)),
            pl.BlockSpec((1, 1, 6 * D), lambda i: (_mod_row(i), 0, 0))]


def _x_tile(x_refs):
    if len(x_refs) == 2:
        return jnp.where(pl.program_id(0) % TPB == 0, x_refs[1][...], x_refs[0][...])
    x_ref, y_ref, mod_ref = x_refs
    return x_ref[...] + mod_ref[0][:, 5 * D:6 * D] * y_ref[...]


def _inproj_kernel(n_x, *refs):
    x_refs = refs[:n_x]
    mod_ref, g_ref, w_ref, wr_ref, qk_ref, vgp_ref, r_ref = refs[n_x:]
    mod = mod_ref[0]
    h = (_rms(_x_tile(x_refs)) * g_ref[...] * (1.0 + mod[:, D:2 * D]) + mod[:, 0:D]).astype(BF16)
    wc = 2 * KEYW
    qk_ref[...] = _dot(h, w_ref[:, 0:wc].astype(BF16))
    for c0 in range(wc, IN_MAIN, wc):
        vgp_ref[:, c0 - wc:c0] = _dot(h, w_ref[:, c0:c0 + wc].astype(BF16)).astype(BF16)
    r_ref[...] = _dot(h, wr_ref[...])


def _inproj(xs, mod3, g, w, wr, l):
    return pl.pallas_call(
        functools.partial(_inproj_kernel, len(xs)),
        grid=(NTILES,),
        in_specs=_x_specs(xs) + [
            pl.BlockSpec((1, 1, 6 * D), lambda i: (_mod_row(i), 0, 0)),
            pl.BlockSpec((1, D), lambda i: (0, 0)),
            pl.BlockSpec((None, D, IN_MAIN), lambda i: (l, 0, 0), pipeline_mode=pl.Buffered(1)),
            pl.BlockSpec((None, D, LANES), lambda i: (l, 0, 0)),
        ],
        out_specs=[
            pl.BlockSpec((TM, 2 * KEYW), lambda i: (i, 0)),
            pl.BlockSpec((TM, IN_MAIN - 2 * KEYW), lambda i: (i, 0)),
            pl.BlockSpec((TM, LANES), lambda i: (i, 0)),
        ],
        out_shape=[
            jax.ShapeDtypeStruct((NT, 2 * KEYW), F32),
            jax.ShapeDtypeStruct((NT, IN_MAIN - 2 * KEYW), BF16),
            jax.ShapeDtypeStruct((NT, LANES), F32),
        ],
        compiler_params=_cparams(("parallel",)),
        name="inproj",
    )(*xs, mod3, g, w, wr)


_NT_DIMS = (((1,), (1,)), ((), ()))
_TN_DIMS = (((0,), (0,)), ((), ()))


def _gla_gates(r_ref, wg_ref, bg_ref, rev):
    d = 1 if rev else 0
    pre = _dot(r_ref[...].astype(BF16), wg_ref[d]) + bg_ref[d]
    lg = (jnp.minimum(pre, 0.0) - jnp.log(1.0 + jnp.exp(-jnp.abs(pre)))) * (1.0 / TAU)
    row = lax.broadcasted_iota(jnp.int32, (RB, RB), 0)
    col = lax.broadcasted_iota(jnp.int32, (RB, RB), 1)
    inside = jnp.logical_and(row // CH == col // CH, (col >= row) if rev else (col <= row))
    tri = jnp.where(inside, 1.0, 0.0).astype(BF16)
    lg_hi, lg_lo = _split2(lg)
    cum = _dot(tri, lg_hi) + _dot(tri, lg_lo)
    return cum, cum - lg


def _gla_kernel(qkf_ref, qkb_ref, vf_ref, vb_ref, rf_ref, rb_ref, wg_ref, bg_ref,
                of_ref, ob_ref, stf_ref, stb_ref):
    @pl.when(pl.program_id(1) == 0)
    def _():
        stf_ref[...] = jnp.zeros_like(stf_ref)
        stb_ref[...] = jnp.zeros_like(stb_ref)

    nc = RB // CH
    dirs = ((qkf_ref, vf_ref, rf_ref, of_ref, stf_ref, False),
            (qkb_ref, vb_ref, rb_ref, ob_ref, stb_ref, True))
    rows3 = lax.broadcasted_iota(jnp.int32, (nc, CH, KEYW), 1)
    t_idx = lax.broadcasted_iota(jnp.int32, (SB, CH), 0)
    s_idx = lax.broadcasted_iota(jnp.int32, (SB, CH), 1)

    gates = [_gla_gates(r_ref, wg_ref, bg_ref, rev) for (_, _, r_ref, _, _, rev) in dirs]

    scores = {}
    for blk in range(CH // SB):
        r0 = blk * SB
        for d, (qk_ref, _, _, _, _, rev) in enumerate(dirs):
            cum, cex = gates[d]
            c3 = cum.reshape(nc, CH, KEYW)
            ce3 = cex.reshape(nc, CH, KEYW)
            q3 = (qk_ref[:, 0:KEYW] * (DK ** -0.5)).reshape(nc, CH, KEYW)
            k3 = qk_ref[:, KEYW:2 * KEYW].reshape(nc, CH, KEYW)
            if rev:
                a = ce3[:, r0 + SB - 1:r0 + SB, :]
                live = rows3 >= r0
                keep = s_idx >= t_idx + r0
            else:
                a = ce3[:, r0:r0 + 1, :]
                live = rows3 < r0 + SB
                keep = s_idx <= t_idx + r0
            qb = (q3[:, r0:r0 + SB, :] * jnp.exp(c3[:, r0:r0 + SB, :] - a)).astype(BF16)
            kb = (k3 * jnp.exp(jnp.where(live, a - c3, 0.0))).astype(BF16)
            for ci in range(nc):
                for h in range(H):
                    hk = slice(h * DK, (h + 1) * DK)
                    s = lax.dot_general(qb[ci][:, hk], kb[ci][:, hk], _NT_DIMS,
                                        preferred_element_type=F32)
                    scores.setdefault((d, ci, h), []).append(jnp.where(keep, s, 0.0))

    for d, (_, v_ref, _, o_ref, _, _) in enumerate(dirs):
        for ci in range(nc):
            rs = slice(ci * CH, (ci + 1) * CH)
            for h in range(H):
                hv = slice(h * DV, (h + 1) * DV)
                sc = jnp.concatenate(scores[(d, ci, h)], axis=0).astype(BF16)
                o_ref[rs, hv] = _dot(sc, v_ref[rs, hv])

    for d, (qk_ref, v_ref, _, o_ref, st_ref, rev) in enumerate(dirs):
        cum = gates[d][0]
        qe = (qk_ref[:, 0:KEYW] * (DK ** -0.5) * jnp.exp(cum)).astype(BF16)
        for h in range(H):
            hk = slice(h * DK, (h + 1) * DK)
            hv = slice(h * DV, (h + 1) * DV)
            st = st_ref[h]
            for ci in (reversed(range(nc)) if rev else range(nc)):
                rs = slice(ci * CH, (ci + 1) * CH)
                c = cum[rs, hk]
                o_ref[rs, hv] += lax.dot_general(qe[rs, hk], st.astype(BF16), _NT_DIMS,
                                                 preferred_element_type=F32)
                tot = c[0:1, :] if rev else c[CH - 1:CH, :]
                kd = (qk_ref[rs, KEYW + h * DK:KEYW + (h + 1) * DK] * jnp.exp(tot - c)).astype(BF16)
                st = st * jnp.exp(tot) + lax.dot_general(v_ref[rs, hv], kd, _TN_DIMS,
                                                         preferred_element_type=F32)
            st_ref[h] = st


def _bwd_block(j):
    return jnp.where(j == 0, 0, NRB - j)


def _gla(qk, vgp, r, wg, bg):
    qk3 = qk.reshape(B, LT, 2 * KEYW)
    vgp3 = vgp.reshape(B, LT, IN_MAIN - 2 * KEYW)
    r3 = r.reshape(B, LT, LANES)
    fwd = lambda b, j: (b, j, 0)
    bwd = lambda b, j: (b, _bwd_block(j), 0)
    o_f, o_b = pl.pallas_call(
        _gla_kernel,
        grid=(B, NRB),
        in_specs=[
            pl.BlockSpec((None, RB, 2 * KEYW), fwd),
            pl.BlockSpec((None, RB, 2 * KEYW), bwd),
            pl.BlockSpec((None, RB, GLA_W), fwd),
            pl.BlockSpec((None, RB, GLA_W), bwd),
            pl.BlockSpec((None, RB, LANES), fwd),
            pl.BlockSpec((None, RB, LANES), bwd),
            pl.BlockSpec((2, LANES, KEYW), lambda b, j: (0, 0, 0)),
            pl.BlockSpec((2, 1, KEYW), lambda b, j: (0, 0, 0)),
        ],
        out_specs=[
            pl.BlockSpec((None, RB, GLA_W), fwd),
            pl.BlockSpec((None, RB, GLA_W), bwd),
        ],
        out_shape=[jax.ShapeDtypeStruct((B, LT, GLA_W), F32)] * 2,
        scratch_shapes=[pltpu.VMEM((H, DV, DK), F32), pltpu.VMEM((H, DV, DK), F32)],
        compiler_params=_cparams(("parallel", "arbitrary")),
        name="gla",
    )(qk3, qk3, vgp3, vgp3, r3, r3, wg, bg)
    return o_f.reshape(NT, GLA_W), o_b.reshape(NT, GLA_W)


def _pool_tables():
    band = np.zeros((2, PG, TM, TM), np.float32)
    inv = np.zeros((2, PG, TM, PGW), np.float32)
    for kind, row_len in enumerate((GRID_W, CTX)):
        for gi, w in enumerate(WINDOWS):
            for n in range(TM):
                base, j = (n // row_len) * row_len, n % row_len
                lo = min(max(j - w // 2, 0), row_len)
                hi = min(max(j - w // 2 + w, 0), row_len)
                band[kind, gi, n, base + lo:base + hi] = 1.0
                inv[kind, gi, n, :] = 1.0 / (hi - lo)
    return band, inv


def _mix_kernel(with_router, n_x, *refs):
    x_refs = refs[:n_x]
    (of_ref, ob_ref, g_ref, p_ref, mod_ref, n2_ref, gain_ref,
     band_ref, inv_ref, wp_ref, ps_ref, wo_ref) = refs[n_x:n_x + 12]
    rest = refs[n_x + 12:]
    if with_router:
        wr_ref, xo_ref, h2_ref, lg_ref = rest
    else:
        xo_ref, h2_ref = rest
    mod = mod_ref[0]
    o = of_ref[...] + ob_ref[...]
    on = jnp.concatenate([_rms(o[:, h * DV:(h + 1) * DV]) for h in range(H)], axis=1)
    read = (on * gain_ref[...] * _silu(g_ref[...].astype(F32))).astype(BF16)
    y = _dot(read, wo_ref[0:GLA_W, :])
    pouts = []
    for gi in range(PG):
        pg = p_ref[:, gi * PGW:(gi + 1) * PGW]
        mixed = _dot(band_ref[0, gi], pg) * inv_ref[0, gi] - pg.astype(F32)
        pouts.append(_dot(mixed.astype(BF16), wp_ref[gi]))
    pool = (jnp.concatenate(pouts, axis=1) * ps_ref[...]).astype(BF16)
    y = y + _dot(pool, wo_ref[GLA_W:, :])
    xn = _x_tile(x_refs) + mod[:, 2 * D:3 * D] * y
    xo_ref[...] = xn
    h2 = _rms(xn) * n2_ref[...] * (1.0 + mod[:, 4 * D:5 * D]) + mod[:, 3 * D:4 * D]
    if with_router:
        for cb in range(SLAB):
            h2_ref[pl.ds(cb, TM, stride=SLAB), :] = h2[:, cb * LANES:(cb + 1) * LANES]
        lg_ref[...] = _dot3(h2, wr_ref[...])
    else:
        h2_ref[...] = h2.astype(BF16)


def _mix(xs, o_f, o_b, vgp, mod3, n2, gain, band, inv, wp, ps, wo, l, wr=None):
    with_router = wr is not None
    kind = lambda i: (jnp.where(i % TPB == 0, 1, 0), 0, 0, 0)
    const2 = lambda i: (0, 0)
    in_specs = _x_specs(xs) + [
        pl.BlockSpec((TM, GLA_W), lambda i: (i, 0)),
        pl.BlockSpec((TM, GLA_W), lambda i: (i, 0)),
        pl.BlockSpec((TM, GLA_W), lambda i: (i, 1)),
        pl.BlockSpec((TM, POOL_W), lambda i: (i, 2)),
        pl.BlockSpec((1, 1, 6 * D), lambda i: (_mod_row(i), 0, 0)),
        pl.BlockSpec((1, D), const2),
        pl.BlockSpec((1, GLA_W), const2),
        pl.BlockSpec((1, PG, TM, TM), kind),
        pl.BlockSpec((1, PG, TM, PGW), kind),
        pl.BlockSpec((PG, PGW, PGW), lambda i: (0, 0, 0)),
        pl.BlockSpec((1, POOL_W), const2),
        pl.BlockSpec((None, D, D), lambda i: (l, 0, 0)),
    ]
    args = [*xs, o_f, o_b, vgp, vgp, mod3, n2, gain, band, inv, wp, ps, wo]
    if with_router:
        h2_spec = pl.BlockSpec((TM * SLAB, LANES), lambda i: (i, 0))
        h2_shape = jax.ShapeDtypeStruct((NT * SLAB, LANES), F32)
    else:
        h2_spec = pl.BlockSpec((TM, D), lambda i: (i, 0))
        h2_shape = jax.ShapeDtypeStruct((NT, D), BF16)
    out_specs = [pl.BlockSpec((TM, D), lambda i: (i, 0)), h2_spec]
    out_shape = [jax.ShapeDtypeStruct((NT, D), F32), h2_shape]
    if with_router:
        in_specs.append(pl.BlockSpec((D, LANES), const2))
        args.append(wr)
        out_specs.append(pl.BlockSpec((TM, LANES), lambda i: (i, 0)))
        out_shape.append(jax.ShapeDtypeStruct((NT, LANES), F32))
    return pl.pallas_call(
        functools.partial(_mix_kernel, with_router, len(xs)),
        grid=(NTILES,),
        in_specs=in_specs,
        out_specs=out_specs,
        out_shape=out_shape,
        compiler_params=_cparams(("parallel",)),
        name="mix_router" if with_router else "mix",
    )(*args)


def _ffn_kernel(sub, straight, te_ref, ns_ref, blk_ref, xs_ref, w1_ref, w3_ref, w2_ref, o_ref):
    i = pl.program_id(0)
    j = pl.program_id(1)
    nsub = ns_ref[i]

    @pl.when(j == 0)
    def _():
        o_ref[...] = jnp.zeros_like(o_ref)

    def rows_update(rows):
        xt = xs_ref[rows, :]
        h1 = _dot(xt, w1_ref[0, 0].astype(BF16))
        h3 = _dot(xt, w3_ref[0, 0].astype(BF16))
        act = (_silu(h1) * h3).astype(BF16)
        o_ref[rows, :] += _dot(act, w2_ref[0, 0].astype(BF16))

    is_straight = False
    for n in straight:
        is_straight = jnp.logical_or(is_straight, nsub == n)

        @pl.when(nsub == n)
        def _(n=n):
            rows_update(pl.ds(0, n * sub))

    @pl.when(jnp.logical_and(nsub > 0, jnp.logical_not(is_straight)))
    def _():
        def body(s, carry):
            rows_update(pl.ds(pl.multiple_of(s * sub, sub), sub))
            return carry

        lax.fori_loop(0, nsub, body, 0)


def _ffn(xs, te, ns, blk, w1, w3, w2, wl, tm, sub, tf, straight):
    n_tiles = xs.shape[0] // tm
    nj = D_FF // tf
    out_spec = pl.BlockSpec((tm, D), lambda i, j, te, ns, blk: (i, 0))
    out_shape = jax.ShapeDtypeStruct((xs.shape[0], D), F32)
    live_j = lambda i, j, te, ns, blk: jnp.where(ns[i] > 0, j, nj - 1)
    grid_spec = pltpu.PrefetchScalarGridSpec(
        num_scalar_prefetch=3,
        grid=(n_tiles, nj),
        in_specs=[
            pl.BlockSpec((tm, D), lambda i, j, te, ns, blk: (blk[i], 0)),
            pl.BlockSpec((1, 1, D, tf), lambda i, j, te, ns, blk: (wl, te[i], 0, live_j(i, j, te, ns, blk))),
            pl.BlockSpec((1, 1, D, tf), lambda i, j, te, ns, blk: (wl, te[i], 0, live_j(i, j, te, ns, blk))),
            pl.BlockSpec((1, 1, tf, D), lambda i, j, te, ns, blk: (wl, te[i], live_j(i, j, te, ns, blk), 0)),
        ],
        out_specs=out_spec,
    )
    return pl.pallas_call(
        functools.partial(_ffn_kernel, sub, straight),
        grid_spec=grid_spec,
        out_shape=out_shape,
        compiler_params=_cparams(("arbitrary", "arbitrary")),
        name="ffn_%d" % tm,
    )(te, ns, blk, xs, w1, w3, w2)


def _lat_tile(t):
    return t + t // (TPB - 1) + 1


def _router_kernel(lg_ref, br_ref, route_ref, cnt_ref, carry_ref):
    @pl.when(pl.program_id(0) == 0)
    def _():
        carry_ref[...] = jnp.zeros_like(carry_ref)

    lane = lax.broadcasted_iota(jnp.int32, (TM, LANES), 1)
    z = jnp.where(lane < NE, lg_ref[...] + br_ref[...], -jnp.inf)
    m0 = jnp.max(z, axis=-1, keepdims=True)
    lane_f = lane.astype(F32)
    e0 = jnp.min(jnp.where(z == m0, lane_f, float(LANES)), axis=-1, keepdims=True)
    z1 = jnp.where(lane_f == e0, -jnp.inf, z)
    m1 = jnp.max(z1, axis=-1, keepdims=True)
    e1 = jnp.min(jnp.where(z1 == m1, lane_f, float(LANES)), axis=-1, keepdims=True)
    t = jnp.exp(m1 - m0)
    p0 = 1.0 / (1.0 + t)
    p1 = t / (1.0 + t)
    oh0 = lane_f == e0
    oh1 = lane_f == e1
    oh = jnp.where(jnp.logical_or(oh0, oh1), 1.0, 0.0)
    row = lax.broadcasted_iota(jnp.int32, (TM, TM), 0)
    col = lax.broadcasted_iota(jnp.int32, (TM, TM), 1)
    before = jnp.where(col < row, 1.0, 0.0).astype(BF16)
    excl = _dot(before, oh.astype(BF16)) + carry_ref[0:1, :]
    rank0 = jnp.sum(jnp.where(oh0, excl, 0.0), axis=-1, keepdims=True)
    rank1 = jnp.sum(jnp.where(oh1, excl, 0.0), axis=-1, keepdims=True)
    total = carry_ref[0:1, :] + jnp.sum(oh, axis=0, keepdims=True)
    carry_ref[...] = jnp.broadcast_to(total, carry_ref.shape)
    cnt_ref[...] = jnp.broadcast_to(total, cnt_ref.shape)
    route = jnp.where(lane == 0, e0, 0.0)
    route = jnp.where(lane == 1, e1, route)
    route = jnp.where(lane == 2, rank0, route)
    route = jnp.where(lane == 3, rank1, route)
    route = jnp.where(lane == 4, p0, route)
    route = jnp.where(lane == 5, p1, route)
    route_ref[...] = route


def _router(logits, b_r):
    n_lat_tiles = B * SEQ // TM
    return pl.pallas_call(
        _router_kernel,
        grid=(n_lat_tiles,),
        in_specs=[
            pl.BlockSpec((TM, LANES), lambda t: (_lat_tile(t), 0)),
            pl.BlockSpec((1, LANES), lambda t: (0, 0)),
        ],
        out_specs=[
            pl.BlockSpec((TM, LANES), lambda t: (t, 0)),
            pl.BlockSpec((8, LANES), lambda t: (0, 0)),
        ],
        out_shape=[jax.ShapeDtypeStruct((B * SEQ, LANES), F32),
                   jax.ShapeDtypeStruct((8, LANES), F32)],
        scratch_shapes=[pltpu.VMEM((8, LANES), F32)],
        compiler_params=_cparams(("arbitrary",)),
        name="router",
    )(logits, b_r)


ISSUE_UNROLL = 8


def _slab_copy(src_hbm, tok, dst, r, sem):
    return pltpu.make_async_copy(src_hbm.at[pl.ds(pl.multiple_of(tok * SLAB, SLAB), SLAB), :],
                                 dst.at[pl.ds(r * SLAB_PITCH, SLAB), :], sem)


def _slab_wait_all(src_hbm, dst, sem):
    pltpu.make_async_copy(src_hbm.at[pl.ds(0, TM * SLAB), :], dst.at[pl.ds(0, TM * SLAB), :], sem).wait()


def _gather_kernel(live_ref, idx_ref, idx_next_ref, h_hbm, o_ref, buf_ref, sems):
    i = pl.program_id(0)
    n = pl.num_programs(0)
    live = live_ref[i]
    nxt = jnp.minimum(i + 1, n - 1)
    start_next = jnp.logical_and(i + 1 < n, live_ref[nxt] != 0)

    def issue(ids_ref, s):
        def body(g, carry):
            for u in range(ISSUE_UNROLL):
                r = g * ISSUE_UNROLL + u
                _slab_copy(h_hbm, ids_ref[0, 0, r], buf_ref.at[s], r, sems.at[s]).start(priority=u % 2)
            return carry

        lax.fori_loop(0, TM // ISSUE_UNROLL, body, 0)

    def finish(s):
        _slab_wait_all(h_hbm, buf_ref.at[s], sems.at[s])
        for cb in range(SLAB):
            o_ref[:, cb * LANES:(cb + 1) * LANES] = (
                buf_ref[s, pl.ds(cb, TM, stride=SLAB_PITCH), :].astype(BF16))

    @pl.when(jnp.logical_and(i == 0, live != 0))
    def _():
        issue(idx_ref, 0)

    for s in range(2):
        @pl.when(jnp.logical_and(start_next, (i + 1) % 2 == s))
        def _(s=s):
            issue(idx_next_ref, s)

    @pl.when(live == 0)
    def _():
        o_ref[...] = jnp.zeros_like(o_ref)

    for s in range(2):
        @pl.when(jnp.logical_and(live != 0, i % 2 == s))
        def _(s=s):
            finish(s)


def _gather(h2_slabs, src, live):
    n = src.shape[0] // TM
    ids = src.reshape(n, 1, TM)
    grid_spec = pltpu.PrefetchScalarGridSpec(
        num_scalar_prefetch=1,
        grid=(n,),
        in_specs=[
            pl.BlockSpec((1, 1, TM), lambda i, live: (i, 0, 0), memory_space=pltpu.SMEM),
            pl.BlockSpec((1, 1, TM), lambda i, live: (jnp.minimum(i + 1, n - 1), 0, 0),
                         memory_space=pltpu.SMEM),
            pl.BlockSpec(memory_space=pl.ANY),
        ],
        out_specs=pl.BlockSpec((TM, D), lambda i, live: (i, 0)),
        scratch_shapes=[pltpu.VMEM((2, TM * SLAB_PITCH, LANES), F32), pltpu.SemaphoreType.DMA((2,))],
    )
    return pl.pallas_call(
        _gather_kernel,
        grid_spec=grid_spec,
        out_shape=jax.ShapeDtypeStruct((src.shape[0], D), BF16),
        compiler_params=_cparams(("arbitrary",)),
        name="dispatch",
    )(live, ids, ids, h2_slabs)


def _combine_kernel(p0_ref, p1_ref, p0n_ref, p1n_ref, x_ref, route_ref, mod_ref, fg_ref, y_hbm, o_ref,
                    buf_ref, sems):
    t = pl.program_id(0)
    n = pl.num_programs(0)

    def issue(q0_ref, q1_ref, s):
        def body(g, carry):
            for u in range(ISSUE_UNROLL):
                r = g * ISSUE_UNROLL + u
                for k, q_ref in enumerate((q0_ref, q1_ref)):
                    pltpu.make_async_copy(y_hbm.at[pl.ds(q_ref[0, 0, r], 1), :],
                                          buf_ref.at[s, k, pl.ds(r, 1), :], sems.at[s, k]).start(priority=k)
            return carry

        lax.fori_loop(0, TM // ISSUE_UNROLL, body, 0)

    def finish(s):
        for k in range(2):
            pltpu.make_async_copy(y_hbm.at[pl.ds(0, TM), :], buf_ref.at[s, k], sems.at[s, k]).wait()
        route = route_ref[...]
        y = route[:, 4:5] * buf_ref[s, 0] + route[:, 5:6] * buf_ref[s, 1]
        xn = x_ref[...] + mod_ref[0][:, 5 * D:6 * D] * y
        o_ref[...] = _rms(xn) * fg_ref[...]

    @pl.when(t == 0)
    def _():
        issue(p0_ref, p1_ref, 0)

    for s in range(2):
        @pl.when(jnp.logical_and(t + 1 < n, (t + 1) % 2 == s))
        def _(s=s):
            issue(p0n_ref, p1n_ref, s)

    for s in range(2):
        @pl.when(t % 2 == s)
        def _(s=s):
            finish(s)


def _combine(x, ys, pos0, pos1, route, mod3, fg):
    n = B * SEQ // TM
    smem = lambda: pl.BlockSpec((1, 1, TM), lambda t: (t, 0, 0), memory_space=pltpu.SMEM)
    smem_next = lambda: pl.BlockSpec((1, 1, TM), lambda t: (jnp.minimum(t + 1, n - 1), 0, 0),
                                     memory_space=pltpu.SMEM)
    p0 = pos0.reshape(n, 1, TM)
    p1 = pos1.reshape(n, 1, TM)
    return pl.pallas_call(
        _combine_kernel,
        grid=(n,),
        in_specs=[
            smem(), smem(), smem_next(), smem_next(),
            pl.BlockSpec((TM, D), lambda t: (_lat_tile(t), 0)),
            pl.BlockSpec((TM, LANES), lambda t: (t, 0)),
            pl.BlockSpec((1, 1, 6 * D), lambda t: (t // (TPB - 1), 0, 0)),
            pl.BlockSpec((1, D), lambda t: (0, 0)),
            pl.BlockSpec(memory_space=pl.ANY),
        ],
        out_specs=pl.BlockSpec((TM, D), lambda t: (t, 0)),
        out_shape=jax.ShapeDtypeStruct((B * SEQ, D), F32),
        scratch_shapes=[pltpu.VMEM((2, 2, TM, D), F32), pltpu.SemaphoreType.DMA((2, 2))],
        compiler_params=_cparams(("arbitrary",)),
        name="combine",
    )(p0, p1, p0, p1, x, route, mod3, fg, ys)


def _moe_plan(route, counts):
    cnt = counts[0, :NE].astype(jnp.int32)
    nsub = (cnt + SUB_MOE - 1) // SUB_MOE
    ntile = (nsub + SUBS_MOE - 1) // SUBS_MOE
    tile_end = jnp.cumsum(ntile)
    tile_start = tile_end - ntile

    def slot_of(k):
        e_k = route[:, k].astype(jnp.int32)
        first = jnp.zeros_like(e_k)
        for ex in range(NE):
            first = jnp.where(e_k == ex, tile_start[ex] * TM_MOE, first)
        return first + route[:, 2 + k].astype(jnp.int32)

    pos0, pos1 = slot_of(0), slot_of(1)
    n_used = tile_end[NE - 1]
    tiles = jnp.arange(NT_MOE, dtype=jnp.int32)
    blk = jnp.minimum(tiles, n_used - 1)
    te = jnp.sum((blk[:, None] >= tile_end[None, :NE - 1]).astype(jnp.int32), axis=1)
    ns = jnp.clip(nsub[te] - (blk - tile_start[te]) * SUBS_MOE, 0, SUBS_MOE)
    ns = jnp.where(tiles < n_used, ns, 0).astype(jnp.int32)
    tok = jnp.arange(B * SEQ, dtype=jnp.int32)
    tok_row = tok + (tok // SEQ + 1) * CTX
    src = jnp.full((R_MOE,), CTX, jnp.int32).at[jnp.concatenate([pos0, pos1])].set(
        jnp.concatenate([tok_row, tok_row]))
    subs = jnp.arange(TM_MOE // SUB_MOE, dtype=jnp.int32)
    live = (subs[None, :] < ns[:, None]).astype(jnp.int32).reshape(-1)
    return pos0, pos1, te, ns, blk, src, live


def kernel(x, c, ctx, c_ctx, w_ada, b_ada, norm1_g, norm2_g, w_in, w_gate, b_gate, gla_norm_g,
           w_pool, pool_scale, w_out, ffn_w1, ffn_w3, ffn_w2, moe_w_router, moe_b_router,
           moe_w1, moe_w3, moe_w2, final_g):
    assert x.shape == (B, SEQ, D) and ctx.shape == (B, CTX, D) and DEPTH == 2

    cvec = jnp.concatenate([c, c_ctx[None, :], jnp.zeros((8 - B - 1, D), F32)], axis=0)
    mods = _ada(cvec, w_ada, b_ada)
    xs = (x, ctx)

    w_r_b = jnp.pad(w_in[:, :, IN_MAIN:], ((0, 0), (0, 0), (0, LANES - 2 * RANK))).astype(BF16)
    w_out_b = w_out.astype(BF16)
    w_pool_b = w_pool.astype(BF16)

    band_np, inv_np = _pool_tables()
    band = jnp.asarray(band_np, BF16)
    inv = jnp.asarray(inv_np, F32)

    dense_te = jnp.zeros((NT // TM_DENSE,), jnp.int32)
    dense_ns = jnp.full((NT // TM_DENSE,), TM_DENSE // SUB_DENSE, jnp.int32)
    dense_blk = jnp.arange(NT // TM_DENSE, dtype=jnp.int32)

    out = None
    for l in range(DEPTH):
        mod3 = mods[l, :3].reshape(3, 1, 6 * D)
        wg = jnp.zeros((2, LANES, KEYW), F32)
        wg = wg.at[0, 0:RANK].set(w_gate[l, 0]).at[1, RANK:2 * RANK].set(w_gate[l, 1]).astype(BF16)
        bg = b_gate[l].reshape(2, 1, KEYW)

        qk, vgp, r = _inproj(xs, mod3, norm1_g[l].reshape(1, D), w_in, w_r_b, l)
        o_f, o_b = _gla(qk, vgp, r, wg, bg)
        mix_args = (xs, o_f, o_b, vgp, mod3, norm2_g[l].reshape(1, D), gla_norm_g[l].reshape(1, GLA_W),
                    band, inv, w_pool_b[l], pool_scale[l].reshape(1, POOL_W), w_out_b, l)
        if l % 2 == 0:
            x1, h2 = _mix(*mix_args)
            jl = l // 2
            ys = _ffn(h2, dense_te, dense_ns, dense_blk, ffn_w1[:, None], ffn_w3[:, None],
                      ffn_w2[:, None], jl, TM_DENSE, SUB_DENSE, TF_DENSE,
                      straight=(TM_DENSE // SUB_DENSE,))
            xs = (x1, ys, mod3)
        else:
            jl = l // 2
            wr = jnp.pad(moe_w_router[jl], ((0, 0), (0, LANES - NE)))
            br = jnp.pad(moe_b_router[jl], (0, LANES - NE)).reshape(1, LANES)
            x1, h2, logits = _mix(*mix_args, wr=wr)
            route, counts = _router(logits, br)
            pos0, pos1, te, ns, blk, src, live = _moe_plan(route, counts)
            xg = _gather(h2, src, live)
            ys = _ffn(xg, te, ns, blk, moe_w1, moe_w3, moe_w2, jl, TM_MOE, SUB_MOE, TF_MOE,
                      straight=(SUBS_MOE, SUBS_MOE - 1, SUBS_MOE - 2))
            out = _combine(x1, ys, pos0, pos1, route, mod3, final_g.reshape(1, D))
    return out.reshape(B, SEQ, D)
```

```python
import functools

import numpy as np
import jax
import jax.numpy as jnp
from jax import lax
from jax.experimental import pallas as pl
from jax.experimental.pallas import tpu as pltpu

F32 = jnp.float32
BF16 = jnp.bfloat16

D = 2048
B = 2
SEQ = 4096
CTX = 256
DEPTH = 2
GRID_W = 64
EPS = 1e-6
GLA_W = 1024
POOL_W = 1024
H = 4
DV = 256
DK = 128
KEYW = H * DK
RANK = 16
TAU = 16.0
WINDOWS = (2, 4, 8, 16)
PG = 4
PGW = 256
D_FF = 5632
NE = 8

LT = CTX + SEQ
NT = B * LT
TM = 256
TPB = LT // TM
NTILES = NT // TM
LANES = 128
SLAB = D // LANES
SLAB_PITCH = SLAB + 4

CH = 64
SB = 16
RB = 256
NRB = LT // RB
assert RB == CTX and RB % CH == 0

TF_DENSE = 512
TF_MOE = 256
TM_DENSE = NT // 8
SUB_DENSE = TM_DENSE // 4
SUB_MOE = 128
SUBS_MOE = 10
STRAIGHT_MOE = (10, 9, 8, 7, 6)
TM_MOE = SUBS_MOE * SUB_MOE
N_ASSIGN = B * SEQ * 2
NT_MOE = (N_ASSIGN // SUB_MOE + NE + (SUBS_MOE - 1) * NE) // SUBS_MOE
R_MOE = NT_MOE * TM_MOE
assert TM_MOE % TM == 0

VMEM_LIMIT = 56 * 1024 * 1024


def _cparams(sem, vmem=VMEM_LIMIT):
    return pltpu.CompilerParams(dimension_semantics=sem, vmem_limit_bytes=vmem)


def _split2(a):
    hi = a.astype(BF16)
    lo = (a - hi.astype(F32)).astype(BF16)
    return hi, lo


def _dot(a, b):
    return jnp.dot(a, b, preferred_element_type=F32)


def _dot3(a, b):
    ah, al = _split2(a)
    bh, bl = _split2(b)
    return _dot(ah, bh) + (_dot(al, bh) + _dot(ah, bl))


def _silu(a):
    return a / (1.0 + jnp.exp(-a))


def _rms(x):
    return x * lax.rsqrt(jnp.mean(x * x, axis=-1, keepdims=True) + EPS)


def _mod_row(i):
    return jnp.where(i % TPB == 0, 2, i // TPB)


ADA_TN = 1024


def _ada_kernel(c_ref, w_ref, b_ref, o_ref):
    s = _silu(c_ref[...])
    o_ref[0] = _dot3(s, w_ref[0]) + b_ref[0]


def _ada(cvec, w_ada, b_ada):
    n = 6 * D
    return pl.pallas_call(
        _ada_kernel,
        grid=(DEPTH, n // ADA_TN),
        in_specs=[
            pl.BlockSpec((8, D), lambda l, j: (0, 0)),
            pl.BlockSpec((1, D, ADA_TN), lambda l, j: (l, 0, j)),
            pl.BlockSpec((1, 1, ADA_TN), lambda l, j: (l, 0, j)),
        ],
        out_specs=pl.BlockSpec((1, 8, ADA_TN), lambda l, j: (l, 0, j)),
        out_shape=jax.ShapeDtypeStruct((DEPTH, 8, n), F32),
        compiler_params=_cparams(("parallel", "parallel")),
        name="ada",
    )(cvec, w_ada, b_ada.reshape(DEPTH, 1, n))


IN_MAIN = 2 * KEYW + 2 * GLA_W + POOL_W


def _x_specs(xs):
    if len(xs) == 2:
        return [pl.BlockSpec((None, TM, D), lambda i: (i // TPB, jnp.maximum(i % TPB - 1, 0), 0)),
                pl.BlockSpec((None, TM, D), lambda i: (i // TPB, 0, 0))]
    return [pl.BlockSpec((TM, D), lambda i: (i, 0)),
            pl.BlockSpec((TM, D), lambda i: (i, 0)),
            pl.BlockSpec((1, 1, 6 * D), lambda i: (_mod_row(i), 0, 0))]


def _x_tile(x_refs):
    if len(x_refs) == 2:
        return jnp.where(pl.program_id(0) % TPB == 0, x_refs[1][...], x_refs[0][...])
    x_ref, y_ref, mod_ref = x_refs
    return x_ref[...] + mod_ref[0][:, 5 * D:6 * D] * y_ref[...]


def _inproj_kernel(n_x, *refs):
    x_refs = refs[:n_x]
    mod_ref, g_ref, w_ref, wr_ref, qk_ref, vgp_ref, r_ref = refs[n_x:]
    mod = mod_ref[0]
    h = (_rms(_x_tile(x_refs)) * g_ref[...] * (1.0 + mod[:, D:2 * D]) + mod[:, 0:D]).astype(BF16)
    def proj(wt):
        return lax.dot_general(h, wt.astype(BF16), _NT_DIMS, preferred_element_type=F32)

    wc = 2 * KEYW
    qk_ref[...] = proj(w_ref[0:wc, :])
    for c0 in range(wc, IN_MAIN, wc):
        vgp_ref[:, c0 - wc:c0] = proj(w_ref[c0:c0 + wc, :]).astype(BF16)
    r_ref[...] = proj(wr_ref[...])


def _inproj(xs, mod3, g, w, wr, l):
    return pl.pallas_call(
        functools.partial(_inproj_kernel, len(xs)),
        grid=(NTILES,),
        in_specs=_x_specs(xs) + [
            pl.BlockSpec((1, 1, 6 * D), lambda i: (_mod_row(i), 0, 0)),
            pl.BlockSpec((1, D), lambda i: (0, 0)),
            pl.BlockSpec((None, IN_MAIN, D), lambda i: (l, 0, 0), pipeline_mode=pl.Buffered(1)),
            pl.BlockSpec((None, LANES, D), lambda i: (l, 0, 0)),
        ],
        out_specs=[
            pl.BlockSpec((TM, 2 * KEYW), lambda i: (i, 0)),
            pl.BlockSpec((TM, IN_MAIN - 2 * KEYW), lambda i: (i, 0)),
            pl.BlockSpec((TM, LANES), lambda i: (i, 0)),
        ],
        out_shape=[
            jax.ShapeDtypeStruct((NT, 2 * KEYW), F32),
            jax.ShapeDtypeStruct((NT, IN_MAIN - 2 * KEYW), BF16),
            jax.ShapeDtypeStruct((NT, LANES), F32),
        ],
        compiler_params=_cparams(("parallel",)),
        name="inproj",
    )(*xs, mod3, g, w, wr)


_NT_DIMS = (((1,), (1,)), ((), ()))
_TN_DIMS = (((0,), (0,)), ((), ()))


def _gla_gates(r_ref, wg_ref, bg_ref, rev):
    d = 1 if rev else 0
    pre = _dot(r_ref[...].astype(BF16), wg_ref[d]) + bg_ref[d]
    lg = (jnp.minimum(pre, 0.0) - jnp.log(1.0 + jnp.exp(-jnp.abs(pre)))) * (1.0 / TAU)
    row = lax.broadcasted_iota(jnp.int32, (RB, RB), 0)
    col = lax.broadcasted_iota(jnp.int32, (RB, RB), 1)
    inside = jnp.logical_and(row // CH == col // CH, (col >= row) if rev else (col <= row))
    tri = jnp.where(inside, 1.0, 0.0).astype(BF16)
    lg_hi, lg_lo = _split2(lg)
    cum = _dot(tri, lg_hi) + _dot(tri, lg_lo)
    return cum, cum - lg


def _gla_kernel(qkf_ref, qkb_ref, vf_ref, vb_ref, rf_ref, rb_ref, wg_ref, bg_ref,
                of_ref, ob_ref, stf_ref, stb_ref):
    @pl.when(pl.program_id(1) == 0)
    def _():
        stf_ref[...] = jnp.zeros_like(stf_ref)
        stb_ref[...] = jnp.zeros_like(stb_ref)

    nc = RB // CH
    dirs = ((qkf_ref, vf_ref, rf_ref, of_ref, stf_ref, False),
            (qkb_ref, vb_ref, rb_ref, ob_ref, stb_ref, True))
    rows3 = lax.broadcasted_iota(jnp.int32, (nc, CH, KEYW), 1)
    t_idx = lax.broadcasted_iota(jnp.int32, (SB, CH), 0)
    s_idx = lax.broadcasted_iota(jnp.int32, (SB, CH), 1)

    gates = [_gla_gates(r_ref, wg_ref, bg_ref, rev) for (_, _, r_ref, _, _, rev) in dirs]

    scores = {}
    for blk in range(CH // SB):
        r0 = blk * SB
        for d, (qk_ref, _, _, _, _, rev) in enumerate(dirs):
            cum, cex = gates[d]
            c3 = cum.reshape(nc, CH, KEYW)
            ce3 = cex.reshape(nc, CH, KEYW)
            q3 = (qk_ref[:, 0:KEYW] * (DK ** -0.5)).reshape(nc, CH, KEYW)
            k3 = qk_ref[:, KEYW:2 * KEYW].reshape(nc, CH, KEYW)
            if rev:
                a = ce3[:, r0 + SB - 1:r0 + SB, :]
                live = rows3 >= r0
                keep = s_idx >= t_idx + r0
            else:
                a = ce3[:, r0:r0 + 1, :]
                live = rows3 < r0 + SB
                keep = s_idx <= t_idx + r0
            qb = (q3[:, r0:r0 + SB, :] * jnp.exp(c3[:, r0:r0 + SB, :] - a)).astype(BF16)
            kb = (k3 * jnp.exp(jnp.where(live, a - c3, 0.0))).astype(BF16)
            for ci in range(nc):
                for h in range(H):
                    hk = slice(h * DK, (h + 1) * DK)
                    s = lax.dot_general(qb[ci][:, hk], kb[ci][:, hk], _NT_DIMS,
                                        preferred_element_type=F32)
                    scores.setdefault((d, ci, h), []).append(jnp.where(keep, s, 0.0))

    for d, (_, v_ref, _, o_ref, _, _) in enumerate(dirs):
        for ci in range(nc):
            rs = slice(ci * CH, (ci + 1) * CH)
            for h in range(H):
                hv = slice(h * DV, (h + 1) * DV)
                sc = jnp.concatenate(scores[(d, ci, h)], axis=0).astype(BF16)
                o_ref[rs, hv] = _dot(sc, v_ref[rs, hv])

    for d, (qk_ref, v_ref, _, o_ref, st_ref, rev) in enumerate(dirs):
        cum = gates[d][0]
        qe = (qk_ref[:, 0:KEYW] * (DK ** -0.5) * jnp.exp(cum)).astype(BF16)
        for h in range(H):
            hk = slice(h * DK, (h + 1) * DK)
            hv = slice(h * DV, (h + 1) * DV)
            st = st_ref[h]
            for ci in (reversed(range(nc)) if rev else range(nc)):
                rs = slice(ci * CH, (ci + 1) * CH)
                c = cum[rs, hk]
                o_ref[rs, hv] += lax.dot_general(qe[rs, hk], st.astype(BF16), _NT_DIMS,
                                                 preferred_element_type=F32)
                tot = c[0:1, :] if rev else c[CH - 1:CH, :]
                kd = (qk_ref[rs, KEYW + h * DK:KEYW + (h + 1) * DK] * jnp.exp(tot - c)).astype(BF16)
                st = st * jnp.exp(tot) + lax.dot_general(v_ref[rs, hv], kd, _TN_DIMS,
                                                         preferred_element_type=F32)
            st_ref[h] = st


def _bwd_block(j):
    return jnp.where(j == 0, 0, NRB - j)


def _gla(qk, vgp, r, wg, bg):
    qk3 = qk.reshape(B, LT, 2 * KEYW)
    vgp3 = vgp.reshape(B, LT, IN_MAIN - 2 * KEYW)
    r3 = r.reshape(B, LT, LANES)
    fwd = lambda b, j: (b, j, 0)
    bwd = lambda b, j: (b, _bwd_block(j), 0)
    o_f, o_b = pl.pallas_call(
        _gla_kernel,
        grid=(B, NRB),
        in_specs=[
            pl.BlockSpec((None, RB, 2 * KEYW), fwd),
            pl.BlockSpec((None, RB, 2 * KEYW), bwd),
            pl.BlockSpec((None, RB, GLA_W), fwd),
            pl.BlockSpec((None, RB, GLA_W), bwd),
            pl.BlockSpec((None, RB, LANES), fwd),
            pl.BlockSpec((None, RB, LANES), bwd),
            pl.BlockSpec((2, LANES, KEYW), lambda b, j: (0, 0, 0)),
            pl.BlockSpec((2, 1, KEYW), lambda b, j: (0, 0, 0)),
        ],
        out_specs=[
            pl.BlockSpec((None, RB, GLA_W), fwd),
            pl.BlockSpec((None, RB, GLA_W), bwd),
        ],
        out_shape=[jax.ShapeDtypeStruct((B, LT, GLA_W), F32)] * 2,
        scratch_shapes=[pltpu.VMEM((H, DV, DK), F32), pltpu.VMEM((H, DV, DK), F32)],
        compiler_params=_cparams(("parallel", "arbitrary")),
        name="gla",
    )(qk3, qk3, vgp3, vgp3, r3, r3, wg, bg)
    return o_f.reshape(NT, GLA_W), o_b.reshape(NT, GLA_W)


def _pool_tables():
    band = np.zeros((2, PG, TM, TM), np.float32)
    inv = np.zeros((2, PG, TM, PGW), np.float32)
    for kind, row_len in enumerate((GRID_W, CTX)):
        for gi, w in enumerate(WINDOWS):
            for n in range(TM):
                base, j = (n // row_len) * row_len, n % row_len
                lo = min(max(j - w // 2, 0), row_len)
                hi = min(max(j - w // 2 + w, 0), row_len)
                band[kind, gi, n, base + lo:base + hi] = 1.0
                inv[kind, gi, n, :] = 1.0 / (hi - lo)
    return band, inv


def _mix_kernel(with_router, n_x, *refs):
    x_refs = refs[:n_x]
    (of_ref, ob_ref, g_ref, p_ref, mod_ref, n2_ref, gain_ref,
     band_ref, inv_ref, wp_ref, ps_ref, wo_ref) = refs[n_x:n_x + 12]
    rest = refs[n_x + 12:]
    if with_router:
        wr_ref, xo_ref, h2_ref, lg_ref = rest
    else:
        xo_ref, h2_ref = rest
    mod = mod_ref[0]
    o = of_ref[...] + ob_ref[...]
    on = jnp.concatenate([_rms(o[:, h * DV:(h + 1) * DV]) for h in range(H)], axis=1)
    read = (on * gain_ref[...] * _silu(g_ref[...].astype(F32))).astype(BF16)
    y = _dot(read, wo_ref[0:GLA_W, :])
    pouts = []
    for gi in range(PG):
        pg = p_ref[:, gi * PGW:(gi + 1) * PGW]
        mixed = _dot(band_ref[0, gi], pg) * inv_ref[0, gi] - pg.astype(F32)
        pouts.append(_dot(mixed.astype(BF16), wp_ref[gi]))
    pool = (jnp.concatenate(pouts, axis=1) * ps_ref[...]).astype(BF16)
    y = y + _dot(pool, wo_ref[GLA_W:, :])
    xn = _x_tile(x_refs) + mod[:, 2 * D:3 * D] * y
    xo_ref[...] = xn
    h2 = _rms(xn) * n2_ref[...] * (1.0 + mod[:, 4 * D:5 * D]) + mod[:, 3 * D:4 * D]
    if with_router:
        for cb in range(SLAB):
            h2_ref[pl.ds(cb, TM, stride=SLAB), :] = h2[:, cb * LANES:(cb + 1) * LANES]
        lg_ref[...] = _dot3(h2, wr_ref[...])
    else:
        h2_ref[...] = h2.astype(BF16)


def _mix(xs, o_f, o_b, vgp, mod3, n2, gain, band, inv, wp, ps, wo, l, wr=None):
    with_router = wr is not None
    kind = lambda i: (jnp.where(i % TPB == 0, 1, 0), 0, 0, 0)
    const2 = lambda i: (0, 0)
    in_specs = _x_specs(xs) + [
        pl.BlockSpec((TM, GLA_W), lambda i: (i, 0)),
        pl.BlockSpec((TM, GLA_W), lambda i: (i, 0)),
        pl.BlockSpec((TM, GLA_W), lambda i: (i, 1)),
        pl.BlockSpec((TM, POOL_W), lambda i: (i, 2)),
        pl.BlockSpec((1, 1, 6 * D), lambda i: (_mod_row(i), 0, 0)),
        pl.BlockSpec((1, D), const2),
        pl.BlockSpec((1, GLA_W), const2),
        pl.BlockSpec((1, PG, TM, TM), kind),
        pl.BlockSpec((1, PG, TM, PGW), kind),
        pl.BlockSpec((PG, PGW, PGW), lambda i: (0, 0, 0)),
        pl.BlockSpec((1, POOL_W), const2),
        pl.BlockSpec((None, D, D), lambda i: (l, 0, 0)),
    ]
    args = [*xs, o_f, o_b, vgp, vgp, mod3, n2, gain, band, inv, wp, ps, wo]
    if with_router:
        h2_spec = pl.BlockSpec((TM * SLAB, LANES), lambda i: (i, 0))
        h2_shape = jax.ShapeDtypeStruct((NT * SLAB, LANES), F32)
    else:
        h2_spec = pl.BlockSpec((TM, D), lambda i: (i, 0))
        h2_shape = jax.ShapeDtypeStruct((NT, D), BF16)
    out_specs = [pl.BlockSpec((TM, D), lambda i: (i, 0)), h2_spec]
    out_shape = [jax.ShapeDtypeStruct((NT, D), F32), h2_shape]
    if with_router:
        in_specs.append(pl.BlockSpec((D, LANES), const2))
        args.append(wr)
        out_specs.append(pl.BlockSpec((TM, LANES), lambda i: (i, 0)))
        out_shape.append(jax.ShapeDtypeStruct((NT, LANES), F32))
    return pl.pallas_call(
        functools.partial(_mix_kernel, with_router, len(xs)),
        grid=(NTILES,),
        in_specs=in_specs,
        out_specs=out_specs,
        out_shape=out_shape,
        compiler_params=_cparams(("parallel",)),
        name="mix_router" if with_router else "mix",
    )(*args)


def _ffn_kernel(sub, straight, te_ref, ns_ref, blk_ref, xs_ref, w1_ref, w3_ref, w2_ref, o_ref):
    i = pl.program_id(0)
    j = pl.program_id(1)
    nsub = ns_ref[i]

    @pl.when(j == 0)
    def _():
        o_ref[...] = jnp.zeros_like(o_ref)

    def rows_update(rows):
        xt = xs_ref[rows, :]
        h1 = _dot(xt, w1_ref[0, 0].astype(BF16))
        h3 = _dot(xt, w3_ref[0, 0].astype(BF16))
        act = (_silu(h1) * h3).astype(BF16)
        o_ref[rows, :] += _dot(act, w2_ref[0, 0].astype(BF16))

    is_straight = False
    for n in straight:
        is_straight = jnp.logical_or(is_straight, nsub == n)

        @pl.when(nsub == n)
        def _(n=n):
            rows_update(pl.ds(0, n * sub))

    @pl.when(jnp.logical_and(nsub > 0, jnp.logical_not(is_straight)))
    def _():
        def body(s, carry):
            rows_update(pl.ds(pl.multiple_of(s * sub, sub), sub))
            return carry

        lax.fori_loop(0, nsub, body, 0)


def _ffn(xs, te, ns, blk, w1, w3, w2, wl, tm, sub, tf, straight):
    n_tiles = xs.shape[0] // tm
    nj = D_FF // tf
    out_spec = pl.BlockSpec((tm, D), lambda i, j, te, ns, blk: (i, 0))
    out_shape = jax.ShapeDtypeStruct((xs.shape[0], D), F32)
    live_j = lambda i, j, te, ns, blk: jnp.where(ns[i] > 0, j, nj - 1)
    grid_spec = pltpu.PrefetchScalarGridSpec(
        num_scalar_prefetch=3,
        grid=(n_tiles, nj),
        in_specs=[
            pl.BlockSpec((tm, D), lambda i, j, te, ns, blk: (blk[i], 0)),
            pl.BlockSpec((1, 1, D, tf), lambda i, j, te, ns, blk: (wl, te[i], 0, live_j(i, j, te, ns, blk))),
            pl.BlockSpec((1, 1, D, tf), lambda i, j, te, ns, blk: (wl, te[i], 0, live_j(i, j, te, ns, blk))),
            pl.BlockSpec((1, 1, tf, D), lambda i, j, te, ns, blk: (wl, te[i], live_j(i, j, te, ns, blk), 0)),
        ],
        out_specs=out_spec,
    )
    return pl.pallas_call(
        functools.partial(_ffn_kernel, sub, straight),
        grid_spec=grid_spec,
        out_shape=out_shape,
        compiler_params=_cparams(("arbitrary", "arbitrary")),
        name="ffn_%d" % tm,
    )(te, ns, blk, xs, w1, w3, w2)


def _lat_tile(t):
    return t + t // (TPB - 1) + 1


def _router_kernel(lg_ref, br_ref, route_ref, cnt_ref, carry_ref):
    @pl.when(pl.program_id(0) == 0)
    def _():
        carry_ref[...] = jnp.zeros_like(carry_ref)

    lane = lax.broadcasted_iota(jnp.int32, (TM, LANES), 1)
    z = jnp.where(lane < NE, lg_ref[...] + br_ref[...], -jnp.inf)
    m0 = jnp.max(z, axis=-1, keepdims=True)
    lane_f = lane.astype(F32)
    e0 = jnp.min(jnp.where(z == m0, lane_f, float(LANES)), axis=-1, keepdims=True)
    z1 = jnp.where(lane_f == e0, -jnp.inf, z)
    m1 = jnp.max(z1, axis=-1, keepdims=True)
    e1 = jnp.min(jnp.where(z1 == m1, lane_f, float(LANES)), axis=-1, keepdims=True)
    t = jnp.exp(m1 - m0)
    p0 = 1.0 / (1.0 + t)
    p1 = t / (1.0 + t)
    oh0 = lane_f == e0
    oh1 = lane_f == e1
    oh = jnp.where(jnp.logical_or(oh0, oh1), 1.0, 0.0)
    row = lax.broadcasted_iota(jnp.int32, (TM, TM), 0)
    col = lax.broadcasted_iota(jnp.int32, (TM, TM), 1)
    before = jnp.where(col < row, 1.0, 0.0).astype(BF16)
    excl = _dot(before, oh.astype(BF16)) + carry_ref[0:1, :]
    rank0 = jnp.sum(jnp.where(oh0, excl, 0.0), axis=-1, keepdims=True)
    rank1 = jnp.sum(jnp.where(oh1, excl, 0.0), axis=-1, keepdims=True)
    total = carry_ref[0:1, :] + jnp.sum(oh, axis=0, keepdims=True)
    carry_ref[...] = jnp.broadcast_to(total, carry_ref.shape)
    cnt_ref[...] = jnp.broadcast_to(total, cnt_ref.shape)
    route = jnp.where(lane == 0, e0, 0.0)
    route = jnp.where(lane == 1, e1, route)
    route = jnp.where(lane == 2, rank0, route)
    route = jnp.where(lane == 3, rank1, route)
    route = jnp.where(lane == 4, p0, route)
    route = jnp.where(lane == 5, p1, route)
    route_ref[...] = route


def _router(logits, b_r):
    n_lat_tiles = B * SEQ // TM
    return pl.pallas_call(
        _router_kernel,
        grid=(n_lat_tiles,),
        in_specs=[
            pl.BlockSpec((TM, LANES), lambda t: (_lat_tile(t), 0)),
            pl.BlockSpec((1, LANES), lambda t: (0, 0)),
        ],
        out_specs=[
            pl.BlockSpec((TM, LANES), lambda t: (t, 0)),
            pl.BlockSpec((8, LANES), lambda t: (0, 0)),
        ],
        out_shape=[jax.ShapeDtypeStruct((B * SEQ, LANES), F32),
                   jax.ShapeDtypeStruct((8, LANES), F32)],
        scratch_shapes=[pltpu.VMEM((8, LANES), F32)],
        compiler_params=_cparams(("arbitrary",)),
        name="router",
    )(logits, b_r)


ISSUE_UNROLL = 8


def _slab_copy(src_hbm, tok, dst, r, sem):
    return pltpu.make_async_copy(src_hbm.at[pl.ds(pl.multiple_of(tok * SLAB, SLAB), SLAB), :],
                                 dst.at[pl.ds(r * SLAB_PITCH, SLAB), :], sem)


def _slab_wait_all(src_hbm, dst, sem):
    pltpu.make_async_copy(src_hbm.at[pl.ds(0, TM * SLAB), :], dst.at[pl.ds(0, TM * SLAB), :], sem).wait()


def _gather_kernel(live_ref, idx_ref, idx_next_ref, h_hbm, o_ref, buf_ref, sems):
    i = pl.program_id(0)
    n = pl.num_programs(0)
    live = live_ref[i]
    nxt = jnp.minimum(i + 1, n - 1)
    start_next = jnp.logical_and(i + 1 < n, live_ref[nxt] != 0)

    def issue(ids_ref, s):
        def body(g, carry):
            for u in range(ISSUE_UNROLL):
                r = g * ISSUE_UNROLL + u
                _slab_copy(h_hbm, ids_ref[0, 0, r], buf_ref.at[s], r, sems.at[s]).start(priority=u % 2)
            return carry

        lax.fori_loop(0, TM // ISSUE_UNROLL, body, 0)

    def finish(s):
        _slab_wait_all(h_hbm, buf_ref.at[s], sems.at[s])
        for cb in range(SLAB):
            o_ref[:, cb * LANES:(cb + 1) * LANES] = (
                buf_ref[s, pl.ds(cb, TM, stride=SLAB_PITCH), :].astype(BF16))

    @pl.when(jnp.logical_and(i == 0, live != 0))
    def _():
        issue(idx_ref, 0)

    for s in range(2):
        @pl.when(jnp.logical_and(start_next, (i + 1) % 2 == s))
        def _(s=s):
            issue(idx_next_ref, s)

    @pl.when(live == 0)
    def _():
        o_ref[...] = jnp.zeros_like(o_ref)

    for s in range(2):
        @pl.when(jnp.logical_and(live != 0, i % 2 == s))
        def _(s=s):
            finish(s)


def _gather(h2_slabs, src, live):
    n = src.shape[0] // TM
    ids = src.reshape(n, 1, TM)
    grid_spec = pltpu.PrefetchScalarGridSpec(
        num_scalar_prefetch=1,
        grid=(n,),
        in_specs=[
            pl.BlockSpec((1, 1, TM), lambda i, live: (i, 0, 0), memory_space=pltpu.SMEM),
            pl.BlockSpec((1, 1, TM), lambda i, live: (jnp.minimum(i + 1, n - 1), 0, 0),
                         memory_space=pltpu.SMEM),
            pl.BlockSpec(memory_space=pl.ANY),
        ],
        out_specs=pl.BlockSpec((TM, D), lambda i, live: (i, 0)),
        scratch_shapes=[pltpu.VMEM((2, TM * SLAB_PITCH, LANES), F32), pltpu.SemaphoreType.DMA((2,))],
    )
    return pl.pallas_call(
        _gather_kernel,
        grid_spec=grid_spec,
        out_shape=jax.ShapeDtypeStruct((src.shape[0], D), BF16),
        compiler_params=_cparams(("arbitrary",)),
        name="dispatch",
    )(live, ids, ids, h2_slabs)


def _combine_kernel(p0_ref, p1_ref, p0n_ref, p1n_ref, x_ref, route_ref, mod_ref, fg_ref, y_hbm, o_ref,
                    buf_ref, sems):
    t = pl.program_id(0)
    n = pl.num_programs(0)

    def issue(q0_ref, q1_ref, s):
        def body(g, carry):
            for u in range(ISSUE_UNROLL):
                r = g * ISSUE_UNROLL + u
                for k, q_ref in enumerate((q0_ref, q1_ref)):
                    pltpu.make_async_copy(y_hbm.at[pl.ds(q_ref[0, 0, r], 1), :],
                                          buf_ref.at[s, k, pl.ds(r, 1), :], sems.at[s, k]).start(priority=k)
            return carry

        lax.fori_loop(0, TM // ISSUE_UNROLL, body, 0)

    def finish(s):
        for k in range(2):
            pltpu.make_async_copy(y_hbm.at[pl.ds(0, TM), :], buf_ref.at[s, k], sems.at[s, k]).wait()
        route = route_ref[...]
        y = route[:, 4:5] * buf_ref[s, 0] + route[:, 5:6] * buf_ref[s, 1]
        xn = x_ref[...] + mod_ref[0][:, 5 * D:6 * D] * y
        o_ref[...] = _rms(xn) * fg_ref[...]

    @pl.when(t == 0)
    def _():
        issue(p0_ref, p1_ref, 0)

    for s in range(2):
        @pl.when(jnp.logical_and(t + 1 < n, (t + 1) % 2 == s))
        def _(s=s):
            issue(p0n_ref, p1n_ref, s)

    for s in range(2):
        @pl.when(t % 2 == s)
        def _(s=s):
            finish(s)


def _combine(x, ys, pos0, pos1, route, mod3, fg):
    n = B * SEQ // TM
    smem = lambda: pl.BlockSpec((1, 1, TM), lambda t: (t, 0, 0), memory_space=pltpu.SMEM)
    smem_next = lambda: pl.BlockSpec((1, 1, TM), lambda t: (jnp.minimum(t + 1, n - 1), 0, 0),
                                     memory_space=pltpu.SMEM)
    p0 = pos0.reshape(n, 1, TM)
    p1 = pos1.reshape(n, 1, TM)
    return pl.pallas_call(
        _combine_kernel,
        grid=(n,),
        in_specs=[
            smem(), smem(), smem_next(), smem_next(),
            pl.BlockSpec((TM, D), lambda t: (_lat_tile(t), 0)),
            pl.BlockSpec((TM, LANES), lambda t: (t, 0)),
            pl.BlockSpec((1, 1, 6 * D), lambda t: (t // (TPB - 1), 0, 0)),
            pl.BlockSpec((1, D), lambda t: (0, 0)),
            pl.BlockSpec(memory_space=pl.ANY),
        ],
        out_specs=pl.BlockSpec((TM, D), lambda t: (t, 0)),
        out_shape=jax.ShapeDtypeStruct((B * SEQ, D), F32),
        scratch_shapes=[pltpu.VMEM((2, 2, TM, D), F32), pltpu.SemaphoreType.DMA((2, 2))],
        compiler_params=_cparams(("arbitrary",)),
        name="combine",
    )(p0, p1, p0, p1, x, route, mod3, fg, ys)


def _moe_plan(route, counts):
    cnt = counts[0, :NE].astype(jnp.int32)
    nsub = (cnt + SUB_MOE - 1) // SUB_MOE
    ntile = (nsub + SUBS_MOE - 1) // SUBS_MOE
    tile_end = jnp.cumsum(ntile)
    tile_start = tile_end - ntile

    def slot_of(k):
        e_k = route[:, k].astype(jnp.int32)
        first = jnp.zeros_like(e_k)
        for ex in range(NE):
            first = jnp.where(e_k == ex, tile_start[ex] * TM_MOE, first)
        return first + route[:, 2 + k].astype(jnp.int32)

    pos0, pos1 = slot_of(0), slot_of(1)
    n_used = tile_end[NE - 1]
    tiles = jnp.arange(NT_MOE, dtype=jnp.int32)
    blk = jnp.minimum(tiles, n_used - 1)
    te = jnp.sum((blk[:, None] >= tile_end[None, :NE - 1]).astype(jnp.int32), axis=1)
    ns = jnp.clip(nsub[te] - (blk - tile_start[te]) * SUBS_MOE, 0, SUBS_MOE)
    ns = jnp.where(tiles < n_used, ns, 0).astype(jnp.int32)
    tok = jnp.arange(B * SEQ, dtype=jnp.int32)
    tok_row = tok + (tok // SEQ + 1) * CTX
    src = jnp.full((R_MOE,), CTX, jnp.int32).at[jnp.concatenate([pos0, pos1])].set(
        jnp.concatenate([tok_row, tok_row]))
    first_row = jnp.arange(0, TM_MOE, TM, dtype=jnp.int32)
    live = (first_row[None, :] < ns[:, None] * SUB_MOE).astype(jnp.int32).reshape(-1)
    return pos0, pos1, te, ns, blk, src, live


def kernel(x, c, ctx, c_ctx, w_ada, b_ada, norm1_g, norm2_g, w_in, w_gate, b_gate, gla_norm_g,
           w_pool, pool_scale, w_out, ffn_w1, ffn_w3, ffn_w2, moe_w_router, moe_b_router,
           moe_w1, moe_w3, moe_w2, final_g):
    assert x.shape == (B, SEQ, D) and ctx.shape == (B, CTX, D) and DEPTH == 2

    cvec = jnp.concatenate([c, c_ctx[None, :], jnp.zeros((8 - B - 1, D), F32)], axis=0)
    mods = _ada(cvec, w_ada, b_ada)
    xs = (x, ctx)

    w_in_t = jnp.swapaxes(w_in, 1, 2)
    w_r_t = jnp.pad(w_in_t[:, IN_MAIN:, :], ((0, 0), (0, LANES - 2 * RANK), (0, 0)))
    w_out_b = w_out.astype(BF16)
    w_pool_b = w_pool.astype(BF16)

    band_np, inv_np = _pool_tables()
    band = jnp.asarray(band_np, BF16)
    inv = jnp.asarray(inv_np, F32)

    dense_te = jnp.zeros((NT // TM_DENSE,), jnp.int32)
    dense_ns = jnp.full((NT // TM_DENSE,), TM_DENSE // SUB_DENSE, jnp.int32)
    dense_blk = jnp.arange(NT // TM_DENSE, dtype=jnp.int32)

    out = None
    for l in range(DEPTH):
        mod3 = mods[l, :3].reshape(3, 1, 6 * D)
        wg = jnp.zeros((2, LANES, KEYW), F32)
        wg = wg.at[0, 0:RANK].set(w_gate[l, 0]).at[1, RANK:2 * RANK].set(w_gate[l, 1]).astype(BF16)
        bg = b_gate[l].reshape(2, 1, KEYW)

        qk, vgp, r = _inproj(xs, mod3, norm1_g[l].reshape(1, D), w_in_t, w_r_t, l)
        o_f, o_b = _gla(qk, vgp, r, wg, bg)
        mix_args = (xs, o_f, o_b, vgp, mod3, norm2_g[l].reshape(1, D), gla_norm_g[l].reshape(1, GLA_W),
                    band, inv, w_pool_b[l], pool_scale[l].reshape(1, POOL_W), w_out_b, l)
        if l % 2 == 0:
            x1, h2 = _mix(*mix_args)
            jl = l // 2
            ys = _ffn(h2, dense_te, dense_ns, dense_blk, ffn_w1[:, None], ffn_w3[:, None],
                      ffn_w2[:, None], jl, TM_DENSE, SUB_DENSE, TF_DENSE,
                      straight=(TM_DENSE // SUB_DENSE,))
            xs = (x1, ys, mod3)
        else:
            jl = l // 2
            wr = jnp.pad(moe_w_router[jl], ((0, 0), (0, LANES - NE)))
            br = jnp.pad(moe_b_router[jl], (0, LANES - NE)).reshape(1, LANES)
            x1, h2, logits = _mix(*mix_args, wr=wr)
            route, counts = _router(logits, br)
            pos0, pos1, te, ns, blk, src, live = _moe_plan(route, counts)
            xg = _gather(h2, src, live)
            ys = _ffn(xg, te, ns, blk, moe_w1, moe_w3, moe_w2, jl, TM_MOE, SUB_MOE, TF_MOE,
                      straight=STRAIGHT_MOE)
            out = _combine(x1, ys, pos0, pos1, route, mod3, final_g.reshape(1, D))
    return out.reshape(B, SEQ, D)
```

```python
import functools

import numpy as np
import jax
import jax.numpy as jnp
from jax import lax
from jax.experimental import pallas as pl
from jax.experimental.pallas import tpu as pltpu

F32 = jnp.float32
BF16 = jnp.bfloat16

D = 2048
B = 2
SEQ = 4096
CTX = 256
DEPTH = 2
GRID_W = 64
EPS = 1e-6
GLA_W = 1024
POOL_W = 1024
H = 4
DV = 256
DK = 128
KEYW = H * DK
RANK = 16
TAU = 16.0
WINDOWS = (2, 4, 8, 16)
PG = 4
PGW = 256
D_FF = 5632
NE = 8

LT = CTX + SEQ
NT = B * LT
TM = 256
TPB = LT // TM
NTILES = NT // TM
LANES = 128
SLAB = D // LANES
SLAB_PITCH = SLAB + 4

CH = 64
SB = 16
RB = 256
NRB = LT // RB
assert RB == CTX and RB % CH == 0

TF_DENSE = 512
TF_MOE = 256
TM_DENSE = NT // 8
SUB_DENSE = TM_DENSE // 4
SUB_MOE = 128
SUBS_MOE = 10
STRAIGHT_MOE = (10, 7, 6)
TM_MOE = SUBS_MOE * SUB_MOE
N_ASSIGN = B * SEQ * 2
NT_MOE = (N_ASSIGN // SUB_MOE + NE + (SUBS_MOE - 1) * NE) // SUBS_MOE
R_MOE = NT_MOE * TM_MOE
assert TM_MOE % TM == 0

VMEM_LIMIT = 56 * 1024 * 1024


def _cparams(sem, vmem=VMEM_LIMIT):
    return pltpu.CompilerParams(dimension_semantics=sem, vmem_limit_bytes=vmem)


def _split2(a):
    hi = a.astype(BF16)
    lo = (a - hi.astype(F32)).astype(BF16)
    return hi, lo


def _dot(a, b):
    return jnp.dot(a, b, preferred_element_type=F32)


def _dot3(a, b):
    ah, al = _split2(a)
    bh, bl = _split2(b)
    return _dot(ah, bh) + (_dot(al, bh) + _dot(ah, bl))


def _silu(a):
    return a / (1.0 + jnp.exp(-a))


def _rms(x):
    return x * lax.rsqrt(jnp.mean(x * x, axis=-1, keepdims=True) + EPS)


def _mod_row(i):
    return jnp.where(i % TPB == 0, 2, i // TPB)


ADA_TN = 2048


def _ada_kernel(c_ref, w_ref, b_ref, o_ref):
    s = _silu(c_ref[...])
    o_ref[0] = _dot3(s, w_ref[0]) + b_ref[0]


def _ada(cvec, w_ada, b_ada):
    n = 6 * D
    return pl.pallas_call(
        _ada_kernel,
        grid=(DEPTH, n // ADA_TN),
        in_specs=[
            pl.BlockSpec((8, D), lambda l, j: (0, 0)),
            pl.BlockSpec((1, D, ADA_TN), lambda l, j: (l, 0, j)),
            pl.BlockSpec((1, 1, ADA_TN), lambda l, j: (l, 0, j)),
        ],
        out_specs=pl.BlockSpec((1, 8, ADA_TN), lambda l, j: (l, 0, j)),
        out_shape=jax.ShapeDtypeStruct((DEPTH, 8, n), F32),
        compiler_params=_cparams(("parallel", "parallel")),
        name="ada",
    )(cvec, w_ada, b_ada.reshape(DEPTH, 1, n))


IN_MAIN = 2 * KEYW + 2 * GLA_W + POOL_W


def _x_specs(xs):
    if len(xs) == 2:
        return [pl.BlockSpec((None, TM, D), lambda i: (i // TPB, jnp.maximum(i % TPB - 1, 0), 0)),
                pl.BlockSpec((None, TM, D), lambda i: (i // TPB, 0, 0))]
    return [pl.BlockSpec((TM, D), lambda i: (i, 0)),
            pl.BlockSpec((TM, D), lambda i: (i, 0)),
            pl.BlockSpec((1, 1, 6 * D), lambda i: (_mod_row(i), 0, 0))]


def _x_tile(x_refs):
    if len(x_refs) == 2:
        return jnp.where(pl.program_id(0) % TPB == 0, x_refs[1][...], x_refs[0][...])
    x_ref, y_ref, mod_ref = x_refs
    return x_ref[...] + mod_ref[0][:, 5 * D:6 * D] * y_ref[...]


def _inproj_kernel(n_x, *refs):
    x_refs = refs[:n_x]
    mod_ref, g_ref, w_ref, wr_ref, qk_ref, vgp_ref, r_ref = refs[n_x:]
    mod = mod_ref[0]
    h = (_rms(_x_tile(x_refs)) * g_ref[...] * (1.0 + mod[:, D:2 * D]) + mod[:, 0:D]).astype(BF16)
    def proj(wt):
        return lax.dot_general(h, wt.astype(BF16), _NT_DIMS, preferred_element_type=F32)

    wc = 2 * KEYW
    qk_ref[...] = proj(w_ref[0:wc, :])
    for c0 in range(wc, IN_MAIN, wc):
        vgp_ref[:, c0 - wc:c0] = proj(w_ref[c0:c0 + wc, :]).astype(BF16)
    r_ref[...] = proj(wr_ref[...])


def _inproj(xs, mod3, g, w, wr, l):
    return pl.pallas_call(
        functools.partial(_inproj_kernel, len(xs)),
        grid=(NTILES,),
        in_specs=_x_specs(xs) + [
            pl.BlockSpec((1, 1, 6 * D), lambda i: (_mod_row(i), 0, 0)),
            pl.BlockSpec((1, D), lambda i: (0, 0)),
            pl.BlockSpec((None, IN_MAIN, D), lambda i: (l, 0, 0), pipeline_mode=pl.Buffered(1)),
            pl.BlockSpec((None, LANES, D), lambda i: (l, 0, 0)),
        ],
        out_specs=[
            pl.BlockSpec((TM, 2 * KEYW), lambda i: (i, 0)),
            pl.BlockSpec((TM, IN_MAIN - 2 * KEYW), lambda i: (i, 0)),
            pl.BlockSpec((TM, LANES), lambda i: (i, 0)),
        ],
        out_shape=[
            jax.ShapeDtypeStruct((NT, 2 * KEYW), F32),
            jax.ShapeDtypeStruct((NT, IN_MAIN - 2 * KEYW), BF16),
            jax.ShapeDtypeStruct((NT, LANES), F32),
        ],
        compiler_params=_cparams(("parallel",)),
        name="inproj",
    )(*xs, mod3, g, w, wr)


_NT_DIMS = (((1,), (1,)), ((), ()))
_TN_DIMS = (((0,), (0,)), ((), ()))


def _gla_gates(r_ref, wg_ref, bg_ref, rev):
    d = 1 if rev else 0
    pre = _dot(r_ref[...].astype(BF16), wg_ref[d]) + bg_ref[d]
    lg = (jnp.minimum(pre, 0.0) - jnp.log(1.0 + jnp.exp(-jnp.abs(pre)))) * (1.0 / TAU)
    row = lax.broadcasted_iota(jnp.int32, (RB, RB), 0)
    col = lax.broadcasted_iota(jnp.int32, (RB, RB), 1)
    inside = jnp.logical_and(row // CH == col // CH, (col >= row) if rev else (col <= row))
    tri = jnp.where(inside, 1.0, 0.0).astype(BF16)
    lg_hi, lg_lo = _split2(lg)
    cum = _dot(tri, lg_hi) + _dot(tri, lg_lo)
    return cum, cum - lg


def _gla_kernel(qkf_ref, qkb_ref, vf_ref, vb_ref, rf_ref, rb_ref, wg_ref, bg_ref,
                of_ref, ob_ref, stf_ref, stb_ref):
    @pl.when(pl.program_id(1) == 0)
    def _():
        stf_ref[...] = jnp.zeros_like(stf_ref)
        stb_ref[...] = jnp.zeros_like(stb_ref)

    nc = RB // CH
    dirs = ((qkf_ref, vf_ref, rf_ref, of_ref, stf_ref, False),
            (qkb_ref, vb_ref, rb_ref, ob_ref, stb_ref, True))
    rows3 = lax.broadcasted_iota(jnp.int32, (nc, CH, KEYW), 1)
    t_idx = lax.broadcasted_iota(jnp.int32, (SB, CH), 0)
    s_idx = lax.broadcasted_iota(jnp.int32, (SB, CH), 1)

    gates = [_gla_gates(r_ref, wg_ref, bg_ref, rev) for (_, _, r_ref, _, _, rev) in dirs]

    scores = {}
    for blk in range(CH // SB):
        r0 = blk * SB
        for d, (qk_ref, _, _, _, _, rev) in enumerate(dirs):
            cum, cex = gates[d]
            c3 = cum.reshape(nc, CH, KEYW)
            ce3 = cex.reshape(nc, CH, KEYW)
            q3 = (qk_ref[:, 0:KEYW] * (DK ** -0.5)).reshape(nc, CH, KEYW)
            k3 = qk_ref[:, KEYW:2 * KEYW].reshape(nc, CH, KEYW)
            if rev:
                a = ce3[:, r0 + SB - 1:r0 + SB, :]
                live = rows3 >= r0
                keep = s_idx >= t_idx + r0
            else:
                a = ce3[:, r0:r0 + 1, :]
                live = rows3 < r0 + SB
                keep = s_idx <= t_idx + r0
            qb = (q3[:, r0:r0 + SB, :] * jnp.exp(c3[:, r0:r0 + SB, :] - a)).astype(BF16)
            kb = (k3 * jnp.exp(jnp.where(live, a - c3, 0.0))).astype(BF16)
            for ci in range(nc):
                for h in range(H):
                    hk = slice(h * DK, (h + 1) * DK)
                    s = lax.dot_general(qb[ci][:, hk], kb[ci][:, hk], _NT_DIMS,
                                        preferred_element_type=F32)
                    scores.setdefault((d, ci, h), []).append(jnp.where(keep, s, 0.0))

    for d, (_, v_ref, _, o_ref, _, _) in enumerate(dirs):
        for ci in range(nc):
            rs = slice(ci * CH, (ci + 1) * CH)
            for h in range(H):
                hv = slice(h * DV, (h + 1) * DV)
                sc = jnp.concatenate(scores[(d, ci, h)], axis=0).astype(BF16)
                o_ref[rs, hv] = _dot(sc, v_ref[rs, hv])

    for d, (qk_ref, v_ref, _, o_ref, st_ref, rev) in enumerate(dirs):
        cum = gates[d][0]
        qe = (qk_ref[:, 0:KEYW] * (DK ** -0.5) * jnp.exp(cum)).astype(BF16)
        for h in range(H):
            hk = slice(h * DK, (h + 1) * DK)
            hv = slice(h * DV, (h + 1) * DV)
            st = st_ref[h]
            for ci in (reversed(range(nc)) if rev else range(nc)):
                rs = slice(ci * CH, (ci + 1) * CH)
                c = cum[rs, hk]
                o_ref[rs, hv] += lax.dot_general(qe[rs, hk], st.astype(BF16), _NT_DIMS,
                                                 preferred_element_type=F32)
                tot = c[0:1, :] if rev else c[CH - 1:CH, :]
                kd = (qk_ref[rs, KEYW + h * DK:KEYW + (h + 1) * DK] * jnp.exp(tot - c)).astype(BF16)
                st = st * jnp.exp(tot) + lax.dot_general(v_ref[rs, hv], kd, _TN_DIMS,
                                                         preferred_element_type=F32)
            st_ref[h] = st


def _bwd_block(j):
    return jnp.where(j == 0, 0, NRB - j)


def _gla(qk, vgp, r, wg, bg):
    qk3 = qk.reshape(B, LT, 2 * KEYW)
    vgp3 = vgp.reshape(B, LT, IN_MAIN - 2 * KEYW)
    r3 = r.reshape(B, LT, LANES)
    fwd = lambda b, j: (b, j, 0)
    bwd = lambda b, j: (b, _bwd_block(j), 0)
    o_f, o_b = pl.pallas_call(
        _gla_kernel,
        grid=(B, NRB),
        in_specs=[
            pl.BlockSpec((None, RB, 2 * KEYW), fwd),
            pl.BlockSpec((None, RB, 2 * KEYW), bwd),
            pl.BlockSpec((None, RB, GLA_W), fwd),
            pl.BlockSpec((None, RB, GLA_W), bwd),
            pl.BlockSpec((None, RB, LANES), fwd),
            pl.BlockSpec((None, RB, LANES), bwd),
            pl.BlockSpec((2, LANES, KEYW), lambda b, j: (0, 0, 0)),
            pl.BlockSpec((2, 1, KEYW), lambda b, j: (0, 0, 0)),
        ],
        out_specs=[
            pl.BlockSpec((None, RB, GLA_W), fwd),
            pl.BlockSpec((None, RB, GLA_W), bwd),
        ],
        out_shape=[jax.ShapeDtypeStruct((B, LT, GLA_W), F32)] * 2,
        scratch_shapes=[pltpu.VMEM((H, DV, DK), F32), pltpu.VMEM((H, DV, DK), F32)],
        compiler_params=_cparams(("parallel", "arbitrary")),
        name="gla",
    )(qk3, qk3, vgp3, vgp3, r3, r3, wg, bg)
    return o_f.reshape(NT, GLA_W), o_b.reshape(NT, GLA_W)


def _pool_tables():
    band = np.zeros((2, PG, TM, TM), np.float32)
    inv = np.zeros((2, PG, TM, PGW), np.float32)
    for kind, row_len in enumerate((GRID_W, CTX)):
        for gi, w in enumerate(WINDOWS):
            for n in range(TM):
                base, j = (n // row_len) * row_len, n % row_len
                lo = min(max(j - w // 2, 0), row_len)
                hi = min(max(j - w // 2 + w, 0), row_len)
                band[kind, gi, n, base + lo:base + hi] = 1.0
                inv[kind, gi, n, :] = 1.0 / (hi - lo)
    return band, inv


def _mix_kernel(with_router, n_x, *refs):
    x_refs = refs[:n_x]
    (of_ref, ob_ref, g_ref, p_ref, mod_ref, n2_ref, gain_ref,
     band_ref, inv_ref, wp_ref, ps_ref, wo_ref) = refs[n_x:n_x + 12]
    rest = refs[n_x + 12:]
    if with_router:
        wr_ref, xo_ref, h2_ref, lg_ref = rest
    else:
        xo_ref, h2_ref = rest
    mod = mod_ref[0]
    o = of_ref[...] + ob_ref[...]
    on = jnp.concatenate([_rms(o[:, h * DV:(h + 1) * DV]) for h in range(H)], axis=1)
    read = (on * gain_ref[...] * _silu(g_ref[...].astype(F32))).astype(BF16)
    y = _dot(read, wo_ref[0:GLA_W, :])
    pouts = []
    for gi in range(PG):
        pg = p_ref[:, gi * PGW:(gi + 1) * PGW]
        mixed = _dot(band_ref[0, gi], pg) * inv_ref[0, gi] - pg.astype(F32)
        pouts.append(_dot(mixed.astype(BF16), wp_ref[gi]))
    pool = (jnp.concatenate(pouts, axis=1) * ps_ref[...]).astype(BF16)
    y = y + _dot(pool, wo_ref[GLA_W:, :])
    xn = _x_tile(x_refs) + mod[:, 2 * D:3 * D] * y
    xo_ref[...] = xn
    h2 = _rms(xn) * n2_ref[...] * (1.0 + mod[:, 4 * D:5 * D]) + mod[:, 3 * D:4 * D]
    if with_router:
        for cb in range(SLAB):
            h2_ref[pl.ds(cb, TM, stride=SLAB), :] = h2[:, cb * LANES:(cb + 1) * LANES]
        h_hi, h_lo = _split2(h2)
        both = _dot(h_hi, wr_ref[...])
        lg_ref[...] = both[:, :LANES] + (both[:, LANES:] + _dot(h_lo, wr_ref[:, :LANES]))
    else:
        h2_ref[...] = h2.astype(BF16)


def _mix(xs, o_f, o_b, vgp, mod3, n2, gain, band, inv, wp, ps, wo, l, wr=None):
    with_router = wr is not None
    kind = lambda i: (jnp.where(i % TPB == 0, 1, 0), 0, 0, 0)
    const2 = lambda i: (0, 0)
    in_specs = _x_specs(xs) + [
        pl.BlockSpec((TM, GLA_W), lambda i: (i, 0)),
        pl.BlockSpec((TM, GLA_W), lambda i: (i, 0)),
        pl.BlockSpec((TM, GLA_W), lambda i: (i, 1)),
        pl.BlockSpec((TM, POOL_W), lambda i: (i, 2)),
        pl.BlockSpec((1, 1, 6 * D), lambda i: (_mod_row(i), 0, 0)),
        pl.BlockSpec((1, D), const2),
        pl.BlockSpec((1, GLA_W), const2),
        pl.BlockSpec((1, PG, TM, TM), kind),
        pl.BlockSpec((1, PG, TM, PGW), kind),
        pl.BlockSpec((PG, PGW, PGW), lambda i: (0, 0, 0)),
        pl.BlockSpec((1, POOL_W), const2),
        pl.BlockSpec((None, D, D), lambda i: (l, 0, 0)),
    ]
    args = [*xs, o_f, o_b, vgp, vgp, mod3, n2, gain, band, inv, wp, ps, wo]
    if with_router:
        h2_spec = pl.BlockSpec((TM * SLAB, LANES), lambda i: (i, 0))
        h2_shape = jax.ShapeDtypeStruct((NT * SLAB, LANES), F32)
    else:
        h2_spec = pl.BlockSpec((TM, D), lambda i: (i, 0))
        h2_shape = jax.ShapeDtypeStruct((NT, D), BF16)
    out_specs = [pl.BlockSpec((TM, D), lambda i: (i, 0)), h2_spec]
    out_shape = [jax.ShapeDtypeStruct((NT, D), F32), h2_shape]
    if with_router:
        in_specs.append(pl.BlockSpec((D, 2 * LANES), const2))
        args.append(wr)
        out_specs.append(pl.BlockSpec((TM, LANES), lambda i: (i, 0)))
        out_shape.append(jax.ShapeDtypeStruct((NT, LANES), F32))
    return pl.pallas_call(
        functools.partial(_mix_kernel, with_router, len(xs)),
        grid=(NTILES,),
        in_specs=in_specs,
        out_specs=out_specs,
        out_shape=out_shape,
        compiler_params=_cparams(("parallel",)),
        name="mix_router" if with_router else "mix",
    )(*args)


def _ffn_kernel(sub, straight, te_ref, ns_ref, blk_ref, xs_ref, w1_ref, w3_ref, w2_ref, o_ref):
    i = pl.program_id(0)
    j = pl.program_id(1)
    nsub = ns_ref[i]

    @pl.when(j == 0)
    def _():
        o_ref[...] = jnp.zeros_like(o_ref)

    def rows_update(rows):
        xt = xs_ref[rows, :]
        h1 = _dot(xt, w1_ref[0, 0].astype(BF16))
        h3 = _dot(xt, w3_ref[0, 0].astype(BF16))
        act = (_silu(h1) * h3).astype(BF16)
        o_ref[rows, :] += _dot(act, w2_ref[0, 0].astype(BF16))

    is_straight = False
    for n in straight:
        is_straight = jnp.logical_or(is_straight, nsub == n)

        @pl.when(nsub == n)
        def _(n=n):
            rows_update(pl.ds(0, n * sub))

    @pl.when(jnp.logical_and(nsub > 0, jnp.logical_not(is_straight)))
    def _():
        def body(s, carry):
            rows_update(pl.ds(pl.multiple_of(s * sub, sub), sub))
            return carry

        lax.fori_loop(0, nsub, body, 0)


def _ffn(xs, te, ns, blk, w1, w3, w2, wl, tm, sub, tf, straight):
    n_tiles = xs.shape[0] // tm
    nj = D_FF // tf
    out_spec = pl.BlockSpec((tm, D), lambda i, j, te, ns, blk: (i, 0))
    out_shape = jax.ShapeDtypeStruct((xs.shape[0], D), F32)
    live_j = lambda i, j, te, ns, blk: jnp.where(ns[i] > 0, j, nj - 1)
    grid_spec = pltpu.PrefetchScalarGridSpec(
        num_scalar_prefetch=3,
        grid=(n_tiles, nj),
        in_specs=[
            pl.BlockSpec((tm, D), lambda i, j, te, ns, blk: (blk[i], 0)),
            pl.BlockSpec((1, 1, D, tf), lambda i, j, te, ns, blk: (wl, te[i], 0, live_j(i, j, te, ns, blk))),
            pl.BlockSpec((1, 1, D, tf), lambda i, j, te, ns, blk: (wl, te[i], 0, live_j(i, j, te, ns, blk))),
            pl.BlockSpec((1, 1, tf, D), lambda i, j, te, ns, blk: (wl, te[i], live_j(i, j, te, ns, blk), 0)),
        ],
        out_specs=out_spec,
    )
    return pl.pallas_call(
        functools.partial(_ffn_kernel, sub, straight),
        grid_spec=grid_spec,
        out_shape=out_shape,
        compiler_params=_cparams(("arbitrary", "arbitrary")),
        name="ffn_%d" % tm,
    )(te, ns, blk, xs, w1, w3, w2)


def _lat_tile(t):
    return t + t // (TPB - 1) + 1


def _router_kernel(lg_ref, br_ref, route_ref, cnt_ref, carry_ref):
    @pl.when(pl.program_id(0) == 0)
    def _():
        carry_ref[...] = jnp.zeros_like(carry_ref)

    lane = lax.broadcasted_iota(jnp.int32, (TM, LANES), 1)
    z = jnp.where(lane < NE, lg_ref[...] + br_ref[...], -jnp.inf)
    m0 = jnp.max(z, axis=-1, keepdims=True)
    lane_f = lane.astype(F32)
    e0 = jnp.min(jnp.where(z == m0, lane_f, float(LANES)), axis=-1, keepdims=True)
    z1 = jnp.where(lane_f == e0, -jnp.inf, z)
    m1 = jnp.max(z1, axis=-1, keepdims=True)
    e1 = jnp.min(jnp.where(z1 == m1, lane_f, float(LANES)), axis=-1, keepdims=True)
    t = jnp.exp(m1 - m0)
    p0 = 1.0 / (1.0 + t)
    p1 = t / (1.0 + t)
    oh0 = lane_f == e0
    oh1 = lane_f == e1
    oh = jnp.where(jnp.logical_or(oh0, oh1), 1.0, 0.0)
    row = lax.broadcasted_iota(jnp.int32, (TM, TM), 0)
    col = lax.broadcasted_iota(jnp.int32, (TM, TM), 1)
    before = jnp.where(col < row, 1.0, 0.0).astype(BF16)
    excl = _dot(before, oh.astype(BF16)) + carry_ref[0:1, :]
    rank0 = jnp.sum(jnp.where(oh0, excl, 0.0), axis=-1, keepdims=True)
    rank1 = jnp.sum(jnp.where(oh1, excl, 0.0), axis=-1, keepdims=True)
    total = carry_ref[0:1, :] + jnp.sum(oh, axis=0, keepdims=True)
    carry_ref[...] = jnp.broadcast_to(total, carry_ref.shape)
    cnt_ref[...] = jnp.broadcast_to(total, cnt_ref.shape)
    route = jnp.where(lane == 0, e0, 0.0)
    route = jnp.where(lane == 1, e1, route)
    route = jnp.where(lane == 2, rank0, route)
    route = jnp.where(lane == 3, rank1, route)
    route = jnp.where(lane == 4, p0, route)
    route = jnp.where(lane == 5, p1, route)
    route_ref[...] = route


def _router(logits, b_r):
    n_lat_tiles = B * SEQ // TM
    return pl.pallas_call(
        _router_kernel,
        grid=(n_lat_tiles,),
        in_specs=[
            pl.BlockSpec((TM, LANES), lambda t: (_lat_tile(t), 0)),
            pl.BlockSpec((1, LANES), lambda t: (0, 0)),
        ],
        out_specs=[
            pl.BlockSpec((TM, LANES), lambda t: (t, 0)),
            pl.BlockSpec((8, LANES), lambda t: (0, 0)),
        ],
        out_shape=[jax.ShapeDtypeStruct((B * SEQ, LANES), F32),
                   jax.ShapeDtypeStruct((8, LANES), F32)],
        scratch_shapes=[pltpu.VMEM((8, LANES), F32)],
        compiler_params=_cparams(("arbitrary",)),
        name="router",
    )(logits, b_r)


ISSUE_UNROLL = 8


def _slab_copy(src_hbm, tok, dst, r, sem):
    return pltpu.make_async_copy(src_hbm.at[pl.ds(pl.multiple_of(tok * SLAB, SLAB), SLAB), :],
                                 dst.at[pl.ds(r * SLAB_PITCH, SLAB), :], sem)


def _slab_wait_all(src_hbm, dst, sem):
    pltpu.make_async_copy(src_hbm.at[pl.ds(0, TM * SLAB), :], dst.at[pl.ds(0, TM * SLAB), :], sem).wait()


def _gather_kernel(live_ref, idx_ref, idx_next_ref, h_hbm, o_ref, buf_ref, sems):
    i = pl.program_id(0)
    n = pl.num_programs(0)
    live = live_ref[i]
    nxt = jnp.minimum(i + 1, n - 1)
    start_next = jnp.logical_and(i + 1 < n, live_ref[nxt] != 0)

    def issue(ids_ref, s):
        def body(g, carry):
            for u in range(ISSUE_UNROLL):
                r = g * ISSUE_UNROLL + u
                _slab_copy(h_hbm, ids_ref[0, 0, r], buf_ref.at[s], r, sems.at[s]).start(priority=u % 2)
            return carry

        lax.fori_loop(0, TM // ISSUE_UNROLL, body, 0)

    def finish(s):
        _slab_wait_all(h_hbm, buf_ref.at[s], sems.at[s])
        for cb in range(SLAB):
            o_ref[:, cb * LANES:(cb + 1) * LANES] = (
                buf_ref[s, pl.ds(cb, TM, stride=SLAB_PITCH), :].astype(BF16))

    @pl.when(jnp.logical_and(i == 0, live != 0))
    def _():
        issue(idx_ref, 0)

    for s in range(2):
        @pl.when(jnp.logical_and(start_next, (i + 1) % 2 == s))
        def _(s=s):
            issue(idx_next_ref, s)

    @pl.when(live == 0)
    def _():
        o_ref[...] = jnp.zeros_like(o_ref)

    for s in range(2):
        @pl.when(jnp.logical_and(live != 0, i % 2 == s))
        def _(s=s):
            finish(s)


def _gather(h2_slabs, src, live):
    n = src.shape[0] // TM
    ids = src.reshape(n, 1, TM)
    grid_spec = pltpu.PrefetchScalarGridSpec(
        num_scalar_prefetch=1,
        grid=(n,),
        in_specs=[
            pl.BlockSpec((1, 1, TM), lambda i, live: (i, 0, 0), memory_space=pltpu.SMEM),
            pl.BlockSpec((1, 1, TM), lambda i, live: (jnp.minimum(i + 1, n - 1), 0, 0),
                         memory_space=pltpu.SMEM),
            pl.BlockSpec(memory_space=pl.ANY),
        ],
        out_specs=pl.BlockSpec((TM, D), lambda i, live: (i, 0)),
        scratch_shapes=[pltpu.VMEM((2, TM * SLAB_PITCH, LANES), F32), pltpu.SemaphoreType.DMA((2,))],
    )
    return pl.pallas_call(
        _gather_kernel,
        grid_spec=grid_spec,
        out_shape=jax.ShapeDtypeStruct((src.shape[0], D), BF16),
        compiler_params=_cparams(("arbitrary",)),
        name="dispatch",
    )(live, ids, ids, h2_slabs)


def _combine_kernel(p0_ref, p1_ref, p0n_ref, p1n_ref, x_ref, route_ref, mod_ref, fg_ref, y_hbm, o_ref,
                    buf_ref, sems):
    t = pl.program_id(0)
    n = pl.num_programs(0)

    def issue(q0_ref, q1_ref, s):
        def body(g, carry):
            for u in range(ISSUE_UNROLL):
                r = g * ISSUE_UNROLL + u
                for k, q_ref in enumerate((q0_ref, q1_ref)):
                    pltpu.make_async_copy(y_hbm.at[pl.ds(q_ref[0, 0, r], 1), :],
                                          buf_ref.at[s, k, pl.ds(r, 1), :], sems.at[s, k]).start(priority=k)
            return carry

        lax.fori_loop(0, TM // ISSUE_UNROLL, body, 0)

    def finish(s):
        for k in range(2):
            pltpu.make_async_copy(y_hbm.at[pl.ds(0, TM), :], buf_ref.at[s, k], sems.at[s, k]).wait()
        route = route_ref[...]
        y = route[:, 4:5] * buf_ref[s, 0] + route[:, 5:6] * buf_ref[s, 1]
        xn = x_ref[...] + mod_ref[0][:, 5 * D:6 * D] * y
        o_ref[...] = _rms(xn) * fg_ref[...]

    @pl.when(t == 0)
    def _():
        issue(p0_ref, p1_ref, 0)

    for s in range(2):
        @pl.when(jnp.logical_and(t + 1 < n, (t + 1) % 2 == s))
        def _(s=s):
            issue(p0n_ref, p1n_ref, s)

    for s in range(2):
        @pl.when(t % 2 == s)
        def _(s=s):
            finish(s)


def _combine(x, ys, pos0, pos1, route, mod3, fg):
    n = B * SEQ // TM
    smem = lambda: pl.BlockSpec((1, 1, TM), lambda t: (t, 0, 0), memory_space=pltpu.SMEM)
    smem_next = lambda: pl.BlockSpec((1, 1, TM), lambda t: (jnp.minimum(t + 1, n - 1), 0, 0),
                                     memory_space=pltpu.SMEM)
    p0 = pos0.reshape(n, 1, TM)
    p1 = pos1.reshape(n, 1, TM)
    return pl.pallas_call(
        _combine_kernel,
        grid=(n,),
        in_specs=[
            smem(), smem(), smem_next(), smem_next(),
            pl.BlockSpec((TM, D), lambda t: (_lat_tile(t), 0)),
            pl.BlockSpec((TM, LANES), lambda t: (t, 0)),
            pl.BlockSpec((1, 1, 6 * D), lambda t: (t // (TPB - 1), 0, 0)),
            pl.BlockSpec((1, D), lambda t: (0, 0)),
            pl.BlockSpec(memory_space=pl.ANY),
        ],
        out_specs=pl.BlockSpec((TM, D), lambda t: (t, 0)),
        out_shape=jax.ShapeDtypeStruct((B * SEQ, D), F32),
        scratch_shapes=[pltpu.VMEM((2, 2, TM, D), F32), pltpu.SemaphoreType.DMA((2, 2))],
        compiler_params=_cparams(("arbitrary",)),
        name="combine",
    )(p0, p1, p0, p1, x, route, mod3, fg, ys)


def _moe_plan(route, counts):
    cnt = counts[0, :NE].astype(jnp.int32)
    nsub = (cnt + SUB_MOE - 1) // SUB_MOE
    ntile = (nsub + SUBS_MOE - 1) // SUBS_MOE
    tile_end = jnp.cumsum(ntile)
    tile_start = tile_end - ntile

    def slot_of(k):
        e_k = route[:, k].astype(jnp.int32)
        first = jnp.zeros_like(e_k)
        for ex in range(NE):
            first = jnp.where(e_k == ex, tile_start[ex] * TM_MOE, first)
        return first + route[:, 2 + k].astype(jnp.int32)

    pos0, pos1 = slot_of(0), slot_of(1)
    n_used = tile_end[NE - 1]
    tiles = jnp.arange(NT_MOE, dtype=jnp.int32)
    blk = jnp.minimum(tiles, n_used - 1)
    te = jnp.sum((blk[:, None] >= tile_end[None, :NE - 1]).astype(jnp.int32), axis=1)
    ns = jnp.clip(nsub[te] - (blk - tile_start[te]) * SUBS_MOE, 0, SUBS_MOE)
    ns = jnp.where(tiles < n_used, ns, 0).astype(jnp.int32)
    tok = jnp.arange(B * SEQ, dtype=jnp.int32)
    tok_row = tok + (tok // SEQ + 1) * CTX
    src = jnp.full((R_MOE,), CTX, jnp.int32).at[jnp.concatenate([pos0, pos1])].set(
        jnp.concatenate([tok_row, tok_row]), unique_indices=True)
    first_row = jnp.arange(0, TM_MOE, TM, dtype=jnp.int32)
    live = (first_row[None, :] < ns[:, None] * SUB_MOE).astype(jnp.int32).reshape(-1)
    return pos0, pos1, te, ns, blk, src, live


def kernel(x, c, ctx, c_ctx, w_ada, b_ada, norm1_g, norm2_g, w_in, w_gate, b_gate, gla_norm_g,
           w_pool, pool_scale, w_out, ffn_w1, ffn_w3, ffn_w2, moe_w_router, moe_b_router,
           moe_w1, moe_w3, moe_w2, final_g):
    assert x.shape == (B, SEQ, D) and ctx.shape == (B, CTX, D) and DEPTH == 2

    cvec = jnp.concatenate([c, c_ctx[None, :], jnp.zeros((8 - B - 1, D), F32)], axis=0)
    mods = _ada(cvec, w_ada, b_ada)
    xs = (x, ctx)

    w_in_t = jnp.swapaxes(w_in, 1, 2)
    w_r_t = jnp.pad(w_in_t[:, IN_MAIN:, :], ((0, 0), (0, LANES - 2 * RANK), (0, 0)))
    w_out_b = w_out.astype(BF16)
    w_pool_b = w_pool.astype(BF16)

    band_np, inv_np = _pool_tables()
    band = jnp.asarray(band_np, BF16)
    inv = jnp.asarray(inv_np, F32)

    dense_te = jnp.zeros((NT // TM_DENSE,), jnp.int32)
    dense_ns = jnp.full((NT // TM_DENSE,), TM_DENSE // SUB_DENSE, jnp.int32)
    dense_blk = jnp.arange(NT // TM_DENSE, dtype=jnp.int32)

    out = None
    for l in range(DEPTH):
        mod3 = mods[l, :3].reshape(3, 1, 6 * D)
        wg = jnp.zeros((2, LANES, KEYW), F32)
        wg = wg.at[0, 0:RANK].set(w_gate[l, 0]).at[1, RANK:2 * RANK].set(w_gate[l, 1]).astype(BF16)
        bg = b_gate[l].reshape(2, 1, KEYW)

        qk, vgp, r = _inproj(xs, mod3, norm1_g[l].reshape(1, D), w_in_t, w_r_t, l)
        o_f, o_b = _gla(qk, vgp, r, wg, bg)
        mix_args = (xs, o_f, o_b, vgp, mod3, norm2_g[l].reshape(1, D), gla_norm_g[l].reshape(1, GLA_W),
                    band, inv, w_pool_b[l], pool_scale[l].reshape(1, POOL_W), w_out_b, l)
        if l % 2 == 0:
            x1, h2 = _mix(*mix_args)
            jl = l // 2
            ys = _ffn(h2, dense_te, dense_ns, dense_blk, ffn_w1[:, None], ffn_w3[:, None],
                      ffn_w2[:, None], jl, TM_DENSE, SUB_DENSE, TF_DENSE,
                      straight=(TM_DENSE // SUB_DENSE,))
            xs = (x1, ys, mod3)
        else:
            jl = l // 2
            wr = jnp.concatenate(_split2(jnp.pad(moe_w_router[jl], ((0, 0), (0, LANES - NE)))), axis=1)
            br = jnp.pad(moe_b_router[jl], (0, LANES - NE)).reshape(1, LANES)
            x1, h2, logits = _mix(*mix_args, wr=wr)
            route, counts = _router(logits, br)
            pos0, pos1, te, ns, blk, src, live = _moe_plan(route, counts)
            xg = _gather(h2, src, live)
            ys = _ffn(xg, te, ns, blk, moe_w1, moe_w3, moe_w2, jl, TM_MOE, SUB_MOE, TF_MOE,
                      straight=STRAIGHT_MOE)
            out = _combine(x1, ys, pos0, pos1, route, mod3, final_g.reshape(1, D))
    return out.reshape(B, SEQ, D)
```

```python
import functools

import numpy as np
import jax
import jax.numpy as jnp
from jax import lax
from jax.experimental import pallas as pl
from jax.experimental.pallas import tpu as pltpu

F32 = jnp.float32
BF16 = jnp.bfloat16

D = 2048
B = 2
SEQ = 4096
CTX = 256
DEPTH = 2
GRID_W = 64
EPS = 1e-6
GLA_W = 1024
POOL_W = 1024
H = 4
DV = 256
DK = 128
KEYW = H * DK
RANK = 16
TAU = 16.0
WINDOWS = (2, 4, 8, 16)
PG = 4
PGW = 256
D_FF = 5632
NE = 8

LT = CTX + SEQ
NT = B * LT
TM = 256
TPB = LT // TM
NTILES = NT // TM
LANES = 128
SLAB = D // LANES
SLAB_PITCH = SLAB + 4

CH = 64
SB = 16
RB = 256
NRB = LT // RB
assert RB == CTX and RB % CH == 0

TF_DENSE = 512
TF_MOE = 256
TM_DENSE = NT // 8
SUB_DENSE = TM_DENSE // 4
SUB_MOE = 256
SUBS_MOE = 5
STRAIGHT_MOE = (5, 4, 3, 2)
TM_MOE = SUBS_MOE * SUB_MOE
N_ASSIGN = B * SEQ * 2
NT_MOE = (N_ASSIGN // SUB_MOE + NE + (SUBS_MOE - 1) * NE) // SUBS_MOE
R_MOE = NT_MOE * TM_MOE
assert TM_MOE % TM == 0

VMEM_LIMIT = 56 * 1024 * 1024


def _cparams(sem, vmem=VMEM_LIMIT):
    return pltpu.CompilerParams(dimension_semantics=sem, vmem_limit_bytes=vmem)


def _split2(a):
    hi = a.astype(BF16)
    lo = (a - hi.astype(F32)).astype(BF16)
    return hi, lo


def _dot(a, b):
    return jnp.dot(a, b, preferred_element_type=F32)


def _dot3(a, b):
    ah, al = _split2(a)
    bh, bl = _split2(b)
    return _dot(ah, bh) + (_dot(al, bh) + _dot(ah, bl))


def _silu(a):
    return a / (1.0 + jnp.exp(-a))


def _rms(x):
    return x * lax.rsqrt(jnp.mean(x * x, axis=-1, keepdims=True) + EPS)


def _mod_row(i):
    return jnp.where(i % TPB == 0, 2, i // TPB)


ADA_TN = 2048


def _ada_kernel(c_ref, w_ref, b_ref, o_ref):
    s = _silu(c_ref[...])
    o_ref[0] = _dot3(s, w_ref[0]) + b_ref[0]


def _ada(cvec, w_ada, b_ada):
    n = 6 * D
    return pl.pallas_call(
        _ada_kernel,
        grid=(DEPTH, n // ADA_TN),
        in_specs=[
            pl.BlockSpec((8, D), lambda l, j: (0, 0)),
            pl.BlockSpec((1, D, ADA_TN), lambda l, j: (l, 0, j)),
            pl.BlockSpec((1, 1, ADA_TN), lambda l, j: (l, 0, j)),
        ],
        out_specs=pl.BlockSpec((1, 8, ADA_TN), lambda l, j: (l, 0, j)),
        out_shape=jax.ShapeDtypeStruct((DEPTH, 8, n), F32),
        compiler_params=_cparams(("parallel", "parallel")),
        name="ada",
    )(cvec, w_ada, b_ada.reshape(DEPTH, 1, n))


IN_MAIN = 2 * KEYW + 2 * GLA_W + POOL_W


def _x_specs(xs):
    if len(xs) == 2:
        return [pl.BlockSpec((None, TM, D), lambda i: (i // TPB, jnp.maximum(i % TPB - 1, 0), 0)),
                pl.BlockSpec((None, TM, D), lambda i: (i // TPB, 0, 0))]
    return [pl.BlockSpec((TM, D), lambda i: (i, 0)),
            pl.BlockSpec((TM, D), lambda i: (i, 0)),
            pl.BlockSpec((1, 1, 6 * D), lambda i: (_mod_row(i), 0, 0))]


def _x_tile(x_refs):
    if len(x_refs) == 2:
        return jnp.where(pl.program_id(0) % TPB == 0, x_refs[1][...], x_refs[0][...])
    x_ref, y_ref, mod_ref = x_refs
    return x_ref[...] + mod_ref[0][:, 5 * D:6 * D] * y_ref[...]


def _inproj_kernel(n_x, *refs):
    x_refs = refs[:n_x]
    mod_ref, g_ref, w_ref, wr_ref, qk_ref, vgp_ref, r_ref = refs[n_x:]
    mod = mod_ref[0]
    h = (_rms(_x_tile(x_refs)) * g_ref[...] * (1.0 + mod[:, D:2 * D]) + mod[:, 0:D]).astype(BF16)
    def proj(wt):
        return lax.dot_general(h, wt.astype(BF16), _NT_DIMS, preferred_element_type=F32)

    wc = 2 * KEYW
    qk_ref[...] = proj(w_ref[0:wc, :])
    for c0 in range(wc, IN_MAIN, wc):
        vgp_ref[:, c0 - wc:c0] = proj(w_ref[c0:c0 + wc, :]).astype(BF16)
    r_ref[...] = proj(wr_ref[...])


def _inproj(xs, mod3, g, w, wr, l):
    return pl.pallas_call(
        functools.partial(_inproj_kernel, len(xs)),
        grid=(NTILES,),
        in_specs=_x_specs(xs) + [
            pl.BlockSpec((1, 1, 6 * D), lambda i: (_mod_row(i), 0, 0)),
            pl.BlockSpec((1, D), lambda i: (0, 0)),
            pl.BlockSpec((None, IN_MAIN, D), lambda i: (l, 0, 0), pipeline_mode=pl.Buffered(1)),
            pl.BlockSpec((None, LANES, D), lambda i: (l, 0, 0)),
        ],
        out_specs=[
            pl.BlockSpec((TM, 2 * KEYW), lambda i: (i, 0)),
            pl.BlockSpec((TM, IN_MAIN - 2 * KEYW), lambda i: (i, 0)),
            pl.BlockSpec((TM, LANES), lambda i: (i, 0)),
        ],
        out_shape=[
            jax.ShapeDtypeStruct((NT, 2 * KEYW), F32),
            jax.ShapeDtypeStruct((NT, IN_MAIN - 2 * KEYW), BF16),
            jax.ShapeDtypeStruct((NT, LANES), F32),
        ],
        compiler_params=_cparams(("parallel",)),
        name="inproj",
    )(*xs, mod3, g, w, wr)


_NT_DIMS = (((1,), (1,)), ((), ()))
_TN_DIMS = (((0,), (0,)), ((), ()))


def _gla_gates(r_ref, wg_ref, bg_ref, rev):
    d = 1 if rev else 0
    pre = _dot(r_ref[...].astype(BF16), wg_ref[d]) + bg_ref[d]
    lg = (jnp.minimum(pre, 0.0) - jnp.log(1.0 + jnp.exp(-jnp.abs(pre)))) * (1.0 / TAU)
    row = lax.broadcasted_iota(jnp.int32, (RB, RB), 0)
    col = lax.broadcasted_iota(jnp.int32, (RB, RB), 1)
    inside = jnp.logical_and(row // CH == col // CH, (col >= row) if rev else (col <= row))
    tri = jnp.where(inside, 1.0, 0.0).astype(BF16)
    lg_hi, lg_lo = _split2(lg)
    cum = _dot(tri, lg_hi) + _dot(tri, lg_lo)
    return cum, cum - lg


def _gla_kernel(qkf_ref, qkb_ref, vf_ref, vb_ref, rf_ref, rb_ref, wg_ref, bg_ref,
                of_ref, ob_ref, stf_ref, stb_ref):
    @pl.when(pl.program_id(1) == 0)
    def _():
        stf_ref[...] = jnp.zeros_like(stf_ref)
        stb_ref[...] = jnp.zeros_like(stb_ref)

    nc = RB // CH
    dirs = ((qkf_ref, vf_ref, rf_ref, of_ref, stf_ref, False),
            (qkb_ref, vb_ref, rb_ref, ob_ref, stb_ref, True))
    rows3 = lax.broadcasted_iota(jnp.int32, (nc, CH, KEYW), 1)
    t_idx = lax.broadcasted_iota(jnp.int32, (SB, CH), 0)
    s_idx = lax.broadcasted_iota(jnp.int32, (SB, CH), 1)

    gates = [_gla_gates(r_ref, wg_ref, bg_ref, rev) for (_, _, r_ref, _, _, rev) in dirs]

    scores = {}
    for blk in range(CH // SB):
        r0 = blk * SB
        for d, (qk_ref, _, _, _, _, rev) in enumerate(dirs):
            cum, cex = gates[d]
            c3 = cum.reshape(nc, CH, KEYW)
            ce3 = cex.reshape(nc, CH, KEYW)
            q3 = (qk_ref[:, 0:KEYW] * (DK ** -0.5)).reshape(nc, CH, KEYW)
            k3 = qk_ref[:, KEYW:2 * KEYW].reshape(nc, CH, KEYW)
            if rev:
                a = ce3[:, r0 + SB - 1:r0 + SB, :]
                live = rows3 >= r0
                keep = s_idx >= t_idx + r0
            else:
                a = ce3[:, r0:r0 + 1, :]
                live = rows3 < r0 + SB
                keep = s_idx <= t_idx + r0
            qb = (q3[:, r0:r0 + SB, :] * jnp.exp(c3[:, r0:r0 + SB, :] - a)).astype(BF16)
            kb = (k3 * jnp.exp(jnp.where(live, a - c3, 0.0))).astype(BF16)
            for ci in range(nc):
                for h in range(H):
                    hk = slice(h * DK, (h + 1) * DK)
                    s = lax.dot_general(qb[ci][:, hk], kb[ci][:, hk], _NT_DIMS,
                                        preferred_element_type=F32)
                    scores.setdefault((d, ci, h), []).append(jnp.where(keep, s, 0.0))

    for d, (_, v_ref, _, o_ref, _, _) in enumerate(dirs):
        for ci in range(nc):
            rs = slice(ci * CH, (ci + 1) * CH)
            for h in range(H):
                hv = slice(h * DV, (h + 1) * DV)
                sc = jnp.concatenate(scores[(d, ci, h)], axis=0).astype(BF16)
                o_ref[rs, hv] = _dot(sc, v_ref[rs, hv])

    for d, (qk_ref, v_ref, _, o_ref, st_ref, rev) in enumerate(dirs):
        cum = gates[d][0]
        qe = (qk_ref[:, 0:KEYW] * (DK ** -0.5) * jnp.exp(cum)).astype(BF16)
        for h in range(H):
            hk = slice(h * DK, (h + 1) * DK)
            hv = slice(h * DV, (h + 1) * DV)
            st = st_ref[h]
            for ci in (reversed(range(nc)) if rev else range(nc)):
                rs = slice(ci * CH, (ci + 1) * CH)
                c = cum[rs, hk]
                o_ref[rs, hv] += lax.dot_general(qe[rs, hk], st.astype(BF16), _NT_DIMS,
                                                 preferred_element_type=F32)
                tot = c[0:1, :] if rev else c[CH - 1:CH, :]
                kd = (qk_ref[rs, KEYW + h * DK:KEYW + (h + 1) * DK] * jnp.exp(tot - c)).astype(BF16)
                st = st * jnp.exp(tot) + lax.dot_general(v_ref[rs, hv], kd, _TN_DIMS,
                                                         preferred_element_type=F32)
            st_ref[h] = st


def _bwd_block(j):
    return jnp.where(j == 0, 0, NRB - j)


def _gla(qk, vgp, r, wg, bg):
    qk3 = qk.reshape(B, LT, 2 * KEYW)
    vgp3 = vgp.reshape(B, LT, IN_MAIN - 2 * KEYW)
    r3 = r.reshape(B, LT, LANES)
    fwd = lambda b, j: (b, j, 0)
    bwd = lambda b, j: (b, _bwd_block(j), 0)
    o_f, o_b = pl.pallas_call(
        _gla_kernel,
        grid=(B, NRB),
        in_specs=[
            pl.BlockSpec((None, RB, 2 * KEYW), fwd),
            pl.BlockSpec((None, RB, 2 * KEYW), bwd),
            pl.BlockSpec((None, RB, GLA_W), fwd),
            pl.BlockSpec((None, RB, GLA_W), bwd),
            pl.BlockSpec((None, RB, LANES), fwd),
            pl.BlockSpec((None, RB, LANES), bwd),
            pl.BlockSpec((2, LANES, KEYW), lambda b, j: (0, 0, 0)),
            pl.BlockSpec((2, 1, KEYW), lambda b, j: (0, 0, 0)),
        ],
        out_specs=[
            pl.BlockSpec((None, RB, GLA_W), fwd),
            pl.BlockSpec((None, RB, GLA_W), bwd),
        ],
        out_shape=[jax.ShapeDtypeStruct((B, LT, GLA_W), F32)] * 2,
        scratch_shapes=[pltpu.VMEM((H, DV, DK), F32), pltpu.VMEM((H, DV, DK), F32)],
        compiler_params=_cparams(("parallel", "arbitrary")),
        name="gla",
    )(qk3, qk3, vgp3, vgp3, r3, r3, wg, bg)
    return o_f.reshape(NT, GLA_W), o_b.reshape(NT, GLA_W)


def _pool_tables():
    band = np.zeros((2, PG, TM, TM), np.float32)
    inv = np.zeros((2, PG, TM, PGW), np.float32)
    for kind, row_len in enumerate((GRID_W, CTX)):
        for gi, w in enumerate(WINDOWS):
            for n in range(TM):
                base, j = (n // row_len) * row_len, n % row_len
                lo = min(max(j - w // 2, 0), row_len)
                hi = min(max(j - w // 2 + w, 0), row_len)
                band[kind, gi, n, base + lo:base + hi] = 1.0
                inv[kind, gi, n, :] = 1.0 / (hi - lo)
    return band, inv


def _mix_kernel(with_router, n_x, *refs):
    x_refs = refs[:n_x]
    (of_ref, ob_ref, g_ref, p_ref, mod_ref, n2_ref, gain_ref,
     band_ref, inv_ref, wp_ref, ps_ref, wo_ref) = refs[n_x:n_x + 12]
    rest = refs[n_x + 12:]
    if with_router:
        wr_ref, xo_ref, h2_ref, lg_ref = rest
    else:
        xo_ref, h2_ref = rest
    mod = mod_ref[0]
    o = of_ref[...] + ob_ref[...]
    on = jnp.concatenate([_rms(o[:, h * DV:(h + 1) * DV]) for h in range(H)], axis=1)
    read = (on * gain_ref[...] * _silu(g_ref[...].astype(F32))).astype(BF16)
    y = _dot(read, wo_ref[0:GLA_W, :])
    pouts = []
    for gi in range(PG):
        pg = p_ref[:, gi * PGW:(gi + 1) * PGW]
        mixed = _dot(band_ref[0, gi], pg) * inv_ref[0, gi] - pg.astype(F32)
        pouts.append(_dot(mixed.astype(BF16), wp_ref[gi]))
    pool = (jnp.concatenate(pouts, axis=1) * ps_ref[...]).astype(BF16)
    y = y + _dot(pool, wo_ref[GLA_W:, :])
    xn = _x_tile(x_refs) + mod[:, 2 * D:3 * D] * y
    xo_ref[...] = xn
    h2 = _rms(xn) * n2_ref[...] * (1.0 + mod[:, 4 * D:5 * D]) + mod[:, 3 * D:4 * D]
    if with_router:
        for cb in range(SLAB):
            h2_ref[pl.ds(cb, TM, stride=SLAB), :] = h2[:, cb * LANES:(cb + 1) * LANES]
        h_hi, h_lo = _split2(h2)
        both = _dot(h_hi, wr_ref[...])
        lg_ref[...] = both[:, :LANES] + (both[:, LANES:] + _dot(h_lo, wr_ref[:, :LANES]))
    else:
        h2_ref[...] = h2.astype(BF16)


def _mix(xs, o_f, o_b, vgp, mod3, n2, gain, band, inv, wp, ps, wo, l, wr=None):
    with_router = wr is not None
    kind = lambda i: (jnp.where(i % TPB == 0, 1, 0), 0, 0, 0)
    const2 = lambda i: (0, 0)
    in_specs = _x_specs(xs) + [
        pl.BlockSpec((TM, GLA_W), lambda i: (i, 0)),
        pl.BlockSpec((TM, GLA_W), lambda i: (i, 0)),
        pl.BlockSpec((TM, GLA_W), lambda i: (i, 1)),
        pl.BlockSpec((TM, POOL_W), lambda i: (i, 2)),
        pl.BlockSpec((1, 1, 6 * D), lambda i: (_mod_row(i), 0, 0)),
        pl.BlockSpec((1, D), const2),
        pl.BlockSpec((1, GLA_W), const2),
        pl.BlockSpec((1, PG, TM, TM), kind),
        pl.BlockSpec((1, PG, TM, PGW), kind),
        pl.BlockSpec((PG, PGW, PGW), lambda i: (0, 0, 0)),
        pl.BlockSpec((1, POOL_W), const2),
        pl.BlockSpec((None, D, D), lambda i: (l, 0, 0)),
    ]
    args = [*xs, o_f, o_b, vgp, vgp, mod3, n2, gain, band, inv, wp, ps, wo]
    if with_router:
        h2_spec = pl.BlockSpec((TM * SLAB, LANES), lambda i: (i, 0))
        h2_shape = jax.ShapeDtypeStruct((NT * SLAB, LANES), F32)
    else:
        h2_spec = pl.BlockSpec((TM, D), lambda i: (i, 0))
        h2_shape = jax.ShapeDtypeStruct((NT, D), BF16)
    out_specs = [pl.BlockSpec((TM, D), lambda i: (i, 0)), h2_spec]
    out_shape = [jax.ShapeDtypeStruct((NT, D), F32), h2_shape]
    if with_router:
        in_specs.append(pl.BlockSpec((D, 2 * LANES), const2))
        args.append(wr)
        out_specs.append(pl.BlockSpec((TM, LANES), lambda i: (i, 0)))
        out_shape.append(jax.ShapeDtypeStruct((NT, LANES), F32))
    return pl.pallas_call(
        functools.partial(_mix_kernel, with_router, len(xs)),
        grid=(NTILES,),
        in_specs=in_specs,
        out_specs=out_specs,
        out_shape=out_shape,
        compiler_params=_cparams(("parallel",)),
        name="mix_router" if with_router else "mix",
    )(*args)


def _ffn_kernel(sub, straight, te_ref, ns_ref, blk_ref, xs_ref, w1_ref, w3_ref, w2_ref, o_ref):
    i = pl.program_id(0)
    j = pl.program_id(1)
    nsub = ns_ref[i]

    @pl.when(j == 0)
    def _():
        o_ref[...] = jnp.zeros_like(o_ref)

    def rows_update(rows):
        xt = xs_ref[rows, :]
        h1 = _dot(xt, w1_ref[0, 0].astype(BF16))
        h3 = _dot(xt, w3_ref[0, 0].astype(BF16))
        act = (_silu(h1) * h3).astype(BF16)
        o_ref[rows, :] += _dot(act, w2_ref[0, 0].astype(BF16))

    is_straight = False
    for n in straight:
        is_straight = jnp.logical_or(is_straight, nsub == n)

        @pl.when(nsub == n)
        def _(n=n):
            rows_update(pl.ds(0, n * sub))

    @pl.when(jnp.logical_and(nsub > 0, jnp.logical_not(is_straight)))
    def _():
        def body(s, carry):
            rows_update(pl.ds(pl.multiple_of(s * sub, sub), sub))
            return carry

        lax.fori_loop(0, nsub, body, 0)


def _ffn(xs, te, ns, blk, w1, w3, w2, wl, tm, sub, tf, straight):
    n_tiles = xs.shape[0] // tm
    nj = D_FF // tf
    out_spec = pl.BlockSpec((tm, D), lambda i, j, te, ns, blk: (i, 0))
    out_shape = jax.ShapeDtypeStruct((xs.shape[0], D), F32)
    live_j = lambda i, j, te, ns, blk: jnp.where(ns[i] > 0, j, nj - 1)
    grid_spec = pltpu.PrefetchScalarGridSpec(
        num_scalar_prefetch=3,
        grid=(n_tiles, nj),
        in_specs=[
            pl.BlockSpec((tm, D), lambda i, j, te, ns, blk: (blk[i], 0)),
            pl.BlockSpec((1, 1, D, tf), lambda i, j, te, ns, blk: (wl, te[i], 0, live_j(i, j, te, ns, blk))),
            pl.BlockSpec((1, 1, D, tf), lambda i, j, te, ns, blk: (wl, te[i], 0, live_j(i, j, te, ns, blk))),
            pl.BlockSpec((1, 1, tf, D), lambda i, j, te, ns, blk: (wl, te[i], live_j(i, j, te, ns, blk), 0)),
        ],
        out_specs=out_spec,
    )
    return pl.pallas_call(
        functools.partial(_ffn_kernel, sub, straight),
        grid_spec=grid_spec,
        out_shape=out_shape,
        compiler_params=_cparams(("arbitrary", "arbitrary")),
        name="ffn_%d" % tm,
    )(te, ns, blk, xs, w1, w3, w2)


def _lat_tile(t):
    return t + t // (TPB - 1) + 1


def _router_kernel(lg_ref, br_ref, route_ref, cnt_ref, carry_ref):
    @pl.when(pl.program_id(0) == 0)
    def _():
        carry_ref[...] = jnp.zeros_like(carry_ref)

    lane = lax.broadcasted_iota(jnp.int32, (TM, LANES), 1)
    z = jnp.where(lane < NE, lg_ref[...] + br_ref[...], -jnp.inf)
    m0 = jnp.max(z, axis=-1, keepdims=True)
    lane_f = lane.astype(F32)
    e0 = jnp.min(jnp.where(z == m0, lane_f, float(LANES)), axis=-1, keepdims=True)
    z1 = jnp.where(lane_f == e0, -jnp.inf, z)
    m1 = jnp.max(z1, axis=-1, keepdims=True)
    e1 = jnp.min(jnp.where(z1 == m1, lane_f, float(LANES)), axis=-1, keepdims=True)
    t = jnp.exp(m1 - m0)
    p0 = 1.0 / (1.0 + t)
    p1 = t / (1.0 + t)
    oh0 = lane_f == e0
    oh1 = lane_f == e1
    oh = jnp.where(jnp.logical_or(oh0, oh1), 1.0, 0.0)
    row = lax.broadcasted_iota(jnp.int32, (TM, TM), 0)
    col = lax.broadcasted_iota(jnp.int32, (TM, TM), 1)
    before = jnp.where(col < row, 1.0, 0.0).astype(BF16)
    excl = _dot(before, oh.astype(BF16)) + carry_ref[0:1, :]
    rank0 = jnp.sum(jnp.where(oh0, excl, 0.0), axis=-1, keepdims=True)
    rank1 = jnp.sum(jnp.where(oh1, excl, 0.0), axis=-1, keepdims=True)
    total = carry_ref[0:1, :] + jnp.sum(oh, axis=0, keepdims=True)
    carry_ref[...] = jnp.broadcast_to(total, carry_ref.shape)
    cnt_ref[...] = jnp.broadcast_to(total, cnt_ref.shape)
    route = jnp.where(lane == 0, e0, 0.0)
    route = jnp.where(lane == 1, e1, route)
    route = jnp.where(lane == 2, rank0, route)
    route = jnp.where(lane == 3, rank1, route)
    route = jnp.where(lane == 4, p0, route)
    route = jnp.where(lane == 5, p1, route)
    route_ref[...] = route


def _router(logits, b_r):
    n_lat_tiles = B * SEQ // TM
    return pl.pallas_call(
        _router_kernel,
        grid=(n_lat_tiles,),
        in_specs=[
            pl.BlockSpec((TM, LANES), lambda t: (_lat_tile(t), 0)),
            pl.BlockSpec((1, LANES), lambda t: (0, 0)),
        ],
        out_specs=[
            pl.BlockSpec((TM, LANES), lambda t: (t, 0)),
            pl.BlockSpec((8, LANES), lambda t: (0, 0)),
        ],
        out_shape=[jax.ShapeDtypeStruct((B * SEQ, LANES), F32),
                   jax.ShapeDtypeStruct((8, LANES), F32)],
        scratch_shapes=[pltpu.VMEM((8, LANES), F32)],
        compiler_params=_cparams(("arbitrary",)),
        name="router",
    )(logits, b_r)


ISSUE_UNROLL = 8


def _slab_copy(src_hbm, tok, dst, r, sem):
    return pltpu.make_async_copy(src_hbm.at[pl.ds(pl.multiple_of(tok * SLAB, SLAB), SLAB), :],
                                 dst.at[pl.ds(r * SLAB_PITCH, SLAB), :], sem)


def _slab_wait_all(src_hbm, dst, sem):
    pltpu.make_async_copy(src_hbm.at[pl.ds(0, TM * SLAB), :], dst.at[pl.ds(0, TM * SLAB), :], sem).wait()


def _gather_kernel(live_ref, idx_ref, idx_next_ref, h_hbm, o_ref, buf_ref, sems):
    i = pl.program_id(0)
    n = pl.num_programs(0)
    live = live_ref[i]
    nxt = jnp.minimum(i + 1, n - 1)
    start_next = jnp.logical_and(i + 1 < n, live_ref[nxt] != 0)

    def issue(ids_ref, s):
        def body(g, carry):
            for u in range(ISSUE_UNROLL):
                r = g * ISSUE_UNROLL + u
                _slab_copy(h_hbm, ids_ref[0, 0, r], buf_ref.at[s], r, sems.at[s]).start(priority=u % 2)
            return carry

        lax.fori_loop(0, TM // ISSUE_UNROLL, body, 0)

    def finish(s):
        _slab_wait_all(h_hbm, buf_ref.at[s], sems.at[s])
        for cb in range(SLAB):
            o_ref[:, cb * LANES:(cb + 1) * LANES] = (
                buf_ref[s, pl.ds(cb, TM, stride=SLAB_PITCH), :].astype(BF16))

    @pl.when(jnp.logical_and(i == 0, live != 0))
    def _():
        issue(idx_ref, 0)

    for s in range(2):
        @pl.when(jnp.logical_and(start_next, (i + 1) % 2 == s))
        def _(s=s):
            issue(idx_next_ref, s)

    @pl.when(live == 0)
    def _():
        o_ref[...] = jnp.zeros_like(o_ref)

    for s in range(2):
        @pl.when(jnp.logical_and(live != 0, i % 2 == s))
        def _(s=s):
            finish(s)


def _gather(h2_slabs, src, live):
    n = src.shape[0] // TM
    ids = src.reshape(n, 1, TM)
    grid_spec = pltpu.PrefetchScalarGridSpec(
        num_scalar_prefetch=1,
        grid=(n,),
        in_specs=[
            pl.BlockSpec((1, 1, TM), lambda i, live: (i, 0, 0), memory_space=pltpu.SMEM),
            pl.BlockSpec((1, 1, TM), lambda i, live: (jnp.minimum(i + 1, n - 1), 0, 0),
                         memory_space=pltpu.SMEM),
            pl.BlockSpec(memory_space=pl.ANY),
        ],
        out_specs=pl.BlockSpec((TM, D), lambda i, live: (i, 0)),
        scratch_shapes=[pltpu.VMEM((2, TM * SLAB_PITCH, LANES), F32), pltpu.SemaphoreType.DMA((2,))],
    )
    return pl.pallas_call(
        _gather_kernel,
        grid_spec=grid_spec,
        out_shape=jax.ShapeDtypeStruct((src.shape[0], D), BF16),
        compiler_params=_cparams(("arbitrary",)),
        name="dispatch",
    )(live, ids, ids, h2_slabs)


def _combine_kernel(p0_ref, p1_ref, p0n_ref, p1n_ref, x_ref, route_ref, mod_ref, fg_ref, y_hbm, o_ref,
                    buf_ref, sems):
    t = pl.program_id(0)
    n = pl.num_programs(0)

    def issue(q0_ref, q1_ref, s):
        def body(g, carry):
            for u in range(ISSUE_UNROLL):
                r = g * ISSUE_UNROLL + u
                for k, q_ref in enumerate((q0_ref, q1_ref)):
                    pltpu.make_async_copy(y_hbm.at[pl.ds(q_ref[0, 0, r], 1), :],
                                          buf_ref.at[s, k, pl.ds(r, 1), :], sems.at[s, k]).start(priority=k)
            return carry

        lax.fori_loop(0, TM // ISSUE_UNROLL, body, 0)

    def finish(s):
        for k in range(2):
            pltpu.make_async_copy(y_hbm.at[pl.ds(0, TM), :], buf_ref.at[s, k], sems.at[s, k]).wait()
        route = route_ref[...]
        y = route[:, 4:5] * buf_ref[s, 0] + route[:, 5:6] * buf_ref[s, 1]
        xn = x_ref[...] + mod_ref[0][:, 5 * D:6 * D] * y
        o_ref[...] = _rms(xn) * fg_ref[...]

    @pl.when(t == 0)
    def _():
        issue(p0_ref, p1_ref, 0)

    for s in range(2):
        @pl.when(jnp.logical_and(t + 1 < n, (t + 1) % 2 == s))
        def _(s=s):
            issue(p0n_ref, p1n_ref, s)

    for s in range(2):
        @pl.when(t % 2 == s)
        def _(s=s):
            finish(s)


def _combine(x, ys, pos0, pos1, route, mod3, fg):
    n = B * SEQ // TM
    smem = lambda: pl.BlockSpec((1, 1, TM), lambda t: (t, 0, 0), memory_space=pltpu.SMEM)
    smem_next = lambda: pl.BlockSpec((1, 1, TM), lambda t: (jnp.minimum(t + 1, n - 1), 0, 0),
                                     memory_space=pltpu.SMEM)
    p0 = pos0.reshape(n, 1, TM)
    p1 = pos1.reshape(n, 1, TM)
    return pl.pallas_call(
        _combine_kernel,
        grid=(n,),
        in_specs=[
            smem(), smem(), smem_next(), smem_next(),
            pl.BlockSpec((TM, D), lambda t: (_lat_tile(t), 0)),
            pl.BlockSpec((TM, LANES), lambda t: (t, 0)),
            pl.BlockSpec((1, 1, 6 * D), lambda t: (t // (TPB - 1), 0, 0)),
            pl.BlockSpec((1, D), lambda t: (0, 0)),
            pl.BlockSpec(memory_space=pl.ANY),
        ],
        out_specs=pl.BlockSpec((TM, D), lambda t: (t, 0)),
        out_shape=jax.ShapeDtypeStruct((B * SEQ, D), F32),
        scratch_shapes=[pltpu.VMEM((2, 2, TM, D), F32), pltpu.SemaphoreType.DMA((2, 2))],
        compiler_params=_cparams(("arbitrary",)),
        name="combine",
    )(p0, p1, p0, p1, x, route, mod3, fg, ys)


def _moe_plan(route, counts):
    cnt = counts[0, :NE].astype(jnp.int32)
    nsub = (cnt + SUB_MOE - 1) // SUB_MOE
    ntile = (nsub + SUBS_MOE - 1) // SUBS_MOE
    tile_end = jnp.cumsum(ntile)
    tile_start = tile_end - ntile

    def slot_of(k):
        e_k = route[:, k].astype(jnp.int32)
        first = jnp.zeros_like(e_k)
        for ex in range(NE):
            first = jnp.where(e_k == ex, tile_start[ex] * TM_MOE, first)
        return first + route[:, 2 + k].astype(jnp.int32)

    pos0, pos1 = slot_of(0), slot_of(1)
    n_used = tile_end[NE - 1]
    tiles = jnp.arange(NT_MOE, dtype=jnp.int32)
    blk = jnp.minimum(tiles, n_used - 1)
    te = jnp.sum((blk[:, None] >= tile_end[None, :NE - 1]).astype(jnp.int32), axis=1)
    ns = jnp.clip(nsub[te] - (blk - tile_start[te]) * SUBS_MOE, 0, SUBS_MOE)
    ns = jnp.where(tiles < n_used, ns, 0).astype(jnp.int32)
    tok = jnp.arange(B * SEQ, dtype=jnp.int32)
    tok_row = tok + (tok // SEQ + 1) * CTX
    src = jnp.full((R_MOE,), CTX, jnp.int32).at[jnp.concatenate([pos0, pos1])].set(
        jnp.concatenate([tok_row, tok_row]), unique_indices=True)
    first_row = jnp.arange(0, TM_MOE, TM, dtype=jnp.int32)
    live = (first_row[None, :] < ns[:, None] * SUB_MOE).astype(jnp.int32).reshape(-1)
    return pos0, pos1, te, ns, blk, src, live


def kernel(x, c, ctx, c_ctx, w_ada, b_ada, norm1_g, norm2_g, w_in, w_gate, b_gate, gla_norm_g,
           w_pool, pool_scale, w_out, ffn_w1, ffn_w3, ffn_w2, moe_w_router, moe_b_router,
           moe_w1, moe_w3, moe_w2, final_g):
    assert x.shape == (B, SEQ, D) and ctx.shape == (B, CTX, D) and DEPTH == 2

    cvec = jnp.concatenate([c, c_ctx[None, :], jnp.zeros((8 - B - 1, D), F32)], axis=0)
    mods = _ada(cvec, w_ada, b_ada)
    xs = (x, ctx)

    w_in_t = jnp.swapaxes(w_in, 1, 2)
    w_r_t = jnp.pad(w_in_t[:, IN_MAIN:, :], ((0, 0), (0, LANES - 2 * RANK), (0, 0)))
    w_out_b = w_out.astype(BF16)
    w_pool_b = w_pool.astype(BF16)

    band_np, inv_np = _pool_tables()
    band = jnp.asarray(band_np, BF16)
    inv = jnp.asarray(inv_np, F32)

    dense_te = jnp.zeros((NT // TM_DENSE,), jnp.int32)
    dense_ns = jnp.full((NT // TM_DENSE,), TM_DENSE // SUB_DENSE, jnp.int32)
    dense_blk = jnp.arange(NT // TM_DENSE, dtype=jnp.int32)

    out = None
    for l in range(DEPTH):
        mod3 = mods[l, :3].reshape(3, 1, 6 * D)
        wg = jnp.zeros((2, LANES, KEYW), F32)
        wg = wg.at[0, 0:RANK].set(w_gate[l, 0]).at[1, RANK:2 * RANK].set(w_gate[l, 1]).astype(BF16)
        bg = b_gate[l].reshape(2, 1, KEYW)

        qk, vgp, r = _inproj(xs, mod3, norm1_g[l].reshape(1, D), w_in_t, w_r_t, l)
        o_f, o_b = _gla(qk, vgp, r, wg, bg)
        mix_args = (xs, o_f, o_b, vgp, mod3, norm2_g[l].reshape(1, D), gla_norm_g[l].reshape(1, GLA_W),
                    band, inv, w_pool_b[l], pool_scale[l].reshape(1, POOL_W), w_out_b, l)
        if l % 2 == 0:
            x1, h2 = _mix(*mix_args)
            jl = l // 2
            ys = _ffn(h2, dense_te, dense_ns, dense_blk, ffn_w1[:, None], ffn_w3[:, None],
                      ffn_w2[:, None], jl, TM_DENSE, SUB_DENSE, TF_DENSE,
                      straight=(TM_DENSE // SUB_DENSE,))
            xs = (x1, ys, mod3)
        else:
            jl = l // 2
            wr = jnp.concatenate(_split2(jnp.pad(moe_w_router[jl], ((0, 0), (0, LANES - NE)))), axis=1)
            br = jnp.pad(moe_b_router[jl], (0, LANES - NE)).reshape(1, LANES)
            x1, h2, logits = _mix(*mix_args, wr=wr)
            route, counts = _router(logits, br)
            pos0, pos1, te, ns, blk, src, live = _moe_plan(route, counts)
            xg = _gather(h2, src, live)
            ys = _ffn(xg, te, ns, blk, moe_w1, moe_w3, moe_w2, jl, TM_MOE, SUB_MOE, TF_MOE,
                      straight=STRAIGHT_MOE)
            out = _combine(x1, ys, pos0, pos1, route, mod3, final_g.reshape(1, D))
    return out.reshape(B, SEQ, D)
```

```python
import functools

import numpy as np
import jax
import jax.numpy as jnp
from jax import lax
from jax.experimental import pallas as pl
from jax.experimental.pallas import tpu as pltpu

F32 = jnp.float32
BF16 = jnp.bfloat16

D = 2048
B = 2
SEQ = 4096
CTX = 256
DEPTH = 2
GRID_W = 64
EPS = 1e-6
GLA_W = 1024
POOL_W = 1024
H = 4
DV = 256
DK = 128
KEYW = H * DK
RANK = 16
TAU = 16.0
WINDOWS = (2, 4, 8, 16)
PG = 4
PGW = 256
D_FF = 5632
NE = 8

LT = CTX + SEQ
NT = B * LT
TM = 256
TPB = LT // TM
NTILES = NT // TM
LANES = 128
SLAB = D // LANES
SLAB_PITCH = SLAB + 4

CH = 64
SB = 16
RB = 256
NRB = LT // RB
assert RB == CTX and RB % CH == 0

TF_DENSE = 512
TF_MOE = 256
TM_DENSE = NT // 8
SUB_DENSE = TM_DENSE // 4
SUB_MOE = 256
SUBS_MOE = 5
STRAIGHT_MOE = (5, 4, 3, 2)
TM_MOE = SUBS_MOE * SUB_MOE
N_ASSIGN = B * SEQ * 2
NT_MOE = (N_ASSIGN // SUB_MOE + NE + (SUBS_MOE - 1) * NE) // SUBS_MOE
R_MOE = NT_MOE * TM_MOE
assert TM_MOE % TM == 0

VMEM_LIMIT = 56 * 1024 * 1024


def _cparams(sem, vmem=VMEM_LIMIT):
    return pltpu.CompilerParams(dimension_semantics=sem, vmem_limit_bytes=vmem)


def _split2(a):
    hi = a.astype(BF16)
    lo = (a - hi.astype(F32)).astype(BF16)
    return hi, lo


def _dot(a, b):
    return jnp.dot(a, b, preferred_element_type=F32)


def _dot3(a, b):
    ah, al = _split2(a)
    bh, bl = _split2(b)
    return _dot(ah, bh) + (_dot(al, bh) + _dot(ah, bl))


def _silu(a):
    return a / (1.0 + jnp.exp(-a))


def _rms(x):
    return x * lax.rsqrt(jnp.mean(x * x, axis=-1, keepdims=True) + EPS)


def _mod_row(i):
    return jnp.where(i % TPB == 0, 2, i // TPB)


ADA_TN = 2048


def _ada_kernel(c_ref, w_ref, b_ref, o_ref):
    s = _silu(c_ref[...])
    o_ref[0] = _dot3(s, w_ref[0]) + b_ref[0]


def _ada(cvec, w_ada, b_ada):
    n = 6 * D
    return pl.pallas_call(
        _ada_kernel,
        grid=(DEPTH, n // ADA_TN),
        in_specs=[
            pl.BlockSpec((8, D), lambda l, j: (0, 0)),
            pl.BlockSpec((1, D, ADA_TN), lambda l, j: (l, 0, j)),
            pl.BlockSpec((1, 1, ADA_TN), lambda l, j: (l, 0, j)),
        ],
        out_specs=pl.BlockSpec((1, 8, ADA_TN), lambda l, j: (l, 0, j)),
        out_shape=jax.ShapeDtypeStruct((DEPTH, 8, n), F32),
        compiler_params=_cparams(("parallel", "parallel")),
        name="ada",
    )(cvec, w_ada, b_ada.reshape(DEPTH, 1, n))


IN_MAIN = 2 * KEYW + 2 * GLA_W + POOL_W


def _x_specs(xs):
    if len(xs) == 2:
        return [pl.BlockSpec((None, TM, D), lambda i: (i // TPB, jnp.maximum(i % TPB - 1, 0), 0)),
                pl.BlockSpec((None, TM, D), lambda i: (i // TPB, 0, 0))]
    return [pl.BlockSpec((TM, D), lambda i: (i, 0)),
            pl.BlockSpec((TM, D), lambda i: (i, 0)),
            pl.BlockSpec((1, 1, 6 * D), lambda i: (_mod_row(i), 0, 0))]


def _x_tile(x_refs):
    if len(x_refs) == 2:
        return jnp.where(pl.program_id(0) % TPB == 0, x_refs[1][...], x_refs[0][...])
    x_ref, y_ref, mod_ref = x_refs
    return x_ref[...] + mod_ref[0][:, 5 * D:6 * D] * y_ref[...]


def _inproj_kernel(n_x, *refs):
    x_refs = refs[:n_x]
    mod_ref, g_ref, w_ref, wr_ref, qk_ref, vgp_ref, r_ref = refs[n_x:]
    mod = mod_ref[0]
    h = (_rms(_x_tile(x_refs)) * g_ref[...] * (1.0 + mod[:, D:2 * D]) + mod[:, 0:D]).astype(BF16)
    def proj(wt):
        return lax.dot_general(h, wt.astype(BF16), _NT_DIMS, preferred_element_type=F32)

    wc = 2 * KEYW
    qk_ref[...] = proj(w_ref[0:wc, :])
    for c0 in range(wc, IN_MAIN, wc):
        vgp_ref[:, c0 - wc:c0] = proj(w_ref[c0:c0 + wc, :]).astype(BF16)
    r_ref[...] = proj(wr_ref[...])


def _inproj(xs, mod3, g, w, wr, l):
    return pl.pallas_call(
        functools.partial(_inproj_kernel, len(xs)),
        grid=(NTILES,),
        in_specs=_x_specs(xs) + [
            pl.BlockSpec((1, 1, 6 * D), lambda i: (_mod_row(i), 0, 0)),
            pl.BlockSpec((1, D), lambda i: (0, 0)),
            pl.BlockSpec((None, IN_MAIN, D), lambda i: (l, 0, 0), pipeline_mode=pl.Buffered(1)),
            pl.BlockSpec((None, LANES, D), lambda i: (l, 0, 0)),
        ],
        out_specs=[
            pl.BlockSpec((TM, 2 * KEYW), lambda i: (i, 0)),
            pl.BlockSpec((TM, IN_MAIN - 2 * KEYW), lambda i: (i, 0)),
            pl.BlockSpec((TM, LANES), lambda i: (i, 0)),
        ],
        out_shape=[
            jax.ShapeDtypeStruct((NT, 2 * KEYW), F32),
            jax.ShapeDtypeStruct((NT, IN_MAIN - 2 * KEYW), BF16),
            jax.ShapeDtypeStruct((NT, LANES), F32),
        ],
        compiler_params=_cparams(("parallel",)),
        name="inproj",
    )(*xs, mod3, g, w, wr)


_NT_DIMS = (((1,), (1,)), ((), ()))
_TN_DIMS = (((0,), (0,)), ((), ()))


def _gla_gates(r_ref, wg_ref, bg_ref, rev):
    d = 1 if rev else 0
    pre = _dot(r_ref[...].astype(BF16), wg_ref[d]) + bg_ref[d]
    lg = (jnp.minimum(pre, 0.0) - jnp.log(1.0 + jnp.exp(-jnp.abs(pre)))) * (1.0 / TAU)
    row = lax.broadcasted_iota(jnp.int32, (RB, RB), 0)
    col = lax.broadcasted_iota(jnp.int32, (RB, RB), 1)
    inside = jnp.logical_and(row // CH == col // CH, (col >= row) if rev else (col <= row))
    tri = jnp.where(inside, 1.0, 0.0).astype(BF16)
    lg_hi, lg_lo = _split2(lg)
    cum = _dot(tri, lg_hi) + _dot(tri, lg_lo)
    return cum, cum - lg


def _gla_kernel(qkf_ref, qkb_ref, vf_ref, vb_ref, rf_ref, rb_ref, wg_ref, bg_ref,
                of_ref, ob_ref, stf_ref, stb_ref):
    @pl.when(pl.program_id(1) == 0)
    def _():
        stf_ref[...] = jnp.zeros_like(stf_ref)
        stb_ref[...] = jnp.zeros_like(stb_ref)

    nc = RB // CH
    dirs = ((qkf_ref, vf_ref, rf_ref, of_ref, stf_ref, False),
            (qkb_ref, vb_ref, rb_ref, ob_ref, stb_ref, True))
    t_idx = lax.broadcasted_iota(jnp.int32, (SB, CH), 0)
    s_idx = lax.broadcasted_iota(jnp.int32, (SB, CH), 1)

    gates = [_gla_gates(r_ref, wg_ref, bg_ref, rev) for (_, _, r_ref, _, _, rev) in dirs]

    scores = {}

    def score_round(blk):
        r0 = blk * SB
        for d, (qk_ref, _, _, _, _, rev) in enumerate(dirs):
            cum, cex = gates[d]
            c3 = cum.reshape(nc, CH, KEYW)
            ce3 = cex.reshape(nc, CH, KEYW)
            q3 = (qk_ref[:, 0:KEYW] * (DK ** -0.5)).reshape(nc, CH, KEYW)
            k3 = qk_ref[:, KEYW:2 * KEYW].reshape(nc, CH, KEYW)
            if rev:
                a = ce3[:, r0 + SB - 1:r0 + SB, :]
                lo, hi = r0, CH
                keep = s_idx >= t_idx + r0
            else:
                a = ce3[:, r0:r0 + 1, :]
                lo, hi = 0, r0 + SB
                keep = s_idx <= t_idx + r0
            qb = (q3[:, r0:r0 + SB, :] * jnp.exp(c3[:, r0:r0 + SB, :] - a)).astype(BF16)
            parts = [k3[:, lo:hi, :] * jnp.exp(a - c3[:, lo:hi, :])]
            if lo > 0:
                parts.insert(0, k3[:, 0:lo, :])
            if hi < CH:
                parts.append(k3[:, hi:CH, :])
            kb = jnp.concatenate(parts, axis=1).astype(BF16)
            for ci in range(nc):
                for h in range(H):
                    hk = slice(h * DK, (h + 1) * DK)
                    s = lax.dot_general(qb[ci][:, hk], kb[ci][:, hk], _NT_DIMS,
                                        preferred_element_type=F32)
                    scores.setdefault((d, ci, h), []).append(jnp.where(keep, s, 0.0))

    qes = [(qk_ref[:, 0:KEYW] * (DK ** -0.5) * jnp.exp(gates[d][0])).astype(BF16)
           for d, (qk_ref, _, _, _, _, _) in enumerate(dirs)]
    states = {(d, h): dirs[d][4][h] for d in range(2) for h in range(H)}

    def recurrence_step(step):
        for d, (qk_ref, v_ref, _, o_ref, _, rev) in enumerate(dirs):
            ci = nc - 1 - step if rev else step
            rs = slice(ci * CH, (ci + 1) * CH)
            for h in range(H):
                hk = slice(h * DK, (h + 1) * DK)
                hv = slice(h * DV, (h + 1) * DV)
                st = states[(d, h)]
                c = gates[d][0][rs, hk]
                o_ref[rs, hv] = lax.dot_general(qes[d][rs, hk], st.astype(BF16), _NT_DIMS,
                                                preferred_element_type=F32)
                tot = c[0:1, :] if rev else c[CH - 1:CH, :]
                kd = (qk_ref[rs, KEYW + h * DK:KEYW + (h + 1) * DK] * jnp.exp(tot - c)).astype(BF16)
                states[(d, h)] = st * jnp.exp(tot) + lax.dot_general(
                    v_ref[rs, hv], kd, _TN_DIMS, preferred_element_type=F32)

    for i in range(max(nc, CH // SB)):
        if i < CH // SB:
            score_round(i)
        if i < nc:
            recurrence_step(i)
    for (d, h), st in states.items():
        dirs[d][4][h] = st

    for d, (_, v_ref, _, o_ref, _, _) in enumerate(dirs):
        for ci in range(nc):
            rs = slice(ci * CH, (ci + 1) * CH)
            for h in range(H):
                hv = slice(h * DV, (h + 1) * DV)
                sc = jnp.concatenate(scores[(d, ci, h)], axis=0).astype(BF16)
                o_ref[rs, hv] += _dot(sc, v_ref[rs, hv])


def _bwd_block(j):
    return jnp.where(j == 0, 0, NRB - j)


def _gla(qk, vgp, r, wg, bg):
    qk3 = qk.reshape(B, LT, 2 * KEYW)
    vgp3 = vgp.reshape(B, LT, IN_MAIN - 2 * KEYW)
    r3 = r.reshape(B, LT, LANES)
    fwd = lambda b, j: (b, j, 0)
    bwd = lambda b, j: (b, _bwd_block(j), 0)
    o_f, o_b = pl.pallas_call(
        _gla_kernel,
        grid=(B, NRB),
        in_specs=[
            pl.BlockSpec((None, RB, 2 * KEYW), fwd),
            pl.BlockSpec((None, RB, 2 * KEYW), bwd),
            pl.BlockSpec((None, RB, GLA_W), fwd),
            pl.BlockSpec((None, RB, GLA_W), bwd),
            pl.BlockSpec((None, RB, LANES), fwd),
            pl.BlockSpec((None, RB, LANES), bwd),
            pl.BlockSpec((2, LANES, KEYW), lambda b, j: (0, 0, 0)),
            pl.BlockSpec((2, 1, KEYW), lambda b, j: (0, 0, 0)),
        ],
        out_specs=[
            pl.BlockSpec((None, RB, GLA_W), fwd),
            pl.BlockSpec((None, RB, GLA_W), bwd),
        ],
        out_shape=[jax.ShapeDtypeStruct((B, LT, GLA_W), F32)] * 2,
        scratch_shapes=[pltpu.VMEM((H, DV, DK), F32), pltpu.VMEM((H, DV, DK), F32)],
        compiler_params=_cparams(("parallel", "arbitrary")),
        name="gla",
    )(qk3, qk3, vgp3, vgp3, r3, r3, wg, bg)
    return o_f.reshape(NT, GLA_W), o_b.reshape(NT, GLA_W)


def _pool_tables():
    band = np.zeros((2, PG, TM, TM), np.float32)
    inv = np.zeros((2, PG, TM, PGW), np.float32)
    for kind, row_len in enumerate((GRID_W, CTX)):
        for gi, w in enumerate(WINDOWS):
            for n in range(TM):
                base, j = (n // row_len) * row_len, n % row_len
                lo = min(max(j - w // 2, 0), row_len)
                hi = min(max(j - w // 2 + w, 0), row_len)
                band[kind, gi, n, base + lo:base + hi] = 1.0
                inv[kind, gi, n, :] = 1.0 / (hi - lo)
    return band, inv


def _mix_kernel(with_router, n_x, *refs):
    x_refs = refs[:n_x]
    (of_ref, ob_ref, g_ref, p_ref, mod_ref, n2_ref, gain_ref,
     band_ref, inv_ref, wp_ref, ps_ref, wo_ref) = refs[n_x:n_x + 12]
    rest = refs[n_x + 12:]
    if with_router:
        wr_ref, xo_ref, h2_ref, lg_ref = rest
    else:
        xo_ref, h2_ref = rest
    mod = mod_ref[0]
    o = of_ref[...] + ob_ref[...]
    on = jnp.concatenate([_rms(o[:, h * DV:(h + 1) * DV]) for h in range(H)], axis=1)
    read = (on * gain_ref[...] * _silu(g_ref[...].astype(F32))).astype(BF16)
    y = _dot(read, wo_ref[0:GLA_W, :])
    pouts = []
    for gi in range(PG):
        pg = p_ref[:, gi * PGW:(gi + 1) * PGW]
        mixed = _dot(band_ref[0, gi], pg) * inv_ref[0, gi] - pg.astype(F32)
        pouts.append(_dot(mixed.astype(BF16), wp_ref[gi]))
    pool = (jnp.concatenate(pouts, axis=1) * ps_ref[...]).astype(BF16)
    y = y + _dot(pool, wo_ref[GLA_W:, :])
    xn = _x_tile(x_refs) + mod[:, 2 * D:3 * D] * y
    xo_ref[...] = xn
    h2 = _rms(xn) * n2_ref[...] * (1.0 + mod[:, 4 * D:5 * D]) + mod[:, 3 * D:4 * D]
    if with_router:
        for cb in range(SLAB):
            h2_ref[pl.ds(cb, TM, stride=SLAB), :] = h2[:, cb * LANES:(cb + 1) * LANES]
        h_hi, h_lo = _split2(h2)
        both = _dot(h_hi, wr_ref[...])
        lg_ref[...] = both[:, :LANES] + (both[:, LANES:] + _dot(h_lo, wr_ref[:, :LANES]))
    else:
        h2_ref[...] = h2.astype(BF16)


def _mix(xs, o_f, o_b, vgp, mod3, n2, gain, band, inv, wp, ps, wo, l, wr=None):
    with_router = wr is not None
    kind = lambda i: (jnp.where(i % TPB == 0, 1, 0), 0, 0, 0)
    const2 = lambda i: (0, 0)
    in_specs = _x_specs(xs) + [
        pl.BlockSpec((TM, GLA_W), lambda i: (i, 0)),
        pl.BlockSpec((TM, GLA_W), lambda i: (i, 0)),
        pl.BlockSpec((TM, GLA_W), lambda i: (i, 1)),
        pl.BlockSpec((TM, POOL_W), lambda i: (i, 2)),
        pl.BlockSpec((1, 1, 6 * D), lambda i: (_mod_row(i), 0, 0)),
        pl.BlockSpec((1, D), const2),
        pl.BlockSpec((1, GLA_W), const2),
        pl.BlockSpec((1, PG, TM, TM), kind),
        pl.BlockSpec((1, PG, TM, PGW), kind),
        pl.BlockSpec((PG, PGW, PGW), lambda i: (0, 0, 0)),
        pl.BlockSpec((1, POOL_W), const2),
        pl.BlockSpec((None, D, D), lambda i: (l, 0, 0)),
    ]
    args = [*xs, o_f, o_b, vgp, vgp, mod3, n2, gain, band, inv, wp, ps, wo]
    if with_router:
        h2_spec = pl.BlockSpec((TM * SLAB, LANES), lambda i: (i, 0))
        h2_shape = jax.ShapeDtypeStruct((NT * SLAB, LANES), F32)
    else:
        h2_spec = pl.BlockSpec((TM, D), lambda i: (i, 0))
        h2_shape = jax.ShapeDtypeStruct((NT, D), BF16)
    out_specs = [pl.BlockSpec((TM, D), lambda i: (i, 0)), h2_spec]
    out_shape = [jax.ShapeDtypeStruct((NT, D), F32), h2_shape]
    if with_router:
        in_specs.append(pl.BlockSpec((D, 2 * LANES), const2))
        args.append(wr)
        out_specs.append(pl.BlockSpec((TM, LANES), lambda i: (i, 0)))
        out_shape.append(jax.ShapeDtypeStruct((NT, LANES), F32))
    return pl.pallas_call(
        functools.partial(_mix_kernel, with_router, len(xs)),
        grid=(NTILES,),
        in_specs=in_specs,
        out_specs=out_specs,
        out_shape=out_shape,
        compiler_params=_cparams(("parallel",)),
        name="mix_router" if with_router else "mix",
    )(*args)


def _ffn_kernel(sub, straight, te_ref, ns_ref, blk_ref, xs_ref, w1_ref, w3_ref, w2_ref, o_ref):
    i = pl.program_id(0)
    j = pl.program_id(1)
    nsub = ns_ref[i]

    @pl.when(j == 0)
    def _():
        o_ref[...] = jnp.zeros_like(o_ref)

    def rows_update(rows):
        xt = xs_ref[rows, :]
        h1 = _dot(xt, w1_ref[0, 0].astype(BF16))
        h3 = _dot(xt, w3_ref[0, 0].astype(BF16))
        act = (_silu(h1) * h3).astype(BF16)
        o_ref[rows, :] += _dot(act, w2_ref[0, 0].astype(BF16))

    is_straight = False
    for n in straight:
        is_straight = jnp.logical_or(is_straight, nsub == n)

        @pl.when(nsub == n)
        def _(n=n):
            rows_update(pl.ds(0, n * sub))

    @pl.when(jnp.logical_and(nsub > 0, jnp.logical_not(is_straight)))
    def _():
        def body(s, carry):
            rows_update(pl.ds(pl.multiple_of(s * sub, sub), sub))
            return carry

        lax.fori_loop(0, nsub, body, 0)


def _ffn(xs, te, ns, blk, w1, w3, w2, wl, tm, sub, tf, straight):
    n_tiles = xs.shape[0] // tm
    nj = D_FF // tf
    out_spec = pl.BlockSpec((tm, D), lambda i, j, te, ns, blk: (i, 0))
    out_shape = jax.ShapeDtypeStruct((xs.shape[0], D), F32)
    live_j = lambda i, j, te, ns, blk: jnp.where(ns[i] > 0, j, nj - 1)
    grid_spec = pltpu.PrefetchScalarGridSpec(
        num_scalar_prefetch=3,
        grid=(n_tiles, nj),
        in_specs=[
            pl.BlockSpec((tm, D), lambda i, j, te, ns, blk: (blk[i], 0)),
            pl.BlockSpec((1, 1, D, tf), lambda i, j, te, ns, blk: (wl, te[i], 0, live_j(i, j, te, ns, blk))),
            pl.BlockSpec((1, 1, D, tf), lambda i, j, te, ns, blk: (wl, te[i], 0, live_j(i, j, te, ns, blk))),
            pl.BlockSpec((1, 1, tf, D), lambda i, j, te, ns, blk: (wl, te[i], live_j(i, j, te, ns, blk), 0)),
        ],
        out_specs=out_spec,
    )
    return pl.pallas_call(
        functools.partial(_ffn_kernel, sub, straight),
        grid_spec=grid_spec,
        out_shape=out_shape,
        compiler_params=_cparams(("arbitrary", "arbitrary")),
        name="ffn_%d" % tm,
    )(te, ns, blk, xs, w1, w3, w2)


def _lat_tile(t):
    return t + t // (TPB - 1) + 1


def _router_kernel(lg_ref, br_ref, route_ref, cnt_ref, carry_ref):
    @pl.when(pl.program_id(0) == 0)
    def _():
        carry_ref[...] = jnp.zeros_like(carry_ref)

    lane = lax.broadcasted_iota(jnp.int32, (TM, LANES), 1)
    z = jnp.where(lane < NE, lg_ref[...] + br_ref[...], -jnp.inf)
    m0 = jnp.max(z, axis=-1, keepdims=True)
    lane_f = lane.astype(F32)
    e0 = jnp.min(jnp.where(z == m0, lane_f, float(LANES)), axis=-1, keepdims=True)
    z1 = jnp.where(lane_f == e0, -jnp.inf, z)
    m1 = jnp.max(z1, axis=-1, keepdims=True)
    e1 = jnp.min(jnp.where(z1 == m1, lane_f, float(LANES)), axis=-1, keepdims=True)
    t = jnp.exp(m1 - m0)
    p0 = 1.0 / (1.0 + t)
    p1 = t / (1.0 + t)
    oh0 = lane_f == e0
    oh1 = lane_f == e1
    oh = jnp.where(jnp.logical_or(oh0, oh1), 1.0, 0.0)
    row = lax.broadcasted_iota(jnp.int32, (TM, TM), 0)
    col = lax.broadcasted_iota(jnp.int32, (TM, TM), 1)
    before = jnp.where(col < row, 1.0, 0.0).astype(BF16)
    excl = _dot(before, oh.astype(BF16)) + carry_ref[0:1, :]
    rank0 = jnp.sum(jnp.where(oh0, excl, 0.0), axis=-1, keepdims=True)
    rank1 = jnp.sum(jnp.where(oh1, excl, 0.0), axis=-1, keepdims=True)
    total = carry_ref[0:1, :] + jnp.sum(oh, axis=0, keepdims=True)
    carry_ref[...] = jnp.broadcast_to(total, carry_ref.shape)
    cnt_ref[...] = jnp.broadcast_to(total, cnt_ref.shape)
    route = jnp.where(lane == 0, e0, 0.0)
    route = jnp.where(lane == 1, e1, route)
    route = jnp.where(lane == 2, rank0, route)
    route = jnp.where(lane == 3, rank1, route)
    route = jnp.where(lane == 4, p0, route)
    route = jnp.where(lane == 5, p1, route)
    route_ref[...] = route


def _router(logits, b_r):
    n_lat_tiles = B * SEQ // TM
    return pl.pallas_call(
        _router_kernel,
        grid=(n_lat_tiles,),
        in_specs=[
            pl.BlockSpec((TM, LANES), lambda t: (_lat_tile(t), 0)),
            pl.BlockSpec((1, LANES), lambda t: (0, 0)),
        ],
        out_specs=[
            pl.BlockSpec((TM, LANES), lambda t: (t, 0)),
            pl.BlockSpec((8, LANES), lambda t: (0, 0)),
        ],
        out_shape=[jax.ShapeDtypeStruct((B * SEQ, LANES), F32),
                   jax.ShapeDtypeStruct((8, LANES), F32)],
        scratch_shapes=[pltpu.VMEM((8, LANES), F32)],
        compiler_params=_cparams(("arbitrary",)),
        name="router",
    )(logits, b_r)


ISSUE_UNROLL = 8


def _slab_copy(src_hbm, tok, dst, r, sem):
    return pltpu.make_async_copy(src_hbm.at[pl.ds(pl.multiple_of(tok * SLAB, SLAB), SLAB), :],
                                 dst.at[pl.ds(r * SLAB_PITCH, SLAB), :], sem)


def _slab_wait_all(src_hbm, dst, sem):
    pltpu.make_async_copy(src_hbm.at[pl.ds(0, TM * SLAB), :], dst.at[pl.ds(0, TM * SLAB), :], sem).wait()


def _gather_kernel(live_ref, idx_ref, idx_next_ref, h_hbm, o_ref, buf_ref, sems):
    i = pl.program_id(0)
    n = pl.num_programs(0)
    live = live_ref[i]
    nxt = jnp.minimum(i + 1, n - 1)
    start_next = jnp.logical_and(i + 1 < n, live_ref[nxt] != 0)

    def issue(ids_ref, s):
        def body(g, carry):
            for u in range(ISSUE_UNROLL):
                r = g * ISSUE_UNROLL + u
                _slab_copy(h_hbm, ids_ref[0, 0, r], buf_ref.at[s], r, sems.at[s]).start(priority=u % 2)
            return carry

        lax.fori_loop(0, TM // ISSUE_UNROLL, body, 0)

    def finish(s):
        _slab_wait_all(h_hbm, buf_ref.at[s], sems.at[s])
        for cb in range(SLAB):
            o_ref[:, cb * LANES:(cb + 1) * LANES] = (
                buf_ref[s, pl.ds(cb, TM, stride=SLAB_PITCH), :].astype(BF16))

    @pl.when(jnp.logical_and(i == 0, live != 0))
    def _():
        issue(idx_ref, 0)

    for s in range(2):
        @pl.when(jnp.logical_and(start_next, (i + 1) % 2 == s))
        def _(s=s):
            issue(idx_next_ref, s)

    @pl.when(live == 0)
    def _():
        o_ref[...] = jnp.zeros_like(o_ref)

    for s in range(2):
        @pl.when(jnp.logical_and(live != 0, i % 2 == s))
        def _(s=s):
            finish(s)


def _gather(h2_slabs, src, live):
    n = src.shape[0] // TM
    ids = src.reshape(n, 1, TM)
    grid_spec = pltpu.PrefetchScalarGridSpec(
        num_scalar_prefetch=1,
        grid=(n,),
        in_specs=[
            pl.BlockSpec((1, 1, TM), lambda i, live: (i, 0, 0), memory_space=pltpu.SMEM),
            pl.BlockSpec((1, 1, TM), lambda i, live: (jnp.minimum(i + 1, n - 1), 0, 0),
                         memory_space=pltpu.SMEM),
            pl.BlockSpec(memory_space=pl.ANY),
        ],
        out_specs=pl.BlockSpec((TM, D), lambda i, live: (i, 0)),
        scratch_shapes=[pltpu.VMEM((2, TM * SLAB_PITCH, LANES), F32), pltpu.SemaphoreType.DMA((2,))],
    )
    return pl.pallas_call(
        _gather_kernel,
        grid_spec=grid_spec,
        out_shape=jax.ShapeDtypeStruct((src.shape[0], D), BF16),
        compiler_params=_cparams(("arbitrary",)),
        name="dispatch",
    )(live, ids, ids, h2_slabs)


def _combine_kernel(p0_ref, p1_ref, p0n_ref, p1n_ref, x_ref, route_ref, mod_ref, fg_ref, y_hbm, o_ref,
                    buf_ref, sems):
    t = pl.program_id(0)
    n = pl.num_programs(0)

    def issue(q0_ref, q1_ref, s):
        def body(g, carry):
            for u in range(ISSUE_UNROLL):
                r = g * ISSUE_UNROLL + u
                for k, q_ref in enumerate((q0_ref, q1_ref)):
                    pltpu.make_async_copy(y_hbm.at[pl.ds(q_ref[0, 0, r], 1), :],
                                          buf_ref.at[s, k, pl.ds(r, 1), :], sems.at[s, k]).start(priority=k)
            return carry

        lax.fori_loop(0, TM // ISSUE_UNROLL, body, 0)

    def finish(s):
        for k in range(2):
            pltpu.make_async_copy(y_hbm.at[pl.ds(0, TM), :], buf_ref.at[s, k], sems.at[s, k]).wait()
        route = route_ref[...]
        y = route[:, 4:5] * buf_ref[s, 0] + route[:, 5:6] * buf_ref[s, 1]
        xn = x_ref[...] + mod_ref[0][:, 5 * D:6 * D] * y
        o_ref[...] = _rms(xn) * fg_ref[...]

    @pl.when(t == 0)
    def _():
        issue(p0_ref, p1_ref, 0)

    for s in range(2):
        @pl.when(jnp.logical_and(t + 1 < n, (t + 1) % 2 == s))
        def _(s=s):
            issue(p0n_ref, p1n_ref, s)

    for s in range(2):
        @pl.when(t % 2 == s)
        def _(s=s):
            finish(s)


def _combine(x, ys, pos0, pos1, route, mod3, fg):
    n = B * SEQ // TM
    smem = lambda: pl.BlockSpec((1, 1, TM), lambda t: (t, 0, 0), memory_space=pltpu.SMEM)
    smem_next = lambda: pl.BlockSpec((1, 1, TM), lambda t: (jnp.minimum(t + 1, n - 1), 0, 0),
                                     memory_space=pltpu.SMEM)
    p0 = pos0.reshape(n, 1, TM)
    p1 = pos1.reshape(n, 1, TM)
    return pl.pallas_call(
        _combine_kernel,
        grid=(n,),
        in_specs=[
            smem(), smem(), smem_next(), smem_next(),
            pl.BlockSpec((TM, D), lambda t: (_lat_tile(t), 0)),
            pl.BlockSpec((TM, LANES), lambda t: (t, 0)),
            pl.BlockSpec((1, 1, 6 * D), lambda t: (t // (TPB - 1), 0, 0)),
            pl.BlockSpec((1, D), lambda t: (0, 0)),
            pl.BlockSpec(memory_space=pl.ANY),
        ],
        out_specs=pl.BlockSpec((TM, D), lambda t: (t, 0)),
        out_shape=jax.ShapeDtypeStruct((B * SEQ, D), F32),
        scratch_shapes=[pltpu.VMEM((2, 2, TM, D), F32), pltpu.SemaphoreType.DMA((2, 2))],
        compiler_params=_cparams(("arbitrary",)),
        name="combine",
    )(p0, p1, p0, p1, x, route, mod3, fg, ys)


def _moe_plan(route, counts):
    cnt = counts[0, :NE].astype(jnp.int32)
    nsub = (cnt + SUB_MOE - 1) // SUB_MOE
    ntile = (nsub + SUBS_MOE - 1) // SUBS_MOE
    tile_end = jnp.cumsum(ntile)
    tile_start = tile_end - ntile

    def slot_of(k):
        e_k = route[:, k].astype(jnp.int32)
        first = jnp.zeros_like(e_k)
        for ex in range(NE):
            first = jnp.where(e_k == ex, tile_start[ex] * TM_MOE, first)
        return first + route[:, 2 + k].astype(jnp.int32)

    pos0, pos1 = slot_of(0), slot_of(1)
    n_used = tile_end[NE - 1]
    tiles = jnp.arange(NT_MOE, dtype=jnp.int32)
    blk = jnp.minimum(tiles, n_used - 1)
    te = jnp.sum((blk[:, None] >= tile_end[None, :NE - 1]).astype(jnp.int32), axis=1)
    ns = jnp.clip(nsub[te] - (blk - tile_start[te]) * SUBS_MOE, 0, SUBS_MOE)
    ns = jnp.where(tiles < n_used, ns, 0).astype(jnp.int32)
    tok = jnp.arange(B * SEQ, dtype=jnp.int32)
    tok_row = tok + (tok // SEQ + 1) * CTX
    src = jnp.full((R_MOE,), CTX, jnp.int32).at[jnp.concatenate([pos0, pos1])].set(
        jnp.concatenate([tok_row, tok_row]), unique_indices=True)
    first_row = jnp.arange(0, TM_MOE, TM, dtype=jnp.int32)
    live = (first_row[None, :] < ns[:, None] * SUB_MOE).astype(jnp.int32).reshape(-1)
    return pos0, pos1, te, ns, blk, src, live


def kernel(x, c, ctx, c_ctx, w_ada, b_ada, norm1_g, norm2_g, w_in, w_gate, b_gate, gla_norm_g,
           w_pool, pool_scale, w_out, ffn_w1, ffn_w3, ffn_w2, moe_w_router, moe_b_router,
           moe_w1, moe_w3, moe_w2, final_g):
    assert x.shape == (B, SEQ, D) and ctx.shape == (B, CTX, D) and DEPTH == 2

    cvec = jnp.concatenate([c, c_ctx[None, :], jnp.zeros((8 - B - 1, D), F32)], axis=0)
    mods = _ada(cvec, w_ada, b_ada)
    xs = (x, ctx)

    w_in_t = jnp.swapaxes(w_in, 1, 2)
    w_r_t = jnp.pad(w_in_t[:, IN_MAIN:, :], ((0, 0), (0, LANES - 2 * RANK), (0, 0)))
    w_out_b = w_out.astype(BF16)
    w_pool_b = w_pool.astype(BF16)

    band_np, inv_np = _pool_tables()
    band = jnp.asarray(band_np, BF16)
    inv = jnp.asarray(inv_np, F32)

    dense_te = jnp.zeros((NT // TM_DENSE,), jnp.int32)
    dense_ns = jnp.full((NT // TM_DENSE,), TM_DENSE // SUB_DENSE, jnp.int32)
    dense_blk = jnp.arange(NT // TM_DENSE, dtype=jnp.int32)

    out = None
    for l in range(DEPTH):
        mod3 = mods[l, :3].reshape(3, 1, 6 * D)
        wg = jnp.zeros((2, LANES, KEYW), F32)
        wg = wg.at[0, 0:RANK].set(w_gate[l, 0]).at[1, RANK:2 * RANK].set(w_gate[l, 1]).astype(BF16)
        bg = b_gate[l].reshape(2, 1, KEYW)

        qk, vgp, r = _inproj(xs, mod3, norm1_g[l].reshape(1, D), w_in_t, w_r_t, l)
        o_f, o_b = _gla(qk, vgp, r, wg, bg)
        mix_args = (xs, o_f, o_b, vgp, mod3, norm2_g[l].reshape(1, D), gla_norm_g[l].reshape(1, GLA_W),
                    band, inv, w_pool_b[l], pool_scale[l].reshape(1, POOL_W), w_out_b, l)
        if l % 2 == 0:
            x1, h2 = _mix(*mix_args)
            jl = l // 2
            ys = _ffn(h2, dense_te, dense_ns, dense_blk, ffn_w1[:, None], ffn_w3[:, None],
                      ffn_w2[:, None], jl, TM_DENSE, SUB_DENSE, TF_DENSE,
                      straight=(TM_DENSE // SUB_DENSE,))
            xs = (x1, ys, mod3)
        else:
            jl = l // 2
            wr = jnp.concatenate(_split2(jnp.pad(moe_w_router[jl], ((0, 0), (0, LANES - NE)))), axis=1)
            br = jnp.pad(moe_b_router[jl], (0, LANES - NE)).reshape(1, LANES)
            x1, h2, logits = _mix(*mix_args, wr=wr)
            route, counts = _router(logits, br)
            pos0, pos1, te, ns, blk, src, live = _moe_plan(route, counts)
            xg = _gather(h2, src, live)
            ys = _ffn(xg, te, ns, blk, moe_w1, moe_w3, moe_w2, jl, TM_MOE, SUB_MOE, TF_MOE,
                      straight=STRAIGHT_MOE)
            out = _combine(x1, ys, pos0, pos1, route, mod3, final_g.reshape(1, D))
    return out.reshape(B, SEQ, D)
```

```python
import functools

import numpy as np
import jax
import jax.numpy as jnp
from jax import lax
from jax.experimental import pallas as pl
from jax.experimental.pallas import tpu as pltpu

F32 = jnp.float32
BF16 = jnp.bfloat16

D = 2048
B = 2
SEQ = 4096
CTX = 256
DEPTH = 2
GRID_W = 64
EPS = 1e-6
GLA_W = 1024
POOL_W = 1024
H = 4
DV = 256
DK = 128
KEYW = H * DK
RANK = 16
TAU = 16.0
WINDOWS = (2, 4, 8, 16)
PG = 4
PGW = 256
D_FF = 5632
NE = 8

LT = CTX + SEQ
NT = B * LT
TM = 256
TPB = LT // TM
NTILES = NT // TM
LANES = 128
SLAB = D // LANES
SLAB_PITCH = SLAB + 4

CH = 64
SB = 16
RB = 256
NRB = LT // RB
assert RB == CTX and RB % CH == 0

TF_DENSE = 512
TF_MOE = 256
TM_DENSE = NT // 8
SUB_DENSE = TM_DENSE // 4
SUB_MOE = 256
SUBS_MOE = 5
STRAIGHT_MOE = (5, 4, 3, 2)
TM_MOE = SUBS_MOE * SUB_MOE
N_ASSIGN = B * SEQ * 2
NT_MOE = (N_ASSIGN // SUB_MOE + NE + (SUBS_MOE - 1) * NE) // SUBS_MOE
R_MOE = NT_MOE * TM_MOE
assert TM_MOE % TM == 0

VMEM_LIMIT = 56 * 1024 * 1024


def _cparams(sem, vmem=VMEM_LIMIT):
    return pltpu.CompilerParams(dimension_semantics=sem, vmem_limit_bytes=vmem)


def _split2(a):
    hi = a.astype(BF16)
    lo = (a - hi.astype(F32)).astype(BF16)
    return hi, lo


def _dot(a, b):
    return jnp.dot(a, b, preferred_element_type=F32)


def _dot3(a, b):
    ah, al = _split2(a)
    bh, bl = _split2(b)
    return _dot(ah, bh) + (_dot(al, bh) + _dot(ah, bl))


def _silu(a):
    return a / (1.0 + jnp.exp(-a))


def _rms(x):
    return x * lax.rsqrt(jnp.mean(x * x, axis=-1, keepdims=True) + EPS)


def _mod_row(i):
    return jnp.where(i % TPB == 0, 2, i // TPB)


ADA_TN = 2048


def _ada_kernel(c_ref, w_ref, b_ref, o_ref):
    s = _silu(c_ref[...])
    o_ref[0] = _dot3(s, w_ref[0]) + b_ref[0]


def _ada(cvec, w_ada, b_ada):
    n = 6 * D
    return pl.pallas_call(
        _ada_kernel,
        grid=(DEPTH, n // ADA_TN),
        in_specs=[
            pl.BlockSpec((8, D), lambda l, j: (0, 0)),
            pl.BlockSpec((1, D, ADA_TN), lambda l, j: (l, 0, j)),
            pl.BlockSpec((1, 1, ADA_TN), lambda l, j: (l, 0, j)),
        ],
        out_specs=pl.BlockSpec((1, 8, ADA_TN), lambda l, j: (l, 0, j)),
        out_shape=jax.ShapeDtypeStruct((DEPTH, 8, n), F32),
        compiler_params=_cparams(("parallel", "parallel")),
        name="ada",
    )(cvec, w_ada, b_ada.reshape(DEPTH, 1, n))


IN_MAIN = 2 * KEYW + 2 * GLA_W + POOL_W


def _x_specs(xs):
    if len(xs) == 2:
        return [pl.BlockSpec((None, TM, D), lambda i: (i // TPB, jnp.maximum(i % TPB - 1, 0), 0)),
                pl.BlockSpec((None, TM, D), lambda i: (i // TPB, 0, 0))]
    return [pl.BlockSpec((TM, D), lambda i: (i, 0)),
            pl.BlockSpec((TM, D), lambda i: (i, 0)),
            pl.BlockSpec((1, 1, 6 * D), lambda i: (_mod_row(i), 0, 0))]


def _x_tile(x_refs):
    if len(x_refs) == 2:
        return jnp.where(pl.program_id(0) % TPB == 0, x_refs[1][...], x_refs[0][...])
    x_ref, y_ref, mod_ref = x_refs
    return x_ref[...] + mod_ref[0][:, 5 * D:6 * D] * y_ref[...]


def _inproj_kernel(n_x, *refs):
    x_refs = refs[:n_x]
    mod_ref, g_ref, w_ref, wr_ref, qk_ref, vgp_ref, r_ref = refs[n_x:]
    mod = mod_ref[0]
    h = (_rms(_x_tile(x_refs)) * g_ref[...] * (1.0 + mod[:, D:2 * D]) + mod[:, 0:D]).astype(BF16)
    def proj(wt):
        return lax.dot_general(h, wt.astype(BF16), _NT_DIMS, preferred_element_type=F32)

    wc = 2 * KEYW
    qk_ref[...] = proj(w_ref[0:wc, :])
    for c0 in range(wc, IN_MAIN, wc):
        vgp_ref[:, c0 - wc:c0] = proj(w_ref[c0:c0 + wc, :]).astype(BF16)
    r_ref[...] = proj(wr_ref[...])


def _inproj(xs, mod3, g, w, wr, l):
    return pl.pallas_call(
        functools.partial(_inproj_kernel, len(xs)),
        grid=(NTILES,),
        in_specs=_x_specs(xs) + [
            pl.BlockSpec((1, 1, 6 * D), lambda i: (_mod_row(i), 0, 0)),
            pl.BlockSpec((1, D), lambda i: (0, 0)),
            pl.BlockSpec((None, IN_MAIN, D), lambda i: (l, 0, 0), pipeline_mode=pl.Buffered(1)),
            pl.BlockSpec((None, LANES, D), lambda i: (l, 0, 0)),
        ],
        out_specs=[
            pl.BlockSpec((TM, 2 * KEYW), lambda i: (i, 0)),
            pl.BlockSpec((TM, IN_MAIN - 2 * KEYW), lambda i: (i, 0)),
            pl.BlockSpec((TM, LANES), lambda i: (i, 0)),
        ],
        out_shape=[
            jax.ShapeDtypeStruct((NT, 2 * KEYW), F32),
            jax.ShapeDtypeStruct((NT, IN_MAIN - 2 * KEYW), BF16),
            jax.ShapeDtypeStruct((NT, LANES), F32),
        ],
        compiler_params=_cparams(("parallel",)),
        name="inproj",
    )(*xs, mod3, g, w, wr)


_NT_DIMS = (((1,), (1,)), ((), ()))
_TN_DIMS = (((0,), (0,)), ((), ()))


def _gla_gates(r_ref, wg_ref, bg_ref, rev):
    d = 1 if rev else 0
    pre = _dot(r_ref[...].astype(BF16), wg_ref[d]) + bg_ref[d]
    lg = (jnp.minimum(pre, 0.0) - jnp.log(1.0 + jnp.exp(-jnp.abs(pre)))) * (1.0 / TAU)
    row = lax.broadcasted_iota(jnp.int32, (RB, RB), 0)
    col = lax.broadcasted_iota(jnp.int32, (RB, RB), 1)
    inside = jnp.logical_and(row // CH == col // CH, (col >= row) if rev else (col <= row))
    tri = jnp.where(inside, 1.0, 0.0).astype(BF16)
    lg_hi, lg_lo = _split2(lg)
    cum = _dot(tri, lg_hi) + _dot(tri, lg_lo)
    return cum, cum - lg


def _gla_kernel(qkf_ref, qkb_ref, vf_ref, vb_ref, rf_ref, rb_ref, wg_ref, bg_ref,
                of_ref, ob_ref, stf_ref, stb_ref):
    @pl.when(pl.program_id(0) == 0)
    def _():
        stf_ref[...] = jnp.zeros_like(stf_ref)
        stb_ref[...] = jnp.zeros_like(stb_ref)

    nc = RB // CH
    dirs = []
    for b in range(B):
        dirs.append((qkf_ref.at[b], vf_ref.at[b], rf_ref.at[b], of_ref.at[b], stf_ref.at[b], False))
        dirs.append((qkb_ref.at[b], vb_ref.at[b], rb_ref.at[b], ob_ref.at[b], stb_ref.at[b], True))
    t_idx = lax.broadcasted_iota(jnp.int32, (SB, CH), 0)
    s_idx = lax.broadcasted_iota(jnp.int32, (SB, CH), 1)

    gates = [_gla_gates(r_ref, wg_ref, bg_ref, rev) for (_, _, r_ref, _, _, rev) in dirs]

    scores = {}

    def score_round(blk):
        r0 = blk * SB
        for d, (qk_ref, _, _, _, _, rev) in enumerate(dirs):
            cum, cex = gates[d]
            c3 = cum.reshape(nc, CH, KEYW)
            ce3 = cex.reshape(nc, CH, KEYW)
            q3 = (qk_ref[:, 0:KEYW] * (DK ** -0.5)).reshape(nc, CH, KEYW)
            k3 = qk_ref[:, KEYW:2 * KEYW].reshape(nc, CH, KEYW)
            if rev:
                a = ce3[:, r0 + SB - 1:r0 + SB, :]
                lo, hi = r0, CH
                keep = s_idx >= t_idx + r0
            else:
                a = ce3[:, r0:r0 + 1, :]
                lo, hi = 0, r0 + SB
                keep = s_idx <= t_idx + r0
            qb = (q3[:, r0:r0 + SB, :] * jnp.exp(c3[:, r0:r0 + SB, :] - a)).astype(BF16)
            parts = [k3[:, lo:hi, :] * jnp.exp(a - c3[:, lo:hi, :])]
            if lo > 0:
                parts.insert(0, k3[:, 0:lo, :])
            if hi < CH:
                parts.append(k3[:, hi:CH, :])
            kb = jnp.concatenate(parts, axis=1).astype(BF16)
            for ci in range(nc):
                for h in range(H):
                    hk = slice(h * DK, (h + 1) * DK)
                    s = lax.dot_general(qb[ci][:, hk], kb[ci][:, hk], _NT_DIMS,
                                        preferred_element_type=F32)
                    scores.setdefault((d, ci, h), []).append(jnp.where(keep, s, 0.0))

    qes = [(qk_ref[:, 0:KEYW] * (DK ** -0.5) * jnp.exp(gates[d][0])).astype(BF16)
           for d, (qk_ref, _, _, _, _, _) in enumerate(dirs)]
    states = {(d, h): dirs[d][4][h] for d in range(len(dirs)) for h in range(H)}

    def recurrence_step(step):
        for d, (qk_ref, v_ref, _, o_ref, _, rev) in enumerate(dirs):
            ci = nc - 1 - step if rev else step
            rs = slice(ci * CH, (ci + 1) * CH)
            for h in range(H):
                hk = slice(h * DK, (h + 1) * DK)
                hv = slice(h * DV, (h + 1) * DV)
                st = states[(d, h)]
                c = gates[d][0][rs, hk]
                o_ref[rs, hv] = lax.dot_general(qes[d][rs, hk], st.astype(BF16), _NT_DIMS,
                                                preferred_element_type=F32)
                tot = c[0:1, :] if rev else c[CH - 1:CH, :]
                kd = (qk_ref[rs, KEYW + h * DK:KEYW + (h + 1) * DK] * jnp.exp(tot - c)).astype(BF16)
                states[(d, h)] = st * jnp.exp(tot) + lax.dot_general(
                    v_ref[rs, hv], kd, _TN_DIMS, preferred_element_type=F32)

    for i in range(max(nc, CH // SB)):
        if i < CH // SB:
            score_round(i)
        if i < nc:
            recurrence_step(i)
    for (d, h), st in states.items():
        dirs[d][4][h] = st

    for d, (_, v_ref, _, o_ref, _, _) in enumerate(dirs):
        for ci in range(nc):
            rs = slice(ci * CH, (ci + 1) * CH)
            for h in range(H):
                hv = slice(h * DV, (h + 1) * DV)
                sc = jnp.concatenate(scores[(d, ci, h)], axis=0).astype(BF16)
                o_ref[rs, hv] += _dot(sc, v_ref[rs, hv])


def _bwd_block(j):
    return jnp.where(j == 0, 0, NRB - j)


def _gla(qk, vgp, r, wg, bg):
    qk3 = qk.reshape(B, LT, 2 * KEYW)
    vgp3 = vgp.reshape(B, LT, IN_MAIN - 2 * KEYW)
    r3 = r.reshape(B, LT, LANES)
    fwd = lambda j: (0, j, 0)
    bwd = lambda j: (0, _bwd_block(j), 0)
    o_f, o_b = pl.pallas_call(
        _gla_kernel,
        grid=(NRB,),
        in_specs=[
            pl.BlockSpec((B, RB, 2 * KEYW), fwd),
            pl.BlockSpec((B, RB, 2 * KEYW), bwd),
            pl.BlockSpec((B, RB, GLA_W), fwd),
            pl.BlockSpec((B, RB, GLA_W), bwd),
            pl.BlockSpec((B, RB, LANES), fwd),
            pl.BlockSpec((B, RB, LANES), bwd),
            pl.BlockSpec((2, LANES, KEYW), lambda j: (0, 0, 0)),
            pl.BlockSpec((2, 1, KEYW), lambda j: (0, 0, 0)),
        ],
        out_specs=[
            pl.BlockSpec((B, RB, GLA_W), fwd),
            pl.BlockSpec((B, RB, GLA_W), bwd),
        ],
        out_shape=[jax.ShapeDtypeStruct((B, LT, GLA_W), F32)] * 2,
        scratch_shapes=[pltpu.VMEM((B, H, DV, DK), F32), pltpu.VMEM((B, H, DV, DK), F32)],
        compiler_params=_cparams(("arbitrary",)),
        name="gla",
    )(qk3, qk3, vgp3, vgp3, r3, r3, wg, bg)
    return o_f.reshape(NT, GLA_W), o_b.reshape(NT, GLA_W)


def _pool_tables():
    band = np.zeros((2, PG, TM, TM), np.float32)
    inv = np.zeros((2, PG, TM, PGW), np.float32)
    for kind, row_len in enumerate((GRID_W, CTX)):
        for gi, w in enumerate(WINDOWS):
            for n in range(TM):
                base, j = (n // row_len) * row_len, n % row_len
                lo = min(max(j - w // 2, 0), row_len)
                hi = min(max(j - w // 2 + w, 0), row_len)
                band[kind, gi, n, base + lo:base + hi] = 1.0
                inv[kind, gi, n, :] = 1.0 / (hi - lo)
    return band, inv


def _mix_kernel(with_router, n_x, *refs):
    x_refs = refs[:n_x]
    (of_ref, ob_ref, g_ref, p_ref, mod_ref, n2_ref, gain_ref,
     band_ref, inv_ref, wp_ref, ps_ref, wo_ref) = refs[n_x:n_x + 12]
    rest = refs[n_x + 12:]
    if with_router:
        wr_ref, xo_ref, h2_ref, lg_ref = rest
    else:
        xo_ref, h2_ref = rest
    mod = mod_ref[0]
    o = of_ref[...] + ob_ref[...]
    on = jnp.concatenate([_rms(o[:, h * DV:(h + 1) * DV]) for h in range(H)], axis=1)
    read = (on * gain_ref[...] * _silu(g_ref[...].astype(F32))).astype(BF16)
    y = _dot(read, wo_ref[0:GLA_W, :])
    pouts = []
    for gi in range(PG):
        pg = p_ref[:, gi * PGW:(gi + 1) * PGW]
        mixed = _dot(band_ref[0, gi], pg) * inv_ref[0, gi] - pg.astype(F32)
        pouts.append(_dot(mixed.astype(BF16), wp_ref[gi]))
    pool = (jnp.concatenate(pouts, axis=1) * ps_ref[...]).astype(BF16)
    y = y + _dot(pool, wo_ref[GLA_W:, :])
    xn = _x_tile(x_refs) + mod[:, 2 * D:3 * D] * y
    xo_ref[...] = xn
    h2 = _rms(xn) * n2_ref[...] * (1.0 + mod[:, 4 * D:5 * D]) + mod[:, 3 * D:4 * D]
    if with_router:
        for cb in range(SLAB):
            h2_ref[pl.ds(cb, TM, stride=SLAB), :] = h2[:, cb * LANES:(cb + 1) * LANES]
        h_hi, h_lo = _split2(h2)
        both = _dot(h_hi, wr_ref[...])
        lg_ref[...] = both[:, :LANES] + (both[:, LANES:] + _dot(h_lo, wr_ref[:, :LANES]))
    else:
        h2_ref[...] = h2.astype(BF16)


def _mix(xs, o_f, o_b, vgp, mod3, n2, gain, band, inv, wp, ps, wo, l, wr=None):
    with_router = wr is not None
    kind = lambda i: (jnp.where(i % TPB == 0, 1, 0), 0, 0, 0)
    const2 = lambda i: (0, 0)
    in_specs = _x_specs(xs) + [
        pl.BlockSpec((TM, GLA_W), lambda i: (i, 0)),
        pl.BlockSpec((TM, GLA_W), lambda i: (i, 0)),
        pl.BlockSpec((TM, GLA_W), lambda i: (i, 1)),
        pl.BlockSpec((TM, POOL_W), lambda i: (i, 2)),
        pl.BlockSpec((1, 1, 6 * D), lambda i: (_mod_row(i), 0, 0)),
        pl.BlockSpec((1, D), const2),
        pl.BlockSpec((1, GLA_W), const2),
        pl.BlockSpec((1, PG, TM, TM), kind),
        pl.BlockSpec((1, PG, TM, PGW), kind),
        pl.BlockSpec((PG, PGW, PGW), lambda i: (0, 0, 0)),
        pl.BlockSpec((1, POOL_W), const2),
        pl.BlockSpec((None, D, D), lambda i: (l, 0, 0)),
    ]
    args = [*xs, o_f, o_b, vgp, vgp, mod3, n2, gain, band, inv, wp, ps, wo]
    if with_router:
        h2_spec = pl.BlockSpec((TM * SLAB, LANES), lambda i: (i, 0))
        h2_shape = jax.ShapeDtypeStruct((NT * SLAB, LANES), F32)
    else:
        h2_spec = pl.BlockSpec((TM, D), lambda i: (i, 0))
        h2_shape = jax.ShapeDtypeStruct((NT, D), BF16)
    out_specs = [pl.BlockSpec((TM, D), lambda i: (i, 0)), h2_spec]
    out_shape = [jax.ShapeDtypeStruct((NT, D), F32), h2_shape]
    if with_router:
        in_specs.append(pl.BlockSpec((D, 2 * LANES), const2))
        args.append(wr)
        out_specs.append(pl.BlockSpec((TM, LANES), lambda i: (i, 0)))
        out_shape.append(jax.ShapeDtypeStruct((NT, LANES), F32))
    return pl.pallas_call(
        functools.partial(_mix_kernel, with_router, len(xs)),
        grid=(NTILES,),
        in_specs=in_specs,
        out_specs=out_specs,
        out_shape=out_shape,
        compiler_params=_cparams(("parallel",)),
        name="mix_router" if with_router else "mix",
    )(*args)


def _ffn_kernel(sub, straight, te_ref, ns_ref, blk_ref, xs_ref, w1_ref, w3_ref, w2_ref, o_ref):
    i = pl.program_id(0)
    j = pl.program_id(1)
    nsub = ns_ref[i]

    @pl.when(j == 0)
    def _():
        o_ref[...] = jnp.zeros_like(o_ref)

    def rows_update(rows):
        xt = xs_ref[rows, :]
        h1 = _dot(xt, w1_ref[0, 0].astype(BF16))
        h3 = _dot(xt, w3_ref[0, 0].astype(BF16))
        act = (_silu(h1) * h3).astype(BF16)
        o_ref[rows, :] += _dot(act, w2_ref[0, 0].astype(BF16))

    is_straight = False
    for n in straight:
        is_straight = jnp.logical_or(is_straight, nsub == n)

        @pl.when(nsub == n)
        def _(n=n):
            rows_update(pl.ds(0, n * sub))

    @pl.when(jnp.logical_and(nsub > 0, jnp.logical_not(is_straight)))
    def _():
        def body(s, carry):
            rows_update(pl.ds(pl.multiple_of(s * sub, sub), sub))
            return carry

        lax.fori_loop(0, nsub, body, 0)


def _ffn(xs, te, ns, blk, w1, w3, w2, wl, tm, sub, tf, straight):
    n_tiles = xs.shape[0] // tm
    nj = D_FF // tf
    out_spec = pl.BlockSpec((tm, D), lambda i, j, te, ns, blk: (i, 0))
    out_shape = jax.ShapeDtypeStruct((xs.shape[0], D), F32)
    live_j = lambda i, j, te, ns, blk: jnp.where(ns[i] > 0, j, nj - 1)
    grid_spec = pltpu.PrefetchScalarGridSpec(
        num_scalar_prefetch=3,
        grid=(n_tiles, nj),
        in_specs=[
            pl.BlockSpec((tm, D), lambda i, j, te, ns, blk: (blk[i], 0)),
            pl.BlockSpec((1, 1, D, tf), lambda i, j, te, ns, blk: (wl, te[i], 0, live_j(i, j, te, ns, blk))),
            pl.BlockSpec((1, 1, D, tf), lambda i, j, te, ns, blk: (wl, te[i], 0, live_j(i, j, te, ns, blk))),
            pl.BlockSpec((1, 1, tf, D), lambda i, j, te, ns, blk: (wl, te[i], live_j(i, j, te, ns, blk), 0)),
        ],
        out_specs=out_spec,
    )
    return pl.pallas_call(
        functools.partial(_ffn_kernel, sub, straight),
        grid_spec=grid_spec,
        out_shape=out_shape,
        compiler_params=_cparams(("arbitrary", "arbitrary")),
        name="ffn_%d" % tm,
    )(te, ns, blk, xs, w1, w3, w2)


def _lat_tile(t):
    return t + t // (TPB - 1) + 1


def _router_kernel(lg_ref, br_ref, route_ref, cnt_ref, carry_ref):
    @pl.when(pl.program_id(0) == 0)
    def _():
        carry_ref[...] = jnp.zeros_like(carry_ref)

    lane = lax.broadcasted_iota(jnp.int32, (TM, LANES), 1)
    z = jnp.where(lane < NE, lg_ref[...] + br_ref[...], -jnp.inf)
    m0 = jnp.max(z, axis=-1, keepdims=True)
    lane_f = lane.astype(F32)
    e0 = jnp.min(jnp.where(z == m0, lane_f, float(LANES)), axis=-1, keepdims=True)
    z1 = jnp.where(lane_f == e0, -jnp.inf, z)
    m1 = jnp.max(z1, axis=-1, keepdims=True)
    e1 = jnp.min(jnp.where(z1 == m1, lane_f, float(LANES)), axis=-1, keepdims=True)
    t = jnp.exp(m1 - m0)
    p0 = 1.0 / (1.0 + t)
    p1 = t / (1.0 + t)
    oh0 = lane_f == e0
    oh1 = lane_f == e1
    oh = jnp.where(jnp.logical_or(oh0, oh1), 1.0, 0.0)
    row = lax.broadcasted_iota(jnp.int32, (TM, TM), 0)
    col = lax.broadcasted_iota(jnp.int32, (TM, TM), 1)
    before = jnp.where(col < row, 1.0, 0.0).astype(BF16)
    excl = _dot(before, oh.astype(BF16)) + carry_ref[0:1, :]
    rank0 = jnp.sum(jnp.where(oh0, excl, 0.0), axis=-1, keepdims=True)
    rank1 = jnp.sum(jnp.where(oh1, excl, 0.0), axis=-1, keepdims=True)
    total = carry_ref[0:1, :] + jnp.sum(oh, axis=0, keepdims=True)
    carry_ref[...] = jnp.broadcast_to(total, carry_ref.shape)
    cnt_ref[...] = jnp.broadcast_to(total, cnt_ref.shape)
    route = jnp.where(lane == 0, e0, 0.0)
    route = jnp.where(lane == 1, e1, route)
    route = jnp.where(lane == 2, rank0, route)
    route = jnp.where(lane == 3, rank1, route)
    route = jnp.where(lane == 4, p0, route)
    route = jnp.where(lane == 5, p1, route)
    route_ref[...] = route


def _router(logits, b_r):
    n_lat_tiles = B * SEQ // TM
    return pl.pallas_call(
        _router_kernel,
        grid=(n_lat_tiles,),
        in_specs=[
            pl.BlockSpec((TM, LANES), lambda t: (_lat_tile(t), 0)),
            pl.BlockSpec((1, LANES), lambda t: (0, 0)),
        ],
        out_specs=[
            pl.BlockSpec((TM, LANES), lambda t: (t, 0)),
            pl.BlockSpec((8, LANES), lambda t: (0, 0)),
        ],
        out_shape=[jax.ShapeDtypeStruct((B * SEQ, LANES), F32),
                   jax.ShapeDtypeStruct((8, LANES), F32)],
        scratch_shapes=[pltpu.VMEM((8, LANES), F32)],
        compiler_params=_cparams(("arbitrary",)),
        name="router",
    )(logits, b_r)


ISSUE_UNROLL = 8


def _slab_copy(src_hbm, tok, dst, r, sem):
    return pltpu.make_async_copy(src_hbm.at[pl.ds(pl.multiple_of(tok * SLAB, SLAB), SLAB), :],
                                 dst.at[pl.ds(r * SLAB_PITCH, SLAB), :], sem)


def _slab_wait_all(src_hbm, dst, sem):
    pltpu.make_async_copy(src_hbm.at[pl.ds(0, TM * SLAB), :], dst.at[pl.ds(0, TM * SLAB), :], sem).wait()


def _gather_kernel(live_ref, idx_ref, idx_next_ref, h_hbm, o_ref, buf_ref, sems):
    i = pl.program_id(0)
    n = pl.num_programs(0)
    live = live_ref[i]
    nxt = jnp.minimum(i + 1, n - 1)
    start_next = jnp.logical_and(i + 1 < n, live_ref[nxt] != 0)

    def issue(ids_ref, s):
        def body(g, carry):
            for u in range(ISSUE_UNROLL):
                r = g * ISSUE_UNROLL + u
                _slab_copy(h_hbm, ids_ref[0, 0, r], buf_ref.at[s], r, sems.at[s]).start(priority=u % 2)
            return carry

        lax.fori_loop(0, TM // ISSUE_UNROLL, body, 0)

    def finish(s):
        _slab_wait_all(h_hbm, buf_ref.at[s], sems.at[s])
        for cb in range(SLAB):
            o_ref[:, cb * LANES:(cb + 1) * LANES] = (
                buf_ref[s, pl.ds(cb, TM, stride=SLAB_PITCH), :].astype(BF16))

    @pl.when(jnp.logical_and(i == 0, live != 0))
    def _():
        issue(idx_ref, 0)

    for s in range(2):
        @pl.when(jnp.logical_and(start_next, (i + 1) % 2 == s))
        def _(s=s):
            issue(idx_next_ref, s)

    @pl.when(live == 0)
    def _():
        o_ref[...] = jnp.zeros_like(o_ref)

    for s in range(2):
        @pl.when(jnp.logical_and(live != 0, i % 2 == s))
        def _(s=s):
            finish(s)


def _gather(h2_slabs, src, live):
    n = src.shape[0] // TM
    ids = src.reshape(n, 1, TM)
    grid_spec = pltpu.PrefetchScalarGridSpec(
        num_scalar_prefetch=1,
        grid=(n,),
        in_specs=[
            pl.BlockSpec((1, 1, TM), lambda i, live: (i, 0, 0), memory_space=pltpu.SMEM),
            pl.BlockSpec((1, 1, TM), lambda i, live: (jnp.minimum(i + 1, n - 1), 0, 0),
                         memory_space=pltpu.SMEM),
            pl.BlockSpec(memory_space=pl.ANY),
        ],
        out_specs=pl.BlockSpec((TM, D), lambda i, live: (i, 0)),
        scratch_shapes=[pltpu.VMEM((2, TM * SLAB_PITCH, LANES), F32), pltpu.SemaphoreType.DMA((2,))],
    )
    return pl.pallas_call(
        _gather_kernel,
        grid_spec=grid_spec,
        out_shape=jax.ShapeDtypeStruct((src.shape[0], D), BF16),
        compiler_params=_cparams(("arbitrary",)),
        name="dispatch",
    )(live, ids, ids, h2_slabs)


def _combine_kernel(p0_ref, p1_ref, p0n_ref, p1n_ref, x_ref, route_ref, mod_ref, fg_ref, y_hbm, o_ref,
                    buf_ref, sems):
    t = pl.program_id(0)
    n = pl.num_programs(0)

    def issue(q0_ref, q1_ref, s):
        def body(g, carry):
            for u in range(ISSUE_UNROLL):
                r = g * ISSUE_UNROLL + u
                for k, q_ref in enumerate((q0_ref, q1_ref)):
                    pltpu.make_async_copy(y_hbm.at[pl.ds(q_ref[0, 0, r], 1), :],
                                          buf_ref.at[s, k, pl.ds(r, 1), :], sems.at[s, k]).start(priority=k)
            return carry

        lax.fori_loop(0, TM // ISSUE_UNROLL, body, 0)

    def finish(s):
        for k in range(2):
            pltpu.make_async_copy(y_hbm.at[pl.ds(0, TM), :], buf_ref.at[s, k], sems.at[s, k]).wait()
        route = route_ref[...]
        y = route[:, 4:5] * buf_ref[s, 0] + route[:, 5:6] * buf_ref[s, 1]
        xn = x_ref[...] + mod_ref[0][:, 5 * D:6 * D] * y
        o_ref[...] = _rms(xn) * fg_ref[...]

    @pl.when(t == 0)
    def _():
        issue(p0_ref, p1_ref, 0)

    for s in range(2):
        @pl.when(jnp.logical_and(t + 1 < n, (t + 1) % 2 == s))
        def _(s=s):
            issue(p0n_ref, p1n_ref, s)

    for s in range(2):
        @pl.when(t % 2 == s)
        def _(s=s):
            finish(s)


def _combine(x, ys, pos0, pos1, route, mod3, fg):
    n = B * SEQ // TM
    smem = lambda: pl.BlockSpec((1, 1, TM), lambda t: (t, 0, 0), memory_space=pltpu.SMEM)
    smem_next = lambda: pl.BlockSpec((1, 1, TM), lambda t: (jnp.minimum(t + 1, n - 1), 0, 0),
                                     memory_space=pltpu.SMEM)
    p0 = pos0.reshape(n, 1, TM)
    p1 = pos1.reshape(n, 1, TM)
    return pl.pallas_call(
        _combine_kernel,
        grid=(n,),
        in_specs=[
            smem(), smem(), smem_next(), smem_next(),
            pl.BlockSpec((TM, D), lambda t: (_lat_tile(t), 0)),
            pl.BlockSpec((TM, LANES), lambda t: (t, 0)),
            pl.BlockSpec((1, 1, 6 * D), lambda t: (t // (TPB - 1), 0, 0)),
            pl.BlockSpec((1, D), lambda t: (0, 0)),
            pl.BlockSpec(memory_space=pl.ANY),
        ],
        out_specs=pl.BlockSpec((TM, D), lambda t: (t, 0)),
        out_shape=jax.ShapeDtypeStruct((B * SEQ, D), F32),
        scratch_shapes=[pltpu.VMEM((2, 2, TM, D), F32), pltpu.SemaphoreType.DMA((2, 2))],
        compiler_params=_cparams(("arbitrary",)),
        name="combine",
    )(p0, p1, p0, p1, x, route, mod3, fg, ys)


def _moe_plan(route, counts):
    cnt = counts[0, :NE].astype(jnp.int32)
    nsub = (cnt + SUB_MOE - 1) // SUB_MOE
    ntile = (nsub + SUBS_MOE - 1) // SUBS_MOE
    tile_end = jnp.cumsum(ntile)
    tile_start = tile_end - ntile

    def slot_of(k):
        e_k = route[:, k].astype(jnp.int32)
        first = jnp.zeros_like(e_k)
        for ex in range(NE):
            first = jnp.where(e_k == ex, tile_start[ex] * TM_MOE, first)
        return first + route[:, 2 + k].astype(jnp.int32)

    pos0, pos1 = slot_of(0), slot_of(1)
    n_used = tile_end[NE - 1]
    tiles = jnp.arange(NT_MOE, dtype=jnp.int32)
    blk = jnp.minimum(tiles, n_used - 1)
    te = jnp.sum((blk[:, None] >= tile_end[None, :NE - 1]).astype(jnp.int32), axis=1)
    ns = jnp.clip(nsub[te] - (blk - tile_start[te]) * SUBS_MOE, 0, SUBS_MOE)
    ns = jnp.where(tiles < n_used, ns, 0).astype(jnp.int32)
    tok = jnp.arange(B * SEQ, dtype=jnp.int32)
    tok_row = tok + (tok // SEQ + 1) * CTX
    src = jnp.full((R_MOE,), CTX, jnp.int32).at[jnp.concatenate([pos0, pos1])].set(
        jnp.concatenate([tok_row, tok_row]), unique_indices=True)
    first_row = jnp.arange(0, TM_MOE, TM, dtype=jnp.int32)
    live = (first_row[None, :] < ns[:, None] * SUB_MOE).astype(jnp.int32).reshape(-1)
    return pos0, pos1, te, ns, blk, src, live


def kernel(x, c, ctx, c_ctx, w_ada, b_ada, norm1_g, norm2_g, w_in, w_gate, b_gate, gla_norm_g,
           w_pool, pool_scale, w_out, ffn_w1, ffn_w3, ffn_w2, moe_w_router, moe_b_router,
           moe_w1, moe_w3, moe_w2, final_g):
    assert x.shape == (B, SEQ, D) and ctx.shape == (B, CTX, D) and DEPTH == 2

    cvec = jnp.concatenate([c, c_ctx[None, :], jnp.zeros((8 - B - 1, D), F32)], axis=0)
    mods = _ada(cvec, w_ada, b_ada)
    xs = (x, ctx)

    w_in_t = jnp.swapaxes(w_in, 1, 2)
    w_r_t = jnp.pad(w_in_t[:, IN_MAIN:, :], ((0, 0), (0, LANES - 2 * RANK), (0, 0)))
    w_out_b = w_out.astype(BF16)
    w_pool_b = w_pool.astype(BF16)

    band_np, inv_np = _pool_tables()
    band = jnp.asarray(band_np, BF16)
    inv = jnp.asarray(inv_np, F32)

    dense_te = jnp.zeros((NT // TM_DENSE,), jnp.int32)
    dense_ns = jnp.full((NT // TM_DENSE,), TM_DENSE // SUB_DENSE, jnp.int32)
    dense_blk = jnp.arange(NT // TM_DENSE, dtype=jnp.int32)

    out = None
    for l in range(DEPTH):
        mod3 = mods[l, :3].reshape(3, 1, 6 * D)
        wg = jnp.zeros((2, LANES, KEYW), F32)
        wg = wg.at[0, 0:RANK].set(w_gate[l, 0]).at[1, RANK:2 * RANK].set(w_gate[l, 1]).astype(BF16)
        bg = b_gate[l].reshape(2, 1, KEYW)

        qk, vgp, r = _inproj(xs, mod3, norm1_g[l].reshape(1, D), w_in_t, w_r_t, l)
        o_f, o_b = _gla(qk, vgp, r, wg, bg)
        mix_args = (xs, o_f, o_b, vgp, mod3, norm2_g[l].reshape(1, D), gla_norm_g[l].reshape(1, GLA_W),
                    band, inv, w_pool_b[l], pool_scale[l].reshape(1, POOL_W), w_out_b, l)
        if l % 2 == 0:
            x1, h2 = _mix(*mix_args)
            jl = l // 2
            ys = _ffn(h2, dense_te, dense_ns, dense_blk, ffn_w1[:, None], ffn_w3[:, None],
                      ffn_w2[:, None], jl, TM_DENSE, SUB_DENSE, TF_DENSE,
                      straight=(TM_DENSE // SUB_DENSE,))
            xs = (x1, ys, mod3)
        else:
            jl = l // 2
            wr = jnp.concatenate(_split2(jnp.pad(moe_w_router[jl], ((0, 0), (0, LANES - NE)))), axis=1)
            br = jnp.pad(moe_b_router[jl], (0, LANES - NE)).reshape(1, LANES)
            x1, h2, logits = _mix(*mix_args, wr=wr)
            route, counts = _router(logits, br)
            pos0, pos1, te, ns, blk, src, live = _moe_plan(route, counts)
            xg = _gather(h2, src, live)
            ys = _ffn(xg, te, ns, blk, moe_w1, moe_w3, moe_w2, jl, TM_MOE, SUB_MOE, TF_MOE,
                      straight=STRAIGHT_MOE)
            out = _combine(x1, ys, pos0, pos1, route, mod3, final_g.reshape(1, D))
    return out.reshape(B, SEQ, D)
```
